```python
import math
import jax, jax.numpy as jnp
from jax import lax
import numpy as np

D_MODEL = 2048
BATCH = 4
SEQ = 4096
DEPTH = 2

MEM_LEN = 256
CONV_WIDTH = D_MODEL // 4
CONV_KERNEL = 31
HEAD_DIM = 128
ATTN_WIDTH = D_MODEL // 2
ATTN_HEADS = ATTN_WIDTH // HEAD_DIM
SB_BLOCK = 128
LRU_WIDTH = D_MODEL // 4
LRU_HEAD_DIM = 128
LRU_HEADS = LRU_WIDTH // LRU_HEAD_DIM
LRU_CONV_KERNEL = 4
LRU_C = 8.0
MIX_WIDTH = CONV_WIDTH + ATTN_WIDTH + LRU_WIDTH
XATTN_HEADS = 4
XATTN_HEAD_DIM = 128
XATTN_WIDTH = XATTN_HEADS * XATTN_HEAD_DIM
IN_SIZES = (CONV_WIDTH, CONV_WIDTH, CONV_WIDTH,
            ATTN_WIDTH, ATTN_WIDTH, ATTN_WIDTH, ATTN_WIDTH,
            LRU_WIDTH, LRU_WIDTH)
IN_WIDTH = 3 * CONV_WIDTH + 4 * ATTN_WIDTH + 2 * LRU_WIDTH

kernel_name = "hymba_style_conv_stickbreak_rglru_trunk"


def rms_norm(x, g, eps=1e-6):
    xf = x.astype(jnp.float32)
    y = xf * lax.rsqrt(jnp.mean(xf * xf, axis=-1, keepdims=True) + eps)
    return (y * g.astype(jnp.float32)).astype(x.dtype)


def layer_norm(x, g, b, eps=1e-5):
    xf = x.astype(jnp.float32)
    mu = jnp.mean(xf, axis=-1, keepdims=True)
    var = jnp.mean(jnp.square(xf - mu), axis=-1, keepdims=True)
    y = (xf - mu) * lax.rsqrt(var + eps)
    return (y * g.astype(jnp.float32) + b.astype(jnp.float32)).astype(x.dtype)


def causal_depthwise_conv(x, w, b):
    K, C = w.shape
    y = lax.conv_general_dilated(
        x, w[:, None, :].astype(x.dtype), window_strides=(1,),
        padding=((K - 1, 0),), dimension_numbers=("NWC", "WIO", "NWC"),
        feature_group_count=C)
    return y + b.astype(x.dtype)


def conformer_conv(val, glu_gate, dw_w, dw_b, ln_g, ln_b, pw_w):
    u = val * jax.nn.sigmoid(glu_gate)
    u = causal_depthwise_conv(u, dw_w, dw_b)
    u = jax.nn.silu(layer_norm(u, ln_g, ln_b))
    return u @ pw_w


def stick_breaking_attention(q, k, v):
    S = q.shape[2]
    scale = HEAD_DIM ** -0.5
    outs = []
    for qb in range(S // SB_BLOCK):
        start = qb * SB_BLOCK
        end = start + SB_BLOCK
        z = jnp.einsum("bhqd,bhkd->bhqk", q[:, :, start:end], k[:, :, :end]).astype(jnp.float32) * scale
        t_pos = start + jnp.arange(SB_BLOCK)
        s_pos = jnp.arange(end)
        causal = s_pos[None, :] < t_pos[:, None]
        log_1mb = jnp.where(causal, jax.nn.log_sigmoid(-z), 0.0)
        after = jnp.sum(log_1mb, axis=-1, keepdims=True) - jnp.cumsum(log_1mb, axis=-1)
        w = jnp.where(causal, jnp.exp(jax.nn.log_sigmoid(z) + after), 0.0)
        outs.append(jnp.einsum("bhqk,bhkd->bhqd", w.astype(v.dtype), v[:, :, :end]))
    return jnp.concatenate(outs, axis=2)


def rg_lru(xc, wa, ba, wx, bx, lam):
    B, S, W = xc.shape
    xh = xc.reshape(B, S, LRU_HEADS, LRU_HEAD_DIM)
    r = jax.nn.sigmoid(jnp.einsum("bsnd,nde->bsne", xh, wa).reshape(B, S, W) + ba)
    i = jax.nn.sigmoid(jnp.einsum("bsnd,nde->bsne", xh, wx).reshape(B, S, W) + bx)
    log_a = -LRU_C * r.astype(jnp.float32) * jax.nn.softplus(-lam.astype(jnp.float32))
    a = jnp.exp(log_a)
    mult = jnp.sqrt(-jnp.expm1(2.0 * log_a))
    b = mult * (i * xc).astype(jnp.float32)

    def combine(left, right):
        a1, b1 = left
        a2, b2 = right
        return a1 * a2, a2 * b1 + b2

    _, h = lax.associative_scan(combine, (a, b), axis=1)
    return h.astype(xc.dtype)


def hybrid_mixer(h, w_in, conv_dw_w, conv_dw_b, conv_ln_g, conv_ln_b, conv_pw_w,
                 lru_conv_w, lru_conv_b, lru_wa, lru_ba, lru_wx, lru_bx, lru_lambda,
                 out_norm_conv, out_norm_attn, out_norm_lru, w_out):
    B, S, _ = h.shape
    u = h @ w_in
    points = []
    acc = 0
    for size in IN_SIZES[:-1]:
        acc += size
        points.append(acc)
    c_val, c_glu, c_gate, q, k, v, a_gate, r_x, r_gate = jnp.split(u, points, axis=-1)

    y_conv = conformer_conv(c_val, c_glu, conv_dw_w, conv_dw_b, conv_ln_g, conv_ln_b, conv_pw_w)

    def heads(t):
        return t.reshape(B, S, ATTN_HEADS, HEAD_DIM).transpose(0, 2, 1, 3)
    y_attn = stick_breaking_attention(heads(q), heads(k), heads(v))
    y_attn = y_attn.transpose(0, 2, 1, 3).reshape(B, S, ATTN_WIDTH)

    xc = causal_depthwise_conv(r_x, lru_conv_w, lru_conv_b)
    y_lru = rg_lru(xc, lru_wa, lru_ba, lru_wx, lru_bx, lru_lambda)

    y = jnp.concatenate([
        rms_norm(y_conv, out_norm_conv) * jax.nn.silu(c_gate),
        rms_norm(y_attn, out_norm_attn) * jax.nn.silu(a_gate),
        rms_norm(y_lru, out_norm_lru) * jax.nn.silu(r_gate),
    ], axis=-1)
    return y @ w_out


def memory_cross_attention(h, memn, wq, wkv, wo):
    B, S, _ = h.shape
    M = memn.shape[1]
    q = (h @ wq).reshape(B, S, XATTN_HEADS, XATTN_HEAD_DIM)
    k, v = jnp.split(memn @ wkv, 2, axis=-1)
    k = k.reshape(B, M, XATTN_HEADS, XATTN_HEAD_DIM)
    v = v.reshape(B, M, XATTN_HEADS, XATTN_HEAD_DIM)
    s = jnp.einsum("bqhd,bkhd->bhqk", q, k).astype(jnp.float32) * (XATTN_HEAD_DIM ** -0.5)
    p = jax.nn.softmax(s, axis=-1).astype(v.dtype)
    o = jnp.einsum("bhqk,bkhd->bqhd", p, v).reshape(B, S, XATTN_WIDTH)
    return o @ wo


def setup_inputs(seed: int = 0) -> dict:
    key = jax.random.key(seed)
    ks = iter(jax.random.split(key, 32))
    f32 = jnp.float32

    def normal(shape, scale):
        return jax.random.normal(next(ks), shape, f32) * scale

    def gain(shape):
        return 1.0 + normal(shape, 0.02)

    u = jax.random.uniform(next(ks), (DEPTH, LRU_WIDTH), f32, minval=0.9, maxval=0.999)
    a0 = u ** (1.0 / LRU_C)
    lru_lambda = jnp.log(a0) - jnp.log1p(-a0)

    return {
        "x": normal((BATCH, SEQ, D_MODEL), 1.0),
        "mem": normal((BATCH, MEM_LEN, D_MODEL), 1.0),
        "mix_norm_g": gain((DEPTH, D_MODEL)),
        "w_in": normal((DEPTH, D_MODEL, IN_WIDTH), D_MODEL ** -0.5),
        "conv_dw_w": normal((DEPTH, CONV_KERNEL, CONV_WIDTH), CONV_KERNEL ** -0.5),
        "conv_dw_b": normal((DEPTH, CONV_WIDTH), 0.01),
        "conv_ln_g": gain((DEPTH, CONV_WIDTH)),
        "conv_ln_b": normal((DEPTH, CONV_WIDTH), 0.01),
        "conv_pw_w": normal((DEPTH, CONV_WIDTH, CONV_WIDTH), CONV_WIDTH ** -0.5),
        "lru_conv_w": normal((DEPTH, LRU_CONV_KERNEL, LRU_WIDTH), LRU_CONV_KERNEL ** -0.5),
        "lru_conv_b": normal((DEPTH, LRU_WIDTH), 0.01),
        "lru_wa": normal((DEPTH, LRU_HEADS, LRU_HEAD_DIM, LRU_HEAD_DIM), LRU_HEAD_DIM ** -0.5),
        "lru_ba": normal((DEPTH, LRU_WIDTH), 0.01),
        "lru_wx": normal((DEPTH, LRU_HEADS, LRU_HEAD_DIM, LRU_HEAD_DIM), LRU_HEAD_DIM ** -0.5),
        "lru_bx": normal((DEPTH, LRU_WIDTH), 0.01),
        "lru_lambda": lru_lambda,
        "out_norm_conv": gain((DEPTH, CONV_WIDTH)),
        "out_norm_attn": gain((DEPTH, ATTN_WIDTH)),
        "out_norm_lru": gain((DEPTH, LRU_WIDTH)),
        "w_out": normal((DEPTH, MIX_WIDTH, D_MODEL), MIX_WIDTH ** -0.5),
        "xattn_norm_g": gain((DEPTH, D_MODEL)),
        "mem_norm_g": gain((DEPTH, D_MODEL)),
        "xattn_wq": normal((DEPTH, D_MODEL, XATTN_WIDTH), D_MODEL ** -0.5),
        "xattn_wkv": normal((DEPTH, D_MODEL, 2 * XATTN_WIDTH), D_MODEL ** -0.5),
        "xattn_wo": normal((DEPTH, XATTN_WIDTH, D_MODEL), XATTN_WIDTH ** -0.5),
        "final_norm_g": gain((D_MODEL,)),
    }


def reference(x, mem, mix_norm_g, w_in, conv_dw_w, conv_dw_b, conv_ln_g, conv_ln_b, conv_pw_w,
              lru_conv_w, lru_conv_b, lru_wa, lru_ba, lru_wx, lru_bx, lru_lambda,
              out_norm_conv, out_norm_attn, out_norm_lru, w_out,
              xattn_norm_g, mem_norm_g, xattn_wq, xattn_wkv, xattn_wo, final_norm_g):
    for l in range(DEPTH):
        h = rms_norm(x, mix_norm_g[l])
        x = x + hybrid_mixer(h, w_in[l], conv_dw_w[l], conv_dw_b[l], conv_ln_g[l], conv_ln_b[l],
                             conv_pw_w[l], lru_conv_w[l], lru_conv_b[l], lru_wa[l], lru_ba[l],
                             lru_wx[l], lru_bx[l], lru_lambda[l], out_norm_conv[l],
                             out_norm_attn[l], out_norm_lru[l], w_out[l])
        h = rms_norm(x, xattn_norm_g[l])
        memn = rms_norm(mem, mem_norm_g[l])
        x = x + memory_cross_attention(h, memn, xattn_wq[l], xattn_wkv[l], xattn_wo[l])
    return rms_norm(x, final_norm_g)
```

```python
import functools

import jax
import jax.numpy as jnp
from jax import lax
from jax.experimental import pallas as pl
from jax.experimental.pallas import tpu as pltpu

F32 = jnp.float32
BF16 = jnp.bfloat16

D_MODEL = 2048
CONV_WIDTH = 512
CONV_KERNEL = 31
HEAD_DIM = 128
ATTN_WIDTH = 1024
ATTN_HEADS = 8
LRU_WIDTH = 512
LRU_HEADS = 4
LRU_CONV_KERNEL = 4
LRU_C = 8.0
XATTN_HEADS = 4
XATTN_WIDTH = 512
IN_WIDTH = 3 * CONV_WIDTH + 4 * ATTN_WIDTH + 2 * LRU_WIDTH
COL_CVAL, COL_CGLU, COL_CGATE = 0, 512, 1024
COL_Q, COL_K, COL_V, COL_AGATE = 1536, 2560, 3584, 4608
COL_RX, COL_RGATE = 5632, 6144

VMEM_LIMIT_BYTES = 56 * 1024 * 1024
SUBLANES = 8

RMS_EPS = 1e-6
LN_EPS = 1e-5


def _params(*sem):
    return pltpu.CompilerParams(dimension_semantics=sem, vmem_limit_bytes=VMEM_LIMIT_BYTES)


def _rms_scale(x):
    return lax.rsqrt(jnp.mean(x * x, axis=-1, keepdims=True) + RMS_EPS)


def _softplus(x):
    return jnp.maximum(x, 0.0) + jnp.log(1.0 + jnp.exp(-jnp.abs(x)))


def _silu(x):
    return x * jax.nn.sigmoid(x)


def _norm_matmul_kernel(x_ref, g_ref, w_ref, o_ref, h_ref):
    @pl.when(pl.program_id(1) == 0)
    def _():
        x = x_ref[...]
        h_ref[...] = (x * _rms_scale(x) * g_ref[...]).astype(BF16)

    o_ref[...] = jnp.dot(h_ref[...], w_ref[...], preferred_element_type=F32).astype(o_ref.dtype)


def _norm_matmul(x, g, w, *, tm, tn, name):
    t, d = x.shape
    n = w.shape[1]
    return pl.pallas_call(
        _norm_matmul_kernel,
        out_shape=jax.ShapeDtypeStruct((t, n), BF16),
        grid=(t // tm, n // tn),
        in_specs=[
            pl.BlockSpec((tm, d), lambda i, j: (i, 0)),
            pl.BlockSpec((1, d), lambda i, j: (0, 0)),
            pl.BlockSpec((d, tn), lambda i, j: (0, j)),
        ],
        out_specs=pl.BlockSpec((tm, tn), lambda i, j: (i, j)),
        scratch_shapes=[pltpu.VMEM((tm, d), BF16)],
        compiler_params=_params("parallel", "arbitrary"),
        name=name,
    )(x, g, w)


CONV_HALO = 32
CONV_ROW_CHUNK = 32


def _conv_branch_kernel(val_ref, glu_ref, dww_ref, dwb_ref, lng_ref, lnb_ref, pw_ref, o_ref,
                        buf_ref, dw_ref):
    ts = o_ref.shape[0]

    @pl.when(pl.program_id(1) == 0)
    def _():
        buf_ref[0:CONV_HALO, :] = jnp.zeros((CONV_HALO, CONV_WIDTH), F32)

    val = val_ref[...].astype(F32)
    glu = glu_ref[...].astype(F32)
    buf_ref[CONV_HALO:CONV_HALO + ts, :] = val * jax.nn.sigmoid(glu)

    base = CONV_HALO - (CONV_KERNEL - 1)
    for c in range(ts // CONV_ROW_CHUNK):
        r0 = c * CONV_ROW_CHUNK
        acc = jnp.broadcast_to(dwb_ref[...], (CONV_ROW_CHUNK, CONV_WIDTH))
        for k in range(CONV_KERNEL):
            acc = acc + dww_ref[k:k + 1, :] * buf_ref[r0 + base + k:r0 + base + k + CONV_ROW_CHUNK, :]
        dw_ref[r0:r0 + CONV_ROW_CHUNK, :] = acc

    buf_ref[0:CONV_HALO, :] = buf_ref[ts:ts + CONV_HALO, :]

    u = dw_ref[...]
    mu = jnp.mean(u, axis=-1, keepdims=True)
    uc = u - mu
    var = jnp.mean(uc * uc, axis=-1, keepdims=True)
    y = uc * lax.rsqrt(var + LN_EPS) * lng_ref[...] + lnb_ref[...]
    y = _silu(y).astype(BF16)
    o_ref[...] = jnp.dot(y, pw_ref[...], preferred_element_type=F32).astype(o_ref.dtype)


def _conv_branch(u3, dww, dwb, lng, lnb, pw, *, ts):
    b, s, _ = u3.shape
    cw = CONV_WIDTH
    vec = pl.BlockSpec((1, cw), lambda bi, ti: (0, 0))
    return pl.pallas_call(
        _conv_branch_kernel,
        out_shape=jax.ShapeDtypeStruct((b, s, cw), BF16),
        grid=(b, s // ts),
        in_specs=[
            pl.BlockSpec((None, ts, cw), lambda bi, ti: (bi, ti, COL_CVAL // cw)),
            pl.BlockSpec((None, ts, cw), lambda bi, ti: (bi, ti, COL_CGLU // cw)),
            pl.BlockSpec((CONV_KERNEL, cw), lambda bi, ti: (0, 0)),
            vec, vec, vec,
            pl.BlockSpec((cw, cw), lambda bi, ti: (0, 0)),
        ],
        out_specs=pl.BlockSpec((None, ts, cw), lambda bi, ti: (bi, ti, 0)),
        scratch_shapes=[pltpu.VMEM((ts + CONV_HALO, cw), F32), pltpu.VMEM((ts, cw), F32)],
        compiler_params=_params("parallel", "arbitrary"),
        name="conv_branch",
    )(u3, u3, dww, dwb, lng, lnb, pw)


SB_TQ = 256
SB_TK = 128


def _sb_attn_kernel(q_ref, k_ref, v_ref, uo_ref, o_ref, acc_ref, carry_ref):
    s_len = q_ref.shape[0]
    scale = HEAD_DIM ** -0.5
    diag_blocks = SB_TQ // SB_TK

    def q_tile(i, _):
        q0 = pl.multiple_of(i * SB_TQ, SB_TQ)
        q = q_ref[pl.ds(q0, SB_TQ), :]
        acc_ref[...] = jnp.zeros_like(acc_ref)
        carry_ref[...] = jnp.zeros_like(carry_ref)

        def block(j, masked):
            k0 = pl.multiple_of(j * SB_TK, SB_TK)
            k = k_ref[pl.ds(k0, SB_TK), :]
            v = v_ref[pl.ds(k0, SB_TK), :]
            z = lax.dot_general(q, k, (((1,), (1,)), ((), ())), preferred_element_type=F32) * scale
            sp = _softplus(z)
            log_1mb = -sp
            if masked:
                t_pos = q0 + lax.broadcasted_iota(jnp.int32, (SB_TQ, SB_TK), 0)
                s_pos = k0 + lax.broadcasted_iota(jnp.int32, (SB_TQ, SB_TK), 1)
                causal = s_pos < t_pos
                log_1mb = jnp.where(causal, log_1mb, 0.0)
            hi = log_1mb.astype(BF16)
            lo = (log_1mb - hi.astype(F32)).astype(BF16)
            uo = uo_ref[...]
            r = (jnp.dot(hi, uo, preferred_element_type=F32)
                 + jnp.dot(lo, uo, preferred_element_type=F32))
            after = carry_ref[...] + r[:, :SB_TK]
            w = jnp.exp((z - sp) + after)
            if masked:
                w = jnp.where(causal, w, 0.0)
            acc_ref[...] += jnp.dot(w.astype(BF16), v, preferred_element_type=F32)
            carry_ref[...] += r[:, SB_TK:]

        last = (i + 1) * diag_blocks - 1
        for d in range(diag_blocks):
            block(last - d, True)

        def off_diag(jj, _):
            block(i * diag_blocks - 1 - jj, False)
            return 0

        lax.fori_loop(0, i * diag_blocks, off_diag, 0)
        o_ref[pl.ds(q0, SB_TQ), :] = acc_ref[...].astype(o_ref.dtype)
        return 0

    lax.fori_loop(0, s_len // SB_TQ, q_tile, 0)


def _sb_attention(u3):
    b, s, _ = u3.shape
    dh = HEAD_DIM
    row = lax.broadcasted_iota(jnp.int32, (SB_TK, 2 * SB_TK), 0)
    col = lax.broadcasted_iota(jnp.int32, (SB_TK, 2 * SB_TK), 1)
    uo = jnp.where((col >= SB_TK) | (row > col), 1.0, 0.0).astype(BF16)

    def head_spec(col0):
        return pl.BlockSpec((None, s, dh), lambda bi, hi: (bi, 0, col0 // dh + hi))

    return pl.pallas_call(
        _sb_attn_kernel,
        out_shape=jax.ShapeDtypeStruct((b, s, ATTN_WIDTH), BF16),
        grid=(b, ATTN_HEADS),
        in_specs=[head_spec(COL_Q), head_spec(COL_K), head_spec(COL_V),
                  pl.BlockSpec((SB_TK, 2 * SB_TK), lambda bi, hi: (0, 0))],
        out_specs=pl.BlockSpec((None, s, dh), lambda bi, hi: (bi, 0, hi)),
        scratch_shapes=[pltpu.VMEM((SB_TQ, dh), F32), pltpu.VMEM((SB_TQ, SB_TK), F32)],
        compiler_params=_params("parallel", "parallel"),
        name="sb_attention",
    )(u3, u3, u3, uo)


LRU_HALO = SUBLANES


def _lru_kernel(x_ref, cw_ref, cb_ref, wa_ref, ba_ref, wx_ref, bx_ref, lam_ref, o_ref,
                buf_ref, a_ref, b_ref, h_ref):
    ts = o_ref.shape[0]
    w = LRU_WIDTH

    @pl.when(pl.program_id(1) == 0)
    def _():
        buf_ref[0:LRU_HALO, :] = jnp.zeros((LRU_HALO, w), F32)
        h_ref[...] = jnp.zeros_like(h_ref)

    buf_ref[LRU_HALO:LRU_HALO + ts, :] = x_ref[...].astype(F32)
    base = LRU_HALO - (LRU_CONV_KERNEL - 1)
    xc = jnp.broadcast_to(cb_ref[...], (ts, w))
    for k in range(LRU_CONV_KERNEL):
        xc = xc + cw_ref[k:k + 1, :] * buf_ref[base + k:base + k + ts, :]
    buf_ref[0:LRU_HALO, :] = buf_ref[ts:ts + LRU_HALO, :]

    xcb = xc.astype(BF16)
    hd = w // LRU_HEADS
    rs, gs = [], []
    for n in range(LRU_HEADS):
        xh = xcb[:, n * hd:(n + 1) * hd]
        rs.append(jnp.dot(xh, wa_ref[n], preferred_element_type=F32))
        gs.append(jnp.dot(xh, wx_ref[n], preferred_element_type=F32))
    r = jax.nn.sigmoid(jnp.concatenate(rs, axis=-1) + ba_ref[...])
    gate = jax.nn.sigmoid(jnp.concatenate(gs, axis=-1) + bx_ref[...])
    log_a = (-LRU_C) * r * _softplus(-lam_ref[...])
    a = jnp.exp(log_a)
    x2 = 2.0 * log_a
    e2 = jnp.exp(x2)
    near = (1.0 - e2) * x2 / jnp.log(e2)
    one_m = jnp.where(x2 > -0.5, jnp.where(e2 == 1.0, -x2, near), 1.0 - e2)
    a_ref[...] = a
    b_ref[...] = jnp.sqrt(one_m) * (gate * xc)

    row = lax.broadcasted_iota(jnp.int32, (SUBLANES, w), 0)

    def group(g, h_prev):
        r0 = pl.multiple_of(g * SUBLANES, SUBLANES)
        ag = a_ref[pl.ds(r0, SUBLANES), :]
        bg = b_ref[pl.ds(r0, SUBLANES), :]
        d = 1
        while d < SUBLANES:
            a_sh = pltpu.roll(ag, d, 0)
            b_sh = pltpu.roll(bg, d, 0)
            m = row >= d
            bg = jnp.where(m, ag * b_sh + bg, bg)
            ag = jnp.where(m, ag * a_sh, ag)
            d *= 2
        h = ag * h_prev + bg
        b_ref[pl.ds(r0, SUBLANES), :] = h
        return jnp.broadcast_to(h[SUBLANES - 1:SUBLANES, :], (SUBLANES, w))

    h_last = lax.fori_loop(0, ts // SUBLANES, group, h_ref[...], unroll=4)
    h_ref[...] = h_last
    o_ref[...] = b_ref[...].astype(o_ref.dtype)


def _lru_branch(u3, cw, cb, wa, ba, wx, bx, lam, *, ts):
    b, s, _ = u3.shape
    w = LRU_WIDTH
    hd = w // LRU_HEADS
    vec = pl.BlockSpec((1, w), lambda bi, ti: (0, 0))
    mat = pl.BlockSpec((LRU_HEADS, hd, hd), lambda bi, ti: (0, 0, 0))
    return pl.pallas_call(
        _lru_kernel,
        out_shape=jax.ShapeDtypeStruct((b, s, w), BF16),
        grid=(b, s // ts),
        in_specs=[
            pl.BlockSpec((None, ts, w), lambda bi, ti: (bi, ti, COL_RX // w)),
            pl.BlockSpec((LRU_CONV_KERNEL, w), lambda bi, ti: (0, 0)),
            vec, mat, vec, mat, vec, vec,
        ],
        out_specs=pl.BlockSpec((None, ts, w), lambda bi, ti: (bi, ti, 0)),
        scratch_shapes=[pltpu.VMEM((ts + LRU_HALO, w), F32), pltpu.VMEM((ts, w), F32),
                        pltpu.VMEM((ts, w), F32), pltpu.VMEM((SUBLANES, w), F32)],
        compiler_params=_params("parallel", "arbitrary"),
        name="lru_branch",
    )(u3, cw, cb, wa, ba, wx, bx, lam)


def _out_proj_kernel(x_ref, yc_ref, ya_ref, yl_ref, gc_ref, ga0_ref, ga1_ref, gl_ref,
                     nc_ref, na_ref, nl_ref, w_ref, o_ref):
    def normed(y_ref, n_ref):
        y = y_ref[...].astype(F32)
        return y * _rms_scale(y) * n_ref[...]

    def gated(yn, gate_ref):
        return (yn * _silu(gate_ref[...].astype(F32))).astype(BF16)

    half = ATTN_WIDTH // 2
    ya = normed(ya_ref, na_ref)
    parts = [
        (gated(normed(yc_ref, nc_ref), gc_ref), 0),
        (gated(ya[:, :half], ga0_ref), CONV_WIDTH),
        (gated(ya[:, half:], ga1_ref), CONV_WIDTH + half),
        (gated(normed(yl_ref, nl_ref), gl_ref), CONV_WIDTH + ATTN_WIDTH),
    ]
    acc = x_ref[...]
    for y, r0 in parts:
        acc = acc + jnp.dot(y, w_ref[r0:r0 + y.shape[1], :], preferred_element_type=F32)
    o_ref[...] = acc


def _out_proj(x, yc, ya, yl, u, nc, na, nl, w, *, tm):
    t, d = x.shape
    gw = 512
    return pl.pallas_call(
        _out_proj_kernel,
        out_shape=jax.ShapeDtypeStruct((t, d), F32),
        grid=(t // tm,),
        in_specs=[
            pl.BlockSpec((tm, d), lambda i: (i, 0)),
            pl.BlockSpec((tm, CONV_WIDTH), lambda i: (i, 0)),
            pl.BlockSpec((tm, ATTN_WIDTH), lambda i: (i, 0)),
            pl.BlockSpec((tm, LRU_WIDTH), lambda i: (i, 0)),
            pl.BlockSpec((tm, gw), lambda i: (i, COL_CGATE // gw)),
            pl.BlockSpec((tm, gw), lambda i: (i, COL_AGATE // gw)),
            pl.BlockSpec((tm, gw), lambda i: (i, COL_AGATE // gw + 1)),
            pl.BlockSpec((tm, gw), lambda i: (i, COL_RGATE // gw)),
            pl.BlockSpec((1, CONV_WIDTH), lambda i: (0, 0)),
            pl.BlockSpec((1, ATTN_WIDTH), lambda i: (0, 0)),
            pl.BlockSpec((1, LRU_WIDTH), lambda i: (0, 0)),
            pl.BlockSpec((d, d), lambda i: (0, 0)),
        ],
        out_specs=pl.BlockSpec((tm, d), lambda i: (i, 0)),
        compiler_params=_params("parallel"),
        name="out_proj",
    )(x, yc, ya, yl, u, u, u, u, nc, na, nl, w)


def _xattn_kernel(x_ref, g_ref, wq_ref, k_ref, v_ref, wo_ref, fg_ref, o_ref, *, final_norm):
    x = x_ref[...]
    h = (x * _rms_scale(x) * g_ref[...]).astype(BF16)
    q = jnp.dot(h, wq_ref[...], preferred_element_type=F32).astype(BF16)
    dh = XATTN_WIDTH // XATTN_HEADS
    scale = dh ** -0.5
    acc = x
    for n in range(XATTN_HEADS):
        qh = q[:, n * dh:(n + 1) * dh]
        kh = k_ref[:, n * dh:(n + 1) * dh]
        vh = v_ref[:, n * dh:(n + 1) * dh]
        s = lax.dot_general(qh, kh, (((1,), (1,)), ((), ())), preferred_element_type=F32) * scale
        e = jnp.exp(s - jnp.max(s, axis=-1, keepdims=True))
        p = (e / jnp.sum(e, axis=-1, keepdims=True)).astype(BF16)
        oh = jnp.dot(p, vh, preferred_element_type=F32).astype(BF16)
        acc = acc + jnp.dot(oh, wo_ref[n * dh:(n + 1) * dh, :], preferred_element_type=F32)
    if final_norm:
        acc = acc * _rms_scale(acc) * fg_ref[...]
    o_ref[...] = acc


def _xattn(x3, g, wq, kv, wo, fg, *, tm, final_norm):
    b, s, d = x3.shape
    m = kv.shape[1]
    xw = XATTN_WIDTH
    return pl.pallas_call(
        functools.partial(_xattn_kernel, final_norm=final_norm),
        out_shape=jax.ShapeDtypeStruct((b, s, d), F32),
        grid=(b, s // tm),
        in_specs=[
            pl.BlockSpec((None, tm, d), lambda bi, ti: (bi, ti, 0)),
            pl.BlockSpec((1, d), lambda bi, ti: (0, 0)),
            pl.BlockSpec((d, xw), lambda bi, ti: (0, 0)),
            pl.BlockSpec((None, m, xw), lambda bi, ti: (bi, 0, 0)),
            pl.BlockSpec((None, m, xw), lambda bi, ti: (bi, 0, 1)),
            pl.BlockSpec((xw, d), lambda bi, ti: (0, 0)),
            pl.BlockSpec((1, d), lambda bi, ti: (0, 0)),
        ],
        out_specs=pl.BlockSpec((None, tm, d), lambda bi, ti: (bi, ti, 0)),
        compiler_params=_params("parallel", "parallel"),
        name="xattn_final" if final_norm else "xattn",
    )(x3, g, wq, kv, kv, wo, fg)


def kernel(x, mem, mix_norm_g, w_in, conv_dw_w, conv_dw_b, conv_ln_g, conv_ln_b, conv_pw_w,
           lru_conv_w, lru_conv_b, lru_wa, lru_ba, lru_wx, lru_bx, lru_lambda,
           out_norm_conv, out_norm_attn, out_norm_lru, w_out,
           xattn_norm_g, mem_norm_g, xattn_wq, xattn_wkv, xattn_wo, final_norm_g):
    b, s, d = x.shape
    m = mem.shape[1]
    depth = w_in.shape[0]
    t = b * s
    row = lambda a: a.reshape(1, -1).astype(F32)

    xt = x.reshape(t, d)
    memt = mem.reshape(b * m, d)
    for l in range(depth):
        u = _norm_matmul(xt, row(mix_norm_g[l]), w_in[l].astype(BF16), tm=1024, tn=512, name="in_proj")
        u3 = u.reshape(b, s, IN_WIDTH)
        y_conv = _conv_branch(u3, conv_dw_w[l], row(conv_dw_b[l]), row(conv_ln_g[l]), row(conv_ln_b[l]),
                              conv_pw_w[l].astype(BF16), ts=512)
        y_attn = _sb_attention(u3)
        y_lru = _lru_branch(u3, lru_conv_w[l], row(lru_conv_b[l]), lru_wa[l].astype(BF16), row(lru_ba[l]),
                            lru_wx[l].astype(BF16), row(lru_bx[l]), row(lru_lambda[l]), ts=512)
        xt = _out_proj(xt, y_conv.reshape(t, -1), y_attn.reshape(t, -1), y_lru.reshape(t, -1), u,
                       row(out_norm_conv[l]), row(out_norm_attn[l]), row(out_norm_lru[l]),
                       w_out[l].astype(BF16), tm=512)
        kv = _norm_matmul(memt, row(mem_norm_g[l]), xattn_wkv[l].astype(BF16), tm=256, tn=1024,
                          name="mem_kv").reshape(b, m, 2 * XATTN_WIDTH)
        xt = _xattn(xt.reshape(b, s, d), row(xattn_norm_g[l]), xattn_wq[l].astype(BF16), kv,
                    xattn_wo[l].astype(BF16), row(final_norm_g), tm=512,
                    final_norm=(l == depth - 1)).reshape(t, d)
    return xt.reshape(b, s, d)
```

```python
import functools

import jax
import jax.numpy as jnp
from jax import lax
from jax.experimental import pallas as pl
from jax.experimental.pallas import tpu as pltpu

F32 = jnp.float32
BF16 = jnp.bfloat16

D_MODEL = 2048
CONV_WIDTH = 512
CONV_KERNEL = 31
HEAD_DIM = 128
ATTN_WIDTH = 1024
ATTN_HEADS = 8
LRU_WIDTH = 512
LRU_HEADS = 4
LRU_CONV_KERNEL = 4
LRU_C = 8.0
XATTN_HEADS = 4
XATTN_WIDTH = 512
IN_WIDTH = 3 * CONV_WIDTH + 4 * ATTN_WIDTH + 2 * LRU_WIDTH
COL_CVAL, COL_CGLU, COL_CGATE = 0, 512, 1024
COL_Q, COL_K, COL_V, COL_AGATE = 1536, 2560, 3584, 4608
COL_RX, COL_RGATE = 5632, 6144

VMEM_LIMIT_BYTES = 56 * 1024 * 1024
SUBLANES = 8

RMS_EPS = 1e-6
LN_EPS = 1e-5


def _params(*sem):
    return pltpu.CompilerParams(dimension_semantics=sem, vmem_limit_bytes=VMEM_LIMIT_BYTES)


def _rms_scale(x):
    return lax.rsqrt(jnp.mean(x * x, axis=-1, keepdims=True) + RMS_EPS)


def _softplus(x):
    return jnp.maximum(x, 0.0) + jnp.log(1.0 + jnp.exp(-jnp.abs(x)))


def _silu(x):
    return x * jax.nn.sigmoid(x)


def _norm_matmul_kernel(x_ref, g_ref, w_ref, o_ref, h_ref):
    @pl.when(pl.program_id(1) == 0)
    def _():
        x = x_ref[...]
        h_ref[...] = (x * _rms_scale(x) * g_ref[...]).astype(BF16)

    o_ref[...] = jnp.dot(h_ref[...], w_ref[...], preferred_element_type=F32).astype(o_ref.dtype)


def _norm_matmul(x, g, w, *, tm, tn, name):
    t, d = x.shape
    n = w.shape[1]
    return pl.pallas_call(
        _norm_matmul_kernel,
        out_shape=jax.ShapeDtypeStruct((t, n), BF16),
        grid=(t // tm, n // tn),
        in_specs=[
            pl.BlockSpec((tm, d), lambda i, j: (i, 0)),
            pl.BlockSpec((1, d), lambda i, j: (0, 0)),
            pl.BlockSpec((d, tn), lambda i, j: (0, j)),
        ],
        out_specs=pl.BlockSpec((tm, tn), lambda i, j: (i, j)),
        scratch_shapes=[pltpu.VMEM((tm, d), BF16)],
        compiler_params=_params("parallel", "arbitrary"),
        name=name,
    )(x, g, w)


CONV_HALO = 32
CONV_ROW_CHUNK = 32


def _conv_branch_kernel(val_ref, glu_ref, dww_ref, dwb_ref, lng_ref, lnb_ref, pw_ref, o_ref,
                        buf_ref, dw_ref):
    ts = o_ref.shape[0]

    @pl.when(pl.program_id(1) == 0)
    def _():
        buf_ref[0:CONV_HALO, :] = jnp.zeros((CONV_HALO, CONV_WIDTH), F32)

    val = val_ref[...].astype(F32)
    glu = glu_ref[...].astype(F32)
    buf_ref[CONV_HALO:CONV_HALO + ts, :] = val * jax.nn.sigmoid(glu)

    base = CONV_HALO - (CONV_KERNEL - 1)
    for c in range(ts // CONV_ROW_CHUNK):
        r0 = c * CONV_ROW_CHUNK
        acc = jnp.broadcast_to(dwb_ref[...], (CONV_ROW_CHUNK, CONV_WIDTH))
        for k in range(CONV_KERNEL):
            acc = acc + dww_ref[k:k + 1, :] * buf_ref[r0 + base + k:r0 + base + k + CONV_ROW_CHUNK, :]
        dw_ref[r0:r0 + CONV_ROW_CHUNK, :] = acc

    buf_ref[0:CONV_HALO, :] = buf_ref[ts:ts + CONV_HALO, :]

    u = dw_ref[...]
    mu = jnp.mean(u, axis=-1, keepdims=True)
    uc = u - mu
    var = jnp.mean(uc * uc, axis=-1, keepdims=True)
    y = uc * lax.rsqrt(var + LN_EPS) * lng_ref[...] + lnb_ref[...]
    y = _silu(y).astype(BF16)
    o_ref[...] = jnp.dot(y, pw_ref[...], preferred_element_type=F32).astype(o_ref.dtype)


def _conv_branch(u3, dww, dwb, lng, lnb, pw, *, ts):
    b, s, _ = u3.shape
    cw = CONV_WIDTH
    vec = pl.BlockSpec((1, cw), lambda bi, ti: (0, 0))
    return pl.pallas_call(
        _conv_branch_kernel,
        out_shape=jax.ShapeDtypeStruct((b, s, cw), BF16),
        grid=(b, s // ts),
        in_specs=[
            pl.BlockSpec((None, ts, cw), lambda bi, ti: (bi, ti, COL_CVAL // cw)),
            pl.BlockSpec((None, ts, cw), lambda bi, ti: (bi, ti, COL_CGLU // cw)),
            pl.BlockSpec((CONV_KERNEL, cw), lambda bi, ti: (0, 0)),
            vec, vec, vec,
            pl.BlockSpec((cw, cw), lambda bi, ti: (0, 0)),
        ],
        out_specs=pl.BlockSpec((None, ts, cw), lambda bi, ti: (bi, ti, 0)),
        scratch_shapes=[pltpu.VMEM((ts + CONV_HALO, cw), F32), pltpu.VMEM((ts, cw), F32)],
        compiler_params=_params("parallel", "arbitrary"),
        name="conv_branch",
    )(u3, u3, dww, dwb, lng, lnb, pw)


SB_T = 128
SB_GROUP = 4
SB_EAGER = 3
SB_PAD = SB_GROUP - 1
SB_LOG_WEIGHT_FLOOR = -104.0


def _sb_split(log_1mb):
    hi = log_1mb.astype(BF16)
    lo = (log_1mb - hi.astype(F32)).astype(BF16)
    return jnp.concatenate([hi, lo], axis=1)


def _sb_attn_kernel(q_ref, k_ref, v_ref, uo_ref, o_ref, kp_ref, vp_ref, acc_ref, carry_ref):
    s_len = q_ref.shape[0]
    scale = HEAD_DIM ** -0.5
    t_sz, n_grp = SB_T, SB_GROUP
    pad_rows = SB_PAD * t_sz
    kp_ref[0:pad_rows, :] = jnp.zeros((pad_rows, HEAD_DIM), BF16)
    vp_ref[0:pad_rows, :] = jnp.zeros((pad_rows, HEAD_DIM), BF16)
    kp_ref[pad_rows:pad_rows + s_len, :] = k_ref[...]
    vp_ref[pad_rows:pad_rows + s_len, :] = v_ref[...]
    causal = (lax.broadcasted_iota(jnp.int32, (t_sz, t_sz), 1)
              < lax.broadcasted_iota(jnp.int32, (t_sz, t_sz), 0))

    def q_tile(i):
        return q_ref[pl.ds(pl.multiple_of(i * t_sz, t_sz), t_sz), :]

    def scores(q, k):
        return lax.dot_general(q, k, (((1,), (1,)), ((), ())), preferred_element_type=F32) * scale

    def group(g, _):
        i0 = g * n_grp

        def eager():
            width = SB_EAGER * t_sz
            zs, sps, starts, lhs = [], [], [], []
            for r in range(n_grp):
                start = pl.multiple_of((i0 + r + SB_PAD - (SB_EAGER - 1)) * t_sz, t_sz)
                z = scores(q_tile(i0 + r), kp_ref[pl.ds(start, width), :])
                sp = _softplus(z)
                pieces = []
                for c in range(SB_EAGER):
                    log_1mb = -sp[:, c * t_sz:(c + 1) * t_sz]
                    if c == SB_EAGER - 1:
                        log_1mb = jnp.where(causal, log_1mb, 0.0)
                    pieces.append(_sb_split(log_1mb))
                lhs.append(jnp.concatenate(pieces, axis=0))
                zs.append(z)
                sps.append(sp)
                starts.append(start)
            sums = [jnp.dot(x, uo_ref[...], preferred_element_type=F32) for x in lhs]
            m = None
            for r in range(n_grp):
                carry = None
                ws = [None] * SB_EAGER
                for c in reversed(range(SB_EAGER)):
                    cols = slice(c * t_sz, (c + 1) * t_sz)
                    sc = sums[r][cols]
                    after = sc[:, :t_sz] if carry is None else carry + sc[:, :t_sz]
                    w = jnp.exp((zs[r][:, cols] - sps[r][:, cols]) + after)
                    if c == SB_EAGER - 1:
                        w = jnp.where(causal, w, 0.0)
                    ws[c] = w.astype(BF16)
                    carry = sc[:, t_sz:] if carry is None else carry + sc[:, t_sz:]
                v = vp_ref[pl.ds(starts[r], width), :]
                acc_ref[r] = jnp.dot(jnp.concatenate(ws, axis=1), v, preferred_element_type=F32)
                carry_ref[r] = carry
                m = carry if m is None else jnp.maximum(m, carry)
            return jnp.max(m)

        def step(d):
            zs, sps, starts, lhs = [], [], [], []
            for r in range(n_grp):
                start = pl.multiple_of((i0 + r + SB_PAD - d) * t_sz, t_sz)
                z = scores(q_tile(i0 + r), kp_ref[pl.ds(start, t_sz), :])
                sp = _softplus(z)
                lhs.append(_sb_split(-sp))
                zs.append(z)
                sps.append(sp)
                starts.append(start)
            sums = jnp.dot(jnp.concatenate(lhs, axis=0), uo_ref[...], preferred_element_type=F32)
            m = None
            for r in range(n_grp):
                sr = sums[r * t_sz:(r + 1) * t_sz]
                w = jnp.exp((zs[r] - sps[r]) + (carry_ref[r] + sr[:, :t_sz]))
                v = vp_ref[pl.ds(starts[r], t_sz), :]
                acc_ref[r] += jnp.dot(w.astype(BF16), v, preferred_element_type=F32)
                carry = carry_ref[r] + sr[:, t_sz:]
                carry_ref[r] = carry
                m = carry if m is None else jnp.maximum(m, carry)
            return jnp.max(m)

        def more(c):
            d, m = c
            return jnp.logical_and(d <= i0 + n_grp - 1, m > SB_LOG_WEIGHT_FLOOR)

        lax.while_loop(more, lambda c: (c[0] + 1, step(c[0])), (jnp.int32(SB_EAGER), eager()))
        for r in range(n_grp):
            o_ref[pl.ds(pl.multiple_of((i0 + r) * t_sz, t_sz), t_sz), :] = acc_ref[r].astype(o_ref.dtype)
        return 0

    lax.fori_loop(0, s_len // (t_sz * n_grp), group, 0)


def _sb_attention(u3):
    b, s, _ = u3.shape
    dh = HEAD_DIM
    assert SB_EAGER - 1 <= SB_PAD and s % (SB_T * SB_GROUP) == 0
    row = lax.broadcasted_iota(jnp.int32, (2 * SB_T, 2 * SB_T), 0) % SB_T
    col = lax.broadcasted_iota(jnp.int32, (2 * SB_T, 2 * SB_T), 1)
    uo = jnp.where((col >= SB_T) | (row > col), 1.0, 0.0).astype(BF16)

    def head_spec(col0):
        return pl.BlockSpec((None, s, dh), lambda bi, hi: (bi, 0, col0 // dh + hi))

    padded = pltpu.VMEM((s + SB_PAD * SB_T, dh), BF16)
    return pl.pallas_call(
        _sb_attn_kernel,
        out_shape=jax.ShapeDtypeStruct((b, s, ATTN_WIDTH), BF16),
        grid=(b, ATTN_HEADS),
        in_specs=[head_spec(COL_Q), head_spec(COL_K), head_spec(COL_V),
                  pl.BlockSpec((2 * SB_T, 2 * SB_T), lambda bi, hi: (0, 0))],
        out_specs=pl.BlockSpec((None, s, dh), lambda bi, hi: (bi, 0, hi)),
        scratch_shapes=[padded, padded,
                        pltpu.VMEM((SB_GROUP, SB_T, dh), F32), pltpu.VMEM((SB_GROUP, SB_T, SB_T), F32)],
        compiler_params=_params("parallel", "parallel"),
        name="sb_attention",
    )(u3, u3, u3, uo)


LRU_HALO = SUBLANES


def _lru_kernel(x_ref, cw_ref, cb_ref, wa_ref, ba_ref, wx_ref, bx_ref, lam_ref, o_ref,
                buf_ref, a_ref, b_ref, h_ref):
    ts = o_ref.shape[0]
    w = LRU_WIDTH

    @pl.when(pl.program_id(1) == 0)
    def _():
        buf_ref[0:LRU_HALO, :] = jnp.zeros((LRU_HALO, w), F32)
        h_ref[...] = jnp.zeros_like(h_ref)

    buf_ref[LRU_HALO:LRU_HALO + ts, :] = x_ref[...].astype(F32)
    base = LRU_HALO - (LRU_CONV_KERNEL - 1)
    xc = jnp.broadcast_to(cb_ref[...], (ts, w))
    for k in range(LRU_CONV_KERNEL):
        xc = xc + cw_ref[k:k + 1, :] * buf_ref[base + k:base + k + ts, :]
    buf_ref[0:LRU_HALO, :] = buf_ref[ts:ts + LRU_HALO, :]

    xcb = xc.astype(BF16)
    hd = w // LRU_HEADS
    rs, gs = [], []
    for n in range(LRU_HEADS):
        xh = xcb[:, n * hd:(n + 1) * hd]
        rs.append(jnp.dot(xh, wa_ref[n], preferred_element_type=F32))
        gs.append(jnp.dot(xh, wx_ref[n], preferred_element_type=F32))
    r = jax.nn.sigmoid(jnp.concatenate(rs, axis=-1) + ba_ref[...])
    gate = jax.nn.sigmoid(jnp.concatenate(gs, axis=-1) + bx_ref[...])
    log_a = (-LRU_C) * r * _softplus(-lam_ref[...])
    a = jnp.exp(log_a)
    x2 = 2.0 * log_a
    e2 = jnp.exp(x2)
    near = (1.0 - e2) * x2 / jnp.log(e2)
    one_m = jnp.where(x2 > -0.5, jnp.where(e2 == 1.0, -x2, near), 1.0 - e2)
    a_ref[...] = a
    b_ref[...] = jnp.sqrt(one_m) * (gate * xc)

    row = lax.broadcasted_iota(jnp.int32, (SUBLANES, w), 0)

    def group(g, h_prev):
        r0 = pl.multiple_of(g * SUBLANES, SUBLANES)
        ag = a_ref[pl.ds(r0, SUBLANES), :]
        bg = b_ref[pl.ds(r0, SUBLANES), :]
        d = 1
        while d < SUBLANES:
            a_sh = pltpu.roll(ag, d, 0)
            b_sh = pltpu.roll(bg, d, 0)
            m = row >= d
            bg = jnp.where(m, ag * b_sh + bg, bg)
            ag = jnp.where(m, ag * a_sh, ag)
            d *= 2
        h = ag * h_prev + bg
        b_ref[pl.ds(r0, SUBLANES), :] = h
        return jnp.broadcast_to(h[SUBLANES - 1:SUBLANES, :], (SUBLANES, w))

    h_last = lax.fori_loop(0, ts // SUBLANES, group, h_ref[...], unroll=4)
    h_ref[...] = h_last
    o_ref[...] = b_ref[...].astype(o_ref.dtype)


def _lru_branch(u3, cw, cb, wa, ba, wx, bx, lam, *, ts):
    b, s, _ = u3.shape
    w = LRU_WIDTH
    hd = w // LRU_HEADS
    vec = pl.BlockSpec((1, w), lambda bi, ti: (0, 0))
    mat = pl.BlockSpec((LRU_HEADS, hd, hd), lambda bi, ti: (0, 0, 0))
    return pl.pallas_call(
        _lru_kernel,
        out_shape=jax.ShapeDtypeStruct((b, s, w), BF16),
        grid=(b, s // ts),
        in_specs=[
            pl.BlockSpec((None, ts, w), lambda bi, ti: (bi, ti, COL_RX // w)),
            pl.BlockSpec((LRU_CONV_KERNEL, w), lambda bi, ti: (0, 0)),
            vec, mat, vec, mat, vec, vec,
        ],
        out_specs=pl.BlockSpec((None, ts, w), lambda bi, ti: (bi, ti, 0)),
        scratch_shapes=[pltpu.VMEM((ts + LRU_HALO, w), F32), pltpu.VMEM((ts, w), F32),
                        pltpu.VMEM((ts, w), F32), pltpu.VMEM((SUBLANES, w), F32)],
        compiler_params=_params("parallel", "arbitrary"),
        name="lru_branch",
    )(u3, cw, cb, wa, ba, wx, bx, lam)


def _out_proj_kernel(x_ref, yc_ref, ya_ref, yl_ref, gc_ref, ga0_ref, ga1_ref, gl_ref,
                     nc_ref, na_ref, nl_ref, w_ref, o_ref):
    def normed(y_ref, n_ref):
        y = y_ref[...].astype(F32)
        return y * _rms_scale(y) * n_ref[...]

    def gated(yn, gate_ref):
        return (yn * _silu(gate_ref[...].astype(F32))).astype(BF16)

    half = ATTN_WIDTH // 2
    ya = normed(ya_ref, na_ref)
    parts = [
        (gated(normed(yc_ref, nc_ref), gc_ref), 0),
        (gated(ya[:, :half], ga0_ref), CONV_WIDTH),
        (gated(ya[:, half:], ga1_ref), CONV_WIDTH + half),
        (gated(normed(yl_ref, nl_ref), gl_ref), CONV_WIDTH + ATTN_WIDTH),
    ]
    acc = x_ref[...]
    for y, r0 in parts:
        acc = acc + jnp.dot(y, w_ref[r0:r0 + y.shape[1], :], preferred_element_type=F32)
    o_ref[...] = acc


def _out_proj(x, yc, ya, yl, u, nc, na, nl, w, *, tm):
    t, d = x.shape
    gw = 512
    return pl.pallas_call(
        _out_proj_kernel,
        out_shape=jax.ShapeDtypeStruct((t, d), F32),
        grid=(t // tm,),
        in_specs=[
            pl.BlockSpec((tm, d), lambda i: (i, 0)),
            pl.BlockSpec((tm, CONV_WIDTH), lambda i: (i, 0)),
            pl.BlockSpec((tm, ATTN_WIDTH), lambda i: (i, 0)),
            pl.BlockSpec((tm, LRU_WIDTH), lambda i: (i, 0)),
            pl.BlockSpec((tm, gw), lambda i: (i, COL_CGATE // gw)),
            pl.BlockSpec((tm, gw), lambda i: (i, COL_AGATE // gw)),
            pl.BlockSpec((tm, gw), lambda i: (i, COL_AGATE // gw + 1)),
            pl.BlockSpec((tm, gw), lambda i: (i, COL_RGATE // gw)),
            pl.BlockSpec((1, CONV_WIDTH), lambda i: (0, 0)),
            pl.BlockSpec((1, ATTN_WIDTH), lambda i: (0, 0)),
            pl.BlockSpec((1, LRU_WIDTH), lambda i: (0, 0)),
            pl.BlockSpec((d, d), lambda i: (0, 0)),
        ],
        out_specs=pl.BlockSpec((tm, d), lambda i: (i, 0)),
        compiler_params=_params("parallel"),
        name="out_proj",
    )(x, yc, ya, yl, u, u, u, u, nc, na, nl, w)


def _xattn_kernel(x_ref, g_ref, wq_ref, k_ref, v_ref, wo_ref, fg_ref, o_ref, *, final_norm):
    x = x_ref[...]
    h = (x * _rms_scale(x) * g_ref[...]).astype(BF16)
    q = jnp.dot(h, wq_ref[...], preferred_element_type=F32).astype(BF16)
    dh = XATTN_WIDTH // XATTN_HEADS
    scale = dh ** -0.5
    acc = x
    for n in range(XATTN_HEADS):
        qh = q[:, n * dh:(n + 1) * dh]
        kh = k_ref[:, n * dh:(n + 1) * dh]
        vh = v_ref[:, n * dh:(n + 1) * dh]
        s = lax.dot_general(qh, kh, (((1,), (1,)), ((), ())), preferred_element_type=F32) * scale
        e = jnp.exp(s - jnp.max(s, axis=-1, keepdims=True))
        p = (e / jnp.sum(e, axis=-1, keepdims=True)).astype(BF16)
        oh = jnp.dot(p, vh, preferred_element_type=F32).astype(BF16)
        acc = acc + jnp.dot(oh, wo_ref[n * dh:(n + 1) * dh, :], preferred_element_type=F32)
    if final_norm:
        acc = acc * _rms_scale(acc) * fg_ref[...]
    o_ref[...] = acc


def _xattn(x3, g, wq, kv, wo, fg, *, tm, final_norm):
    b, s, d = x3.shape
    m = kv.shape[1]
    xw = XATTN_WIDTH
    return pl.pallas_call(
        functools.partial(_xattn_kernel, final_norm=final_norm),
        out_shape=jax.ShapeDtypeStruct((b, s, d), F32),
        grid=(b, s // tm),
        in_specs=[
            pl.BlockSpec((None, tm, d), lambda bi, ti: (bi, ti, 0)),
            pl.BlockSpec((1, d), lambda bi, ti: (0, 0)),
            pl.BlockSpec((d, xw), lambda bi, ti: (0, 0)),
            pl.BlockSpec((None, m, xw), lambda bi, ti: (bi, 0, 0)),
            pl.BlockSpec((None, m, xw), lambda bi, ti: (bi, 0, 1)),
            pl.BlockSpec((xw, d), lambda bi, ti: (0, 0)),
            pl.BlockSpec((1, d), lambda bi, ti: (0, 0)),
        ],
        out_specs=pl.BlockSpec((None, tm, d), lambda bi, ti: (bi, ti, 0)),
        compiler_params=_params("parallel", "parallel"),
        name="xattn_final" if final_norm else "xattn",
    )(x3, g, wq, kv, kv, wo, fg)


def kernel(x, mem, mix_norm_g, w_in, conv_dw_w, conv_dw_b, conv_ln_g, conv_ln_b, conv_pw_w,
           lru_conv_w, lru_conv_b, lru_wa, lru_ba, lru_wx, lru_bx, lru_lambda,
           out_norm_conv, out_norm_attn, out_norm_lru, w_out,
           xattn_norm_g, mem_norm_g, xattn_wq, xattn_wkv, xattn_wo, final_norm_g):
    b, s, d = x.shape
    m = mem.shape[1]
    depth = w_in.shape[0]
    t = b * s
    row = lambda a: a.reshape(1, -1).astype(F32)

    xt = x.reshape(t, d)
    memt = mem.reshape(b * m, d)
    for l in range(depth):
        u = _norm_matmul(xt, row(mix_norm_g[l]), w_in[l].astype(BF16), tm=1024, tn=512, name="in_proj")
        u3 = u.reshape(b, s, IN_WIDTH)
        y_conv = _conv_branch(u3, conv_dw_w[l], row(conv_dw_b[l]), row(conv_ln_g[l]), row(conv_ln_b[l]),
                              conv_pw_w[l].astype(BF16), ts=512)
        y_attn = _sb_attention(u3)
        y_lru = _lru_branch(u3, lru_conv_w[l], row(lru_conv_b[l]), lru_wa[l].astype(BF16), row(lru_ba[l]),
                            lru_wx[l].astype(BF16), row(lru_bx[l]), row(lru_lambda[l]), ts=512)
        xt = _out_proj(xt, y_conv.reshape(t, -1), y_attn.reshape(t, -1), y_lru.reshape(t, -1), u,
                       row(out_norm_conv[l]), row(out_norm_attn[l]), row(out_norm_lru[l]),
                       w_out[l].astype(BF16), tm=512)
        kv = _norm_matmul(memt, row(mem_norm_g[l]), xattn_wkv[l].astype(BF16), tm=256, tn=1024,
                          name="mem_kv").reshape(b, m, 2 * XATTN_WIDTH)
        xt = _xattn(xt.reshape(b, s, d), row(xattn_norm_g[l]), xattn_wq[l].astype(BF16), kv,
                    xattn_wo[l].astype(BF16), row(final_norm_g), tm=512,
                    final_norm=(l == depth - 1)).reshape(t, d)
    return xt.reshape(b, s, d)
```

```python
import functools

import jax
import jax.numpy as jnp
from jax import lax
from jax.experimental import pallas as pl
from jax.experimental.pallas import tpu as pltpu

F32 = jnp.float32
BF16 = jnp.bfloat16

D_MODEL = 2048
CONV_WIDTH = 512
CONV_KERNEL = 31
HEAD_DIM = 128
ATTN_WIDTH = 1024
ATTN_HEADS = 8
LRU_WIDTH = 512
LRU_HEADS = 4
LRU_CONV_KERNEL = 4
LRU_C = 8.0
XATTN_HEADS = 4
XATTN_WIDTH = 512
IN_WIDTH = 3 * CONV_WIDTH + 4 * ATTN_WIDTH + 2 * LRU_WIDTH
COL_CVAL, COL_CGLU, COL_CGATE = 0, 512, 1024
COL_Q, COL_K, COL_V, COL_AGATE = 1536, 2560, 3584, 4608
COL_RX, COL_RGATE = 5632, 6144

VMEM_LIMIT_BYTES = 56 * 1024 * 1024
SUBLANES = 8

RMS_EPS = 1e-6
LN_EPS = 1e-5


def _params(*sem):
    return pltpu.CompilerParams(dimension_semantics=sem, vmem_limit_bytes=VMEM_LIMIT_BYTES)


def _rms_scale(x):
    return lax.rsqrt(jnp.mean(x * x, axis=-1, keepdims=True) + RMS_EPS)


def _softplus(x):
    return jnp.maximum(x, 0.0) + jnp.log(1.0 + jnp.exp(-jnp.abs(x)))


def _silu(x):
    return x * jax.nn.sigmoid(x)


def _norm_matmul_kernel(x_ref, g_ref, w_ref, o_ref, h_ref):
    @pl.when(pl.program_id(1) == 0)
    def _():
        x = x_ref[...]
        h_ref[...] = (x * _rms_scale(x) * g_ref[...]).astype(BF16)

    o_ref[...] = jnp.dot(h_ref[...], w_ref[...], preferred_element_type=F32).astype(o_ref.dtype)


def _norm_matmul(x, g, w, *, tm, tn, name):
    t, d = x.shape
    n = w.shape[1]
    return pl.pallas_call(
        _norm_matmul_kernel,
        out_shape=jax.ShapeDtypeStruct((t, n), BF16),
        grid=(t // tm, n // tn),
        in_specs=[
            pl.BlockSpec((tm, d), lambda i, j: (i, 0)),
            pl.BlockSpec((1, d), lambda i, j: (0, 0)),
            pl.BlockSpec((d, tn), lambda i, j: (0, j)),
        ],
        out_specs=pl.BlockSpec((tm, tn), lambda i, j: (i, j)),
        scratch_shapes=[pltpu.VMEM((tm, d), BF16)],
        compiler_params=_params("parallel", "arbitrary"),
        name=name,
    )(x, g, w)


CONV_HALO = 32
CONV_ROW_CHUNK = 64
LANES = 128


def _conv_branch_kernel(val_ref, glu_ref, dww_ref, dwb_ref, lng_ref, lnb_ref, pw_ref, o_ref,
                        buf_ref, dw_ref):
    ts = o_ref.shape[0]

    @pl.when(pl.program_id(1) == 0)
    def _():
        buf_ref[0:CONV_HALO, :] = jnp.zeros((CONV_HALO, CONV_WIDTH), F32)
        buf_ref[CONV_HALO + ts:CONV_HALO + ts + SUBLANES, :] = jnp.zeros((SUBLANES, CONV_WIDTH), F32)

    val = val_ref[...].astype(F32)
    glu = glu_ref[...].astype(F32)
    buf_ref[CONV_HALO:CONV_HALO + ts, :] = val * jax.nn.sigmoid(glu)

    base = CONV_HALO - (CONV_KERNEL - 1)
    rows = CONV_ROW_CHUNK + SUBLANES

    def chunk(c, _):
        r0 = pl.multiple_of(c * CONV_ROW_CHUNK, CONV_ROW_CHUNK)
        for lb in range(CONV_WIDTH // LANES):
            lanes = slice(lb * LANES, (lb + 1) * LANES)
            a = None
            for rho in reversed(range(SUBLANES)):
                q = None
                for m in range((base + CONV_KERNEL - 1) // SUBLANES + 1):
                    k = SUBLANES * m + rho - base
                    if 0 <= k < CONV_KERNEL:
                        term = dww_ref[k:k + 1, lanes] * buf_ref[pl.ds(r0 + SUBLANES * m, rows), lanes]
                        q = term if q is None else q + term
                a = q if a is None else q + pltpu.roll(a, rows - 1, 0)
            dw_ref[pl.ds(r0, CONV_ROW_CHUNK), lanes] = a[:CONV_ROW_CHUNK] + dwb_ref[:, lanes]
        return 0

    lax.fori_loop(0, ts // CONV_ROW_CHUNK, chunk, 0)

    buf_ref[0:CONV_HALO, :] = buf_ref[ts:ts + CONV_HALO, :]

    u = dw_ref[...]
    mu = jnp.mean(u, axis=-1, keepdims=True)
    uc = u - mu
    var = jnp.mean(uc * uc, axis=-1, keepdims=True)
    y = uc * lax.rsqrt(var + LN_EPS) * lng_ref[...] + lnb_ref[...]
    y = _silu(y).astype(BF16)
    o_ref[...] = jnp.dot(y, pw_ref[...], preferred_element_type=F32).astype(o_ref.dtype)


def _conv_branch(u3, dww, dwb, lng, lnb, pw, *, ts):
    b, s, _ = u3.shape
    cw = CONV_WIDTH
    vec = pl.BlockSpec((1, cw), lambda bi, ti: (0, 0))
    return pl.pallas_call(
        _conv_branch_kernel,
        out_shape=jax.ShapeDtypeStruct((b, s, cw), BF16),
        grid=(b, s // ts),
        in_specs=[
            pl.BlockSpec((None, ts, cw), lambda bi, ti: (bi, ti, COL_CVAL // cw)),
            pl.BlockSpec((None, ts, cw), lambda bi, ti: (bi, ti, COL_CGLU // cw)),
            pl.BlockSpec((CONV_KERNEL, cw), lambda bi, ti: (0, 0)),
            vec, vec, vec,
            pl.BlockSpec((cw, cw), lambda bi, ti: (0, 0)),
        ],
        out_specs=pl.BlockSpec((None, ts, cw), lambda bi, ti: (bi, ti, 0)),
        scratch_shapes=[pltpu.VMEM((ts + CONV_HALO + SUBLANES, cw), F32), pltpu.VMEM((ts, cw), F32)],
        compiler_params=_params("parallel", "arbitrary"),
        name="conv_branch",
    )(u3, u3, dww, dwb, lng, lnb, pw)


SB_T = 128
SB_GROUP = 4
SB_EAGER = 3
SB_PAD = SB_GROUP - 1
SB_LOG_WEIGHT_FLOOR = -104.0


def _sb_split(log_1mb):
    hi = log_1mb.astype(BF16)
    lo = (log_1mb - hi.astype(F32)).astype(BF16)
    return jnp.concatenate([hi, lo], axis=1)


def _sb_attn_kernel(q_ref, k_ref, v_ref, uo_ref, o_ref, kp_ref, vp_ref, acc_ref, carry_ref):
    s_len = q_ref.shape[0]
    scale = HEAD_DIM ** -0.5
    t_sz, n_grp = SB_T, SB_GROUP
    pad_rows = SB_PAD * t_sz
    kp_ref[0:pad_rows, :] = jnp.zeros((pad_rows, HEAD_DIM), BF16)
    vp_ref[0:pad_rows, :] = jnp.zeros((pad_rows, HEAD_DIM), BF16)
    kp_ref[pad_rows:pad_rows + s_len, :] = k_ref[...]
    vp_ref[pad_rows:pad_rows + s_len, :] = v_ref[...]
    causal = (lax.broadcasted_iota(jnp.int32, (t_sz, t_sz), 1)
              < lax.broadcasted_iota(jnp.int32, (t_sz, t_sz), 0))

    def q_tile(i):
        return q_ref[pl.ds(pl.multiple_of(i * t_sz, t_sz), t_sz), :]

    def scores(q, k):
        return lax.dot_general(q, k, (((1,), (1,)), ((), ())), preferred_element_type=F32) * scale

    def group(g, _):
        i0 = g * n_grp

        def eager():
            width = SB_EAGER * t_sz
            zs, sps, starts, lhs = [], [], [], []
            for r in range(n_grp):
                start = pl.multiple_of((i0 + r + SB_PAD - (SB_EAGER - 1)) * t_sz, t_sz)
                z = scores(q_tile(i0 + r), kp_ref[pl.ds(start, width), :])
                sp = _softplus(z)
                pieces = []
                for c in range(SB_EAGER):
                    log_1mb = -sp[:, c * t_sz:(c + 1) * t_sz]
                    if c == SB_EAGER - 1:
                        log_1mb = jnp.where(causal, log_1mb, 0.0)
                    pieces.append(_sb_split(log_1mb))
                lhs.append(jnp.concatenate(pieces, axis=0))
                zs.append(z)
                sps.append(sp)
                starts.append(start)
            sums = [jnp.dot(x, uo_ref[...], preferred_element_type=F32) for x in lhs]
            m = None
            for r in range(n_grp):
                carry = None
                ws = [None] * SB_EAGER
                for c in reversed(range(SB_EAGER)):
                    cols = slice(c * t_sz, (c + 1) * t_sz)
                    sc = sums[r][cols]
                    after = sc[:, :t_sz] if carry is None else carry + sc[:, :t_sz]
                    w = jnp.exp((zs[r][:, cols] - sps[r][:, cols]) + after)
                    if c == SB_EAGER - 1:
                        w = jnp.where(causal, w, 0.0)
                    ws[c] = w.astype(BF16)
                    carry = sc[:, t_sz:] if carry is None else carry + sc[:, t_sz:]
                v = vp_ref[pl.ds(starts[r], width), :]
                acc_ref[r] = jnp.dot(jnp.concatenate(ws, axis=1), v, preferred_element_type=F32)
                carry_ref[r] = carry
                m = carry if m is None else jnp.maximum(m, carry)
            return jnp.max(m)

        def step(d):
            zs, sps, starts, lhs = [], [], [], []
            for r in range(n_grp):
                start = pl.multiple_of((i0 + r + SB_PAD - d) * t_sz, t_sz)
                z = scores(q_tile(i0 + r), kp_ref[pl.ds(start, t_sz), :])
                sp = _softplus(z)
                lhs.append(_sb_split(-sp))
                zs.append(z)
                sps.append(sp)
                starts.append(start)
            sums = jnp.dot(jnp.concatenate(lhs, axis=0), uo_ref[...], preferred_element_type=F32)
            m = None
            for r in range(n_grp):
                sr = sums[r * t_sz:(r + 1) * t_sz]
                w = jnp.exp((zs[r] - sps[r]) + (carry_ref[r] + sr[:, :t_sz]))
                v = vp_ref[pl.ds(starts[r], t_sz), :]
                acc_ref[r] += jnp.dot(w.astype(BF16), v, preferred_element_type=F32)
                carry = carry_ref[r] + sr[:, t_sz:]
                carry_ref[r] = carry
                m = carry if m is None else jnp.maximum(m, carry)
            return jnp.max(m)

        def more(c):
            d, m = c
            return jnp.logical_and(d <= i0 + n_grp - 1, m > SB_LOG_WEIGHT_FLOOR)

        lax.while_loop(more, lambda c: (c[0] + 1, step(c[0])), (jnp.int32(SB_EAGER), eager()))
        for r in range(n_grp):
            o_ref[pl.ds(pl.multiple_of((i0 + r) * t_sz, t_sz), t_sz), :] = acc_ref[r].astype(o_ref.dtype)
        return 0

    lax.fori_loop(0, s_len // (t_sz * n_grp), group, 0)


def _sb_attention(u3):
    b, s, _ = u3.shape
    dh = HEAD_DIM
    assert SB_EAGER - 1 <= SB_PAD and s % (SB_T * SB_GROUP) == 0
    row = lax.broadcasted_iota(jnp.int32, (2 * SB_T, 2 * SB_T), 0) % SB_T
    col = lax.broadcasted_iota(jnp.int32, (2 * SB_T, 2 * SB_T), 1)
    uo = jnp.where((col >= SB_T) | (row > col), 1.0, 0.0).astype(BF16)

    def head_spec(col0):
        return pl.BlockSpec((None, s, dh), lambda bi, hi: (bi, 0, col0 // dh + hi))

    padded = pltpu.VMEM((s + SB_PAD * SB_T, dh), BF16)
    return pl.pallas_call(
        _sb_attn_kernel,
        out_shape=jax.ShapeDtypeStruct((b, s, ATTN_WIDTH), BF16),
        grid=(b, ATTN_HEADS),
        in_specs=[head_spec(COL_Q), head_spec(COL_K), head_spec(COL_V),
                  pl.BlockSpec((2 * SB_T, 2 * SB_T), lambda bi, hi: (0, 0))],
        out_specs=pl.BlockSpec((None, s, dh), lambda bi, hi: (bi, 0, hi)),
        scratch_shapes=[padded, padded,
                        pltpu.VMEM((SB_GROUP, SB_T, dh), F32), pltpu.VMEM((SB_GROUP, SB_T, SB_T), F32)],
        compiler_params=_params("parallel", "parallel"),
        name="sb_attention",
    )(u3, u3, u3, uo)


LRU_HALO = SUBLANES


def _lru_kernel(x_ref, cw_ref, cb_ref, wa_ref, ba_ref, wx_ref, bx_ref, lam_ref, o_ref,
                buf_ref, a_ref, b_ref, h_ref):
    ts = o_ref.shape[0]
    w = LRU_WIDTH

    @pl.when(pl.program_id(1) == 0)
    def _():
        buf_ref[0:LRU_HALO, :] = jnp.zeros((LRU_HALO, w), F32)
        h_ref[...] = jnp.zeros_like(h_ref)

    buf_ref[LRU_HALO:LRU_HALO + ts, :] = x_ref[...].astype(F32)
    base = LRU_HALO - (LRU_CONV_KERNEL - 1)
    xc = jnp.broadcast_to(cb_ref[...], (ts, w))
    for k in range(LRU_CONV_KERNEL):
        xc = xc + cw_ref[k:k + 1, :] * buf_ref[base + k:base + k + ts, :]
    buf_ref[0:LRU_HALO, :] = buf_ref[ts:ts + LRU_HALO, :]

    xcb = xc.astype(BF16)
    hd = w // LRU_HEADS
    rs, gs = [], []
    for n in range(LRU_HEADS):
        xh = xcb[:, n * hd:(n + 1) * hd]
        rs.append(jnp.dot(xh, wa_ref[n], preferred_element_type=F32))
        gs.append(jnp.dot(xh, wx_ref[n], preferred_element_type=F32))
    r = jax.nn.sigmoid(jnp.concatenate(rs, axis=-1) + ba_ref[...])
    gate = jax.nn.sigmoid(jnp.concatenate(gs, axis=-1) + bx_ref[...])
    log_a = (-LRU_C) * r * _softplus(-lam_ref[...])
    a = jnp.exp(log_a)
    b = jnp.sqrt(jnp.tanh(-log_a) * (1.0 + a * a)) * (gate * xc)
    a_ref[...] = a
    b_ref[...] = b

    row = lax.broadcasted_iota(jnp.int32, (SUBLANES, w), 0)

    def group(g, h_prev):
        r0 = pl.multiple_of(g * SUBLANES, SUBLANES)
        ag = a_ref[pl.ds(r0, SUBLANES), :]
        bg = b_ref[pl.ds(r0, SUBLANES), :]
        d = 1
        while d < SUBLANES:
            a_sh = pltpu.roll(ag, d, 0)
            b_sh = pltpu.roll(bg, d, 0)
            m = row >= d
            bg = jnp.where(m, ag * b_sh + bg, bg)
            ag = jnp.where(m, ag * a_sh, ag)
            d *= 2
        h = ag * h_prev + bg
        b_ref[pl.ds(r0, SUBLANES), :] = h
        return jnp.broadcast_to(h[SUBLANES - 1:SUBLANES, :], (SUBLANES, w))

    h_ref[...] = lax.fori_loop(0, ts // SUBLANES, group, h_ref[...], unroll=4)
    o_ref[...] = b_ref[...].astype(o_ref.dtype)


def _lru_branch(u3, cw, cb, wa, ba, wx, bx, lam, *, ts):
    b, s, _ = u3.shape
    w = LRU_WIDTH
    hd = w // LRU_HEADS
    vec = pl.BlockSpec((1, w), lambda bi, ti: (0, 0))
    mat = pl.BlockSpec((LRU_HEADS, hd, hd), lambda bi, ti: (0, 0, 0))
    return pl.pallas_call(
        _lru_kernel,
        out_shape=jax.ShapeDtypeStruct((b, s, w), BF16),
        grid=(b, s // ts),
        in_specs=[
            pl.BlockSpec((None, ts, w), lambda bi, ti: (bi, ti, COL_RX // w)),
            pl.BlockSpec((LRU_CONV_KERNEL, w), lambda bi, ti: (0, 0)),
            vec, mat, vec, mat, vec, vec,
        ],
        out_specs=pl.BlockSpec((None, ts, w), lambda bi, ti: (bi, ti, 0)),
        scratch_shapes=[pltpu.VMEM((ts + LRU_HALO, w), F32), pltpu.VMEM((ts, w), F32),
                        pltpu.VMEM((ts, w), F32), pltpu.VMEM((SUBLANES, w), F32)],
        compiler_params=_params("parallel", "arbitrary"),
        name="lru_branch",
    )(u3, cw, cb, wa, ba, wx, bx, lam)


def _out_proj_kernel(x_ref, yc_ref, ya_ref, yl_ref, gc_ref, ga0_ref, ga1_ref, gl_ref,
                     nc_ref, na_ref, nl_ref, w_ref, o_ref):
    def normed(y_ref, n_ref):
        y = y_ref[...].astype(F32)
        return y * _rms_scale(y) * n_ref[...]

    def gated(yn, gate_ref):
        return (yn * _silu(gate_ref[...].astype(F32))).astype(BF16)

    half = ATTN_WIDTH // 2
    ya = normed(ya_ref, na_ref)
    parts = [
        (gated(normed(yc_ref, nc_ref), gc_ref), 0),
        (gated(ya[:, :half], ga0_ref), CONV_WIDTH),
        (gated(ya[:, half:], ga1_ref), CONV_WIDTH + half),
        (gated(normed(yl_ref, nl_ref), gl_ref), CONV_WIDTH + ATTN_WIDTH),
    ]
    acc = x_ref[...]
    for y, r0 in parts:
        acc = acc + jnp.dot(y, w_ref[r0:r0 + y.shape[1], :], preferred_element_type=F32)
    o_ref[...] = acc


def _out_proj(x, yc, ya, yl, u, nc, na, nl, w, *, tm):
    t, d = x.shape
    gw = 512
    return pl.pallas_call(
        _out_proj_kernel,
        out_shape=jax.ShapeDtypeStruct((t, d), F32),
        grid=(t // tm,),
        in_specs=[
            pl.BlockSpec((tm, d), lambda i: (i, 0)),
            pl.BlockSpec((tm, CONV_WIDTH), lambda i: (i, 0)),
            pl.BlockSpec((tm, ATTN_WIDTH), lambda i: (i, 0)),
            pl.BlockSpec((tm, LRU_WIDTH), lambda i: (i, 0)),
            pl.BlockSpec((tm, gw), lambda i: (i, COL_CGATE // gw)),
            pl.BlockSpec((tm, gw), lambda i: (i, COL_AGATE // gw)),
            pl.BlockSpec((tm, gw), lambda i: (i, COL_AGATE // gw + 1)),
            pl.BlockSpec((tm, gw), lambda i: (i, COL_RGATE // gw)),
            pl.BlockSpec((1, CONV_WIDTH), lambda i: (0, 0)),
            pl.BlockSpec((1, ATTN_WIDTH), lambda i: (0, 0)),
            pl.BlockSpec((1, LRU_WIDTH), lambda i: (0, 0)),
            pl.BlockSpec((d, d), lambda i: (0, 0)),
        ],
        out_specs=pl.BlockSpec((tm, d), lambda i: (i, 0)),
        compiler_params=_params("parallel"),
        name="out_proj",
    )(x, yc, ya, yl, u, u, u, u, nc, na, nl, w)


def _xattn_kernel(x_ref, g_ref, wq_ref, k_ref, v_ref, wo_ref, fg_ref, o_ref, *, final_norm):
    x = x_ref[...]
    h = (x * _rms_scale(x) * g_ref[...]).astype(BF16)
    q = jnp.dot(h, wq_ref[...], preferred_element_type=F32).astype(BF16)
    dh = XATTN_WIDTH // XATTN_HEADS
    scale = dh ** -0.5
    acc = x
    for n in range(XATTN_HEADS):
        qh = q[:, n * dh:(n + 1) * dh]
        kh = k_ref[:, n * dh:(n + 1) * dh]
        vh = v_ref[:, n * dh:(n + 1) * dh]
        s = lax.dot_general(qh, kh, (((1,), (1,)), ((), ())), preferred_element_type=F32) * scale
        e = jnp.exp(s - jnp.max(s, axis=-1, keepdims=True))
        p = (e / jnp.sum(e, axis=-1, keepdims=True)).astype(BF16)
        oh = jnp.dot(p, vh, preferred_element_type=F32).astype(BF16)
        acc = acc + jnp.dot(oh, wo_ref[n * dh:(n + 1) * dh, :], preferred_element_type=F32)
    if final_norm:
        acc = acc * _rms_scale(acc) * fg_ref[...]
    o_ref[...] = acc


def _xattn(x3, g, wq, kv, wo, fg, *, tm, final_norm):
    b, s, d = x3.shape
    m = kv.shape[1]
    xw = XATTN_WIDTH
    return pl.pallas_call(
        functools.partial(_xattn_kernel, final_norm=final_norm),
        out_shape=jax.ShapeDtypeStruct((b, s, d), F32),
        grid=(b, s // tm),
        in_specs=[
            pl.BlockSpec((None, tm, d), lambda bi, ti: (bi, ti, 0)),
            pl.BlockSpec((1, d), lambda bi, ti: (0, 0)),
            pl.BlockSpec((d, xw), lambda bi, ti: (0, 0)),
            pl.BlockSpec((None, m, xw), lambda bi, ti: (bi, 0, 0)),
            pl.BlockSpec((None, m, xw), lambda bi, ti: (bi, 0, 1)),
            pl.BlockSpec((xw, d), lambda bi, ti: (0, 0)),
            pl.BlockSpec((1, d), lambda bi, ti: (0, 0)),
        ],
        out_specs=pl.BlockSpec((None, tm, d), lambda bi, ti: (bi, ti, 0)),
        compiler_params=_params("parallel", "parallel"),
        name="xattn_final" if final_norm else "xattn",
    )(x3, g, wq, kv, kv, wo, fg)


def kernel(x, mem, mix_norm_g, w_in, conv_dw_w, conv_dw_b, conv_ln_g, conv_ln_b, conv_pw_w,
           lru_conv_w, lru_conv_b, lru_wa, lru_ba, lru_wx, lru_bx, lru_lambda,
           out_norm_conv, out_norm_attn, out_norm_lru, w_out,
           xattn_norm_g, mem_norm_g, xattn_wq, xattn_wkv, xattn_wo, final_norm_g):
    b, s, d = x.shape
    m = mem.shape[1]
    depth = w_in.shape[0]
    t = b * s
    row = lambda a: a.reshape(1, -1).astype(F32)

    xt = x.reshape(t, d)
    memt = mem.reshape(b * m, d)
    for l in range(depth):
        u = _norm_matmul(xt, row(mix_norm_g[l]), w_in[l].astype(BF16), tm=1024, tn=512, name="in_proj")
        u3 = u.reshape(b, s, IN_WIDTH)
        y_conv = _conv_branch(u3, conv_dw_w[l], row(conv_dw_b[l]), row(conv_ln_g[l]), row(conv_ln_b[l]),
                              conv_pw_w[l].astype(BF16), ts=512)
        y_attn = _sb_attention(u3)
        y_lru = _lru_branch(u3, lru_conv_w[l], row(lru_conv_b[l]), lru_wa[l].astype(BF16), row(lru_ba[l]),
                            lru_wx[l].astype(BF16), row(lru_bx[l]), row(lru_lambda[l]), ts=512)
        xt = _out_proj(xt, y_conv.reshape(t, -1), y_attn.reshape(t, -1), y_lru.reshape(t, -1), u,
                       row(out_norm_conv[l]), row(out_norm_attn[l]), row(out_norm_lru[l]),
                       w_out[l].astype(BF16), tm=512)
        kv = _norm_matmul(memt, row(mem_norm_g[l]), xattn_wkv[l].astype(BF16), tm=256, tn=1024,
                          name="mem_kv").reshape(b, m, 2 * XATTN_WIDTH)
        xt = _xattn(xt.reshape(b, s, d), row(xattn_norm_g[l]), xattn_wq[l].astype(BF16), kv,
                    xattn_wo[l].astype(BF16), row(final_norm_g), tm=512,
                    final_norm=(l == depth - 1)).reshape(t, d)
    return xt.reshape(b, s, d)
```

```python
import functools

import jax
import jax.numpy as jnp
from jax import lax
from jax.experimental import pallas as pl
from jax.experimental.pallas import tpu as pltpu

F32 = jnp.float32
BF16 = jnp.bfloat16

D_MODEL = 2048
CONV_WIDTH = 512
CONV_KERNEL = 31
HEAD_DIM = 128
ATTN_WIDTH = 1024
ATTN_HEADS = 8
LRU_WIDTH = 512
LRU_HEADS = 4
LRU_CONV_KERNEL = 4
LRU_C = 8.0
XATTN_HEADS = 4
XATTN_WIDTH = 512
IN_WIDTH = 3 * CONV_WIDTH + 4 * ATTN_WIDTH + 2 * LRU_WIDTH

COL_BLOCK = 512
N_COL_BLOCKS = IN_WIDTH // COL_BLOCK
W_BLOCK_LRU_X = 11
N_BRANCH_BLOCKS = 3
U_WIDTH = IN_WIDTH - N_BRANCH_BLOCKS * COL_BLOCK
UCOL_CGATE, UCOL_Q, UCOL_K, UCOL_V, UCOL_AGATE, UCOL_RGATE = 0, 512, 1536, 2560, 3584, 4608

VMEM_LIMIT_BYTES = 56 * 1024 * 1024
SUBLANES = 8
LANES = 128

TM_MIXER_IN = 1024
TM_OUT_PROJ = 512
TM_XATTN = 512

RMS_EPS = 1e-6
LN_EPS = 1e-5


def _params(*sem):
    return pltpu.CompilerParams(dimension_semantics=sem, vmem_limit_bytes=VMEM_LIMIT_BYTES)


def _rms_scale(x):
    return lax.rsqrt(jnp.mean(x * x, axis=-1, keepdims=True) + RMS_EPS)


def _softplus(x):
    return jnp.maximum(x, 0.0) + jnp.log(1.0 + jnp.exp(-jnp.abs(x)))


def _silu(x):
    return x * jax.nn.sigmoid(x)


def _norm_matmul_kernel(x_ref, g_ref, w_ref, o_ref, h_ref):
    @pl.when(pl.program_id(1) == 0)
    def _():
        x = x_ref[...]
        h_ref[...] = (x * _rms_scale(x) * g_ref[...]).astype(BF16)

    o_ref[...] = jnp.dot(h_ref[...], w_ref[...], preferred_element_type=F32).astype(o_ref.dtype)


def _norm_matmul(x, g, w, *, tm, tn, name):
    t, d = x.shape
    n = w.shape[1]
    return pl.pallas_call(
        _norm_matmul_kernel,
        out_shape=jax.ShapeDtypeStruct((t, n), BF16),
        grid=(t // tm, n // tn),
        in_specs=[
            pl.BlockSpec((tm, d), lambda i, j: (i, 0)),
            pl.BlockSpec((1, d), lambda i, j: (0, 0)),
            pl.BlockSpec((d, tn), lambda i, j: (0, j)),
        ],
        out_specs=pl.BlockSpec((tm, tn), lambda i, j: (i, j)),
        scratch_shapes=[pltpu.VMEM((tm, d), BF16)],
        compiler_params=_params("parallel", "arbitrary"),
        name=name,
    )(x, g, w)


CONV_HALO = 32
CONV_ROW_CHUNK = 64
LRU_HALO = SUBLANES
BRANCH_ROWS = 128


def _depthwise_conv_rows(buf_ref, dww_ref, dwb_ref, r0):
    base = CONV_HALO - (CONV_KERNEL - 1)
    rows = CONV_ROW_CHUNK + SUBLANES
    out = []
    for lb in range(CONV_WIDTH // LANES):
        lanes = slice(lb * LANES, (lb + 1) * LANES)
        a = None
        for rho in reversed(range(SUBLANES)):
            q = None
            for m in range((base + CONV_KERNEL - 1) // SUBLANES + 1):
                k = SUBLANES * m + rho - base
                if 0 <= k < CONV_KERNEL:
                    term = dww_ref[k:k + 1, lanes] * buf_ref[pl.ds(r0 + SUBLANES * m, rows), lanes]
                    q = term if q is None else q + term
            a = q if a is None else q + pltpu.roll(a, rows - 1, 0)
        out.append(a[:CONV_ROW_CHUNK] + dwb_ref[:, lanes])
    return jnp.concatenate(out, axis=1)


def _conv_branch_rows(cbuf_ref, dww_ref, dwb_ref, lng_ref, lnb_ref, r0):
    u = jnp.concatenate([_depthwise_conv_rows(cbuf_ref, dww_ref, dwb_ref, r0 + c * CONV_ROW_CHUNK)
                         for c in range(BRANCH_ROWS // CONV_ROW_CHUNK)], axis=0)
    mu = jnp.mean(u, axis=-1, keepdims=True)
    uc = u - mu
    var = jnp.mean(uc * uc, axis=-1, keepdims=True)
    y = uc * lax.rsqrt(var + LN_EPS) * lng_ref[...] + lnb_ref[...]
    return _silu(y).astype(BF16)


def _lru_branch_rows(rbuf_ref, cw_ref, cb_ref, wa_ref, ba_ref, wx_ref, bx_ref, lam_ref, hstate_ref, r0):
    w = LRU_WIDTH
    n = BRANCH_ROWS
    window = rbuf_ref[pl.ds(r0, n + LRU_HALO), :]
    xc = jnp.broadcast_to(cb_ref[...], (n, w))
    for k in range(LRU_CONV_KERNEL):
        shift = LRU_HALO - (LRU_CONV_KERNEL - 1) + k
        if shift % SUBLANES == 0:
            tap = window[shift:shift + n]
        else:
            tap = pltpu.roll(window, n + LRU_HALO - shift, 0)[:n]
        xc = xc + cw_ref[k:k + 1, :] * tap

    xcb = xc.astype(BF16)
    hd = w // LRU_HEADS
    rs, gs = [], []
    for hix in range(LRU_HEADS):
        xh = xcb[:, hix * hd:(hix + 1) * hd]
        rs.append(jnp.dot(xh, wa_ref[hix], preferred_element_type=F32))
        gs.append(jnp.dot(xh, wx_ref[hix], preferred_element_type=F32))
    r = jax.nn.sigmoid(jnp.concatenate(rs, axis=-1) + ba_ref[...])
    gate = jax.nn.sigmoid(jnp.concatenate(gs, axis=-1) + bx_ref[...])
    log_a = (-LRU_C) * r * _softplus(-lam_ref[...])
    a = jnp.exp(log_a)
    b = jnp.sqrt(jnp.tanh(-log_a) * (1.0 + a * a)) * (gate * xc)

    row = lax.broadcasted_iota(jnp.int32, (SUBLANES, w), 0)
    h_prev = hstate_ref[...]
    hs = []
    for g in range(n // SUBLANES):
        ag = a[g * SUBLANES:(g + 1) * SUBLANES]
        bg = b[g * SUBLANES:(g + 1) * SUBLANES]
        d = 1
        while d < SUBLANES:
            a_sh = pltpu.roll(ag, d, 0)
            b_sh = pltpu.roll(bg, d, 0)
            m = row >= d
            bg = jnp.where(m, ag * b_sh + bg, bg)
            ag = jnp.where(m, ag * a_sh, ag)
            d *= 2
        h = ag * h_prev + bg
        hs.append(h)
        h_prev = jnp.broadcast_to(h[SUBLANES - 1:SUBLANES, :], (SUBLANES, w))
    hstate_ref[...] = h_prev
    return jnp.concatenate(hs, axis=0)


def _mixer_in_kernel(x_ref, g_ref, w_ref,
                     dww_ref, dwb_ref, lng_ref, lnb_ref, pw_ref,
                     cw_ref, cb_ref, wa_ref, ba_ref, wx_ref, bx_ref, lam_ref,
                     u_ref, yc_ref, yl_ref,
                     h_ref, val_ref, cbuf_ref, rbuf_ref, hstate_ref, *, tiles_per_seq):
    i = pl.program_id(0)
    j = pl.program_id(1)
    tm = x_ref.shape[0]
    n_branch_steps = tm // BRANCH_ROWS
    first_out_step = N_BRANCH_BLOCKS

    def project():
        return jnp.dot(h_ref[...], w_ref[...], preferred_element_type=F32)

    @pl.when(j == 0)
    def _():
        @pl.when(i % tiles_per_seq == 0)
        def _():
            cbuf_ref[0:CONV_HALO, :] = jnp.zeros((CONV_HALO, CONV_WIDTH), F32)
            rbuf_ref[0:LRU_HALO, :] = jnp.zeros((LRU_HALO, LRU_WIDTH), F32)
            hstate_ref[...] = jnp.zeros_like(hstate_ref)
            cbuf_ref[CONV_HALO + tm:CONV_HALO + tm + SUBLANES, :] = jnp.zeros((SUBLANES, CONV_WIDTH), F32)

        x = x_ref[...]
        h_ref[...] = (x * _rms_scale(x) * g_ref[...]).astype(BF16)
        val_ref[...] = project()

    @pl.when(j == 1)
    def _():
        cbuf_ref[CONV_HALO:CONV_HALO + tm, :] = val_ref[...] * jax.nn.sigmoid(project())

    @pl.when(j == 2)
    def _():
        rbuf_ref[LRU_HALO:LRU_HALO + tm, :] = project()

    @pl.when(jnp.logical_and(j >= first_out_step, j < first_out_step + n_branch_steps))
    def _():
        r0 = pl.multiple_of((j - first_out_step) * BRANCH_ROWS, BRANCH_ROWS)
        yl = _lru_branch_rows(rbuf_ref, cw_ref, cb_ref, wa_ref, ba_ref, wx_ref, bx_ref, lam_ref, hstate_ref, r0)
        yl_ref[pl.ds(r0, BRANCH_ROWS), :] = yl.astype(yl_ref.dtype)
        u_ref[...] = project().astype(u_ref.dtype)
        yc_in = _conv_branch_rows(cbuf_ref, dww_ref, dwb_ref, lng_ref, lnb_ref, r0)
        yc = jnp.dot(yc_in, pw_ref[...], preferred_element_type=F32)
        yc_ref[pl.ds(r0, BRANCH_ROWS), :] = yc.astype(yc_ref.dtype)

    @pl.when(j >= first_out_step + n_branch_steps)
    def _():
        u_ref[...] = project().astype(u_ref.dtype)

        @pl.when(j == first_out_step + n_branch_steps)
        def _():
            cbuf_ref[0:CONV_HALO, :] = cbuf_ref[tm:tm + CONV_HALO, :]
            rbuf_ref[0:LRU_HALO, :] = rbuf_ref[tm:tm + LRU_HALO, :]


def _mixer_in(x, g, w, conv_params, lru_params, *, seq_len):
    t, d = x.shape
    tm = TM_MIXER_IN
    assert seq_len % tm == 0 and tm % BRANCH_ROWS == 0 and BRANCH_ROWS % CONV_ROW_CHUNK == 0
    assert N_BRANCH_BLOCKS + tm // BRANCH_ROWS < N_COL_BLOCKS
    dww, dwb, lng, lnb, pw = conv_params
    cw, cb, wa, ba, wx, bx, lam = lru_params
    hd = LRU_WIDTH // LRU_HEADS

    def w_block(i, j):
        blk = jnp.where(j == 2, W_BLOCK_LRU_X, jnp.where(jnp.logical_and(j > 2, j <= W_BLOCK_LRU_X), j - 1, j))
        return (0, blk)

    const = lambda shape: pl.BlockSpec(shape, lambda i, j: (0,) * len(shape))
    return pl.pallas_call(
        functools.partial(_mixer_in_kernel, tiles_per_seq=seq_len // tm),
        out_shape=(jax.ShapeDtypeStruct((t, U_WIDTH), BF16),
                   jax.ShapeDtypeStruct((t, CONV_WIDTH), BF16),
                   jax.ShapeDtypeStruct((t, LRU_WIDTH), BF16)),
        grid=(t // tm, N_COL_BLOCKS),
        in_specs=[
            pl.BlockSpec((tm, d), lambda i, j: (i, 0)),
            const((1, d)),
            pl.BlockSpec((d, COL_BLOCK), w_block),
            const((CONV_KERNEL, CONV_WIDTH)), const((1, CONV_WIDTH)), const((1, CONV_WIDTH)),
            const((1, CONV_WIDTH)), const((CONV_WIDTH, CONV_WIDTH)),
            const((LRU_CONV_KERNEL, LRU_WIDTH)), const((1, LRU_WIDTH)),
            const((LRU_HEADS, hd, hd)), const((1, LRU_WIDTH)),
            const((LRU_HEADS, hd, hd)), const((1, LRU_WIDTH)), const((1, LRU_WIDTH)),
        ],
        out_specs=(
            pl.BlockSpec((tm, COL_BLOCK), lambda i, j: (i, jnp.maximum(j - N_BRANCH_BLOCKS, 0))),
            pl.BlockSpec((tm, CONV_WIDTH), lambda i, j: (i, 0)),
            pl.BlockSpec((tm, LRU_WIDTH), lambda i, j: (i, 0)),
        ),
        scratch_shapes=[
            pltpu.VMEM((tm, d), BF16),
            pltpu.VMEM((tm, CONV_WIDTH), F32),
            pltpu.VMEM((CONV_HALO + tm + SUBLANES, CONV_WIDTH), F32),
            pltpu.VMEM((LRU_HALO + tm, LRU_WIDTH), F32),
            pltpu.VMEM((SUBLANES, LRU_WIDTH), F32),
        ],
        compiler_params=_params("arbitrary", "arbitrary"),
        name="mixer_in",
    )(x, g, w, dww, dwb, lng, lnb, pw, cw, cb, wa, ba, wx, bx, lam)


SB_T = 128
SB_GROUP = 4
SB_EAGER = 3
SB_PAD = SB_GROUP - 1
SB_LOG_WEIGHT_FLOOR = -104.0


def _sb_split(log_1mb):
    hi = log_1mb.astype(BF16)
    lo = (log_1mb - hi.astype(F32)).astype(BF16)
    return jnp.concatenate([hi, lo], axis=1)


def _sb_attn_kernel(q_ref, k_ref, v_ref, uo_ref, o_ref, kp_ref, vp_ref, acc_ref, carry_ref):
    s_len = q_ref.shape[0]
    scale = HEAD_DIM ** -0.5
    t_sz, n_grp = SB_T, SB_GROUP
    pad_rows = SB_PAD * t_sz
    kp_ref[0:pad_rows, :] = jnp.zeros((pad_rows, HEAD_DIM), BF16)
    vp_ref[0:pad_rows, :] = jnp.zeros((pad_rows, HEAD_DIM), BF16)
    kp_ref[pad_rows:pad_rows + s_len, :] = k_ref[...]
    vp_ref[pad_rows:pad_rows + s_len, :] = v_ref[...]
    causal = (lax.broadcasted_iota(jnp.int32, (t_sz, t_sz), 1)
              < lax.broadcasted_iota(jnp.int32, (t_sz, t_sz), 0))

    def q_tile(i):
        return q_ref[pl.ds(pl.multiple_of(i * t_sz, t_sz), t_sz), :]

    def neg_scores(q, k):
        zn = lax.dot_general(q, k, (((1,), (1,)), ((), ())), preferred_element_type=F32) * (-scale)
        log_1mb = jnp.minimum(zn, 0.0) - jnp.log(1.0 + jnp.exp(-jnp.abs(zn)))
        return zn, log_1mb

    def group(g, _):
        i0 = g * n_grp

        def eager():
            width = SB_EAGER * t_sz
            zns, ls, starts, lhs = [], [], [], []
            for r in range(n_grp):
                start = pl.multiple_of((i0 + r + SB_PAD - (SB_EAGER - 1)) * t_sz, t_sz)
                zn, log_1mb = neg_scores(q_tile(i0 + r), kp_ref[pl.ds(start, width), :])
                pieces = []
                for c in range(SB_EAGER):
                    lc = log_1mb[:, c * t_sz:(c + 1) * t_sz]
                    if c == SB_EAGER - 1:
                        lc = jnp.where(causal, lc, 0.0)
                    pieces.append(_sb_split(lc))
                lhs.append(jnp.concatenate(pieces, axis=0))
                zns.append(zn)
                ls.append(log_1mb)
                starts.append(start)
            sums = [jnp.dot(x, uo_ref[...], preferred_element_type=F32) for x in lhs]
            m = None
            for r in range(n_grp):
                carry = None
                ws = [None] * SB_EAGER
                for c in reversed(range(SB_EAGER)):
                    cols = slice(c * t_sz, (c + 1) * t_sz)
                    sc = sums[r][cols]
                    after = sc[:, :t_sz] if carry is None else carry + sc[:, :t_sz]
                    w = jnp.exp((ls[r][:, cols] - zns[r][:, cols]) + after)
                    if c == SB_EAGER - 1:
                        w = jnp.where(causal, w, 0.0)
                    ws[c] = w.astype(BF16)
                    carry = sc[:, t_sz:] if carry is None else carry + sc[:, t_sz:]
                v = vp_ref[pl.ds(starts[r], width), :]
                acc_ref[r] = jnp.dot(jnp.concatenate(ws, axis=1), v, preferred_element_type=F32)
                carry_ref[r] = carry
                m = carry if m is None else jnp.maximum(m, carry)
            return jnp.max(m)

        def step(d):
            zns, ls, starts, lhs = [], [], [], []
            for r in range(n_grp):
                start = pl.multiple_of((i0 + r + SB_PAD - d) * t_sz, t_sz)
                zn, log_1mb = neg_scores(q_tile(i0 + r), kp_ref[pl.ds(start, t_sz), :])
                lhs.append(_sb_split(log_1mb))
                zns.append(zn)
                ls.append(log_1mb)
                starts.append(start)
            sums = jnp.dot(jnp.concatenate(lhs, axis=0), uo_ref[...], preferred_element_type=F32)
            m = None
            for r in range(n_grp):
                sr = sums[r * t_sz:(r + 1) * t_sz]
                w = jnp.exp((ls[r] - zns[r]) + (carry_ref[r] + sr[:, :t_sz]))
                v = vp_ref[pl.ds(starts[r], t_sz), :]
                acc_ref[r] += jnp.dot(w.astype(BF16), v, preferred_element_type=F32)
                carry = carry_ref[r] + sr[:, t_sz:]
                carry_ref[r] = carry
                m = carry if m is None else jnp.maximum(m, carry)
            return jnp.max(m)

        def more(c):
            d, m = c
            return jnp.logical_and(d <= i0 + n_grp - 1, m > SB_LOG_WEIGHT_FLOOR)

        lax.while_loop(more, lambda c: (c[0] + 1, step(c[0])), (jnp.int32(SB_EAGER), eager()))
        for r in range(n_grp):
            o_ref[pl.ds(pl.multiple_of((i0 + r) * t_sz, t_sz), t_sz), :] = acc_ref[r].astype(o_ref.dtype)
        return 0

    lax.fori_loop(0, s_len // (t_sz * n_grp), group, 0)


def _sb_attention(u3):
    b, s, _ = u3.shape
    dh = HEAD_DIM
    assert SB_EAGER - 1 <= SB_PAD and s % (SB_T * SB_GROUP) == 0
    row = lax.broadcasted_iota(jnp.int32, (2 * SB_T, 2 * SB_T), 0) % SB_T
    col = lax.broadcasted_iota(jnp.int32, (2 * SB_T, 2 * SB_T), 1)
    uo = jnp.where((col >= SB_T) | (row > col), 1.0, 0.0).astype(BF16)

    def head_spec(col0):
        return pl.BlockSpec((None, s, dh), lambda bi, hi: (bi, 0, col0 // dh + hi))

    padded = pltpu.VMEM((s + SB_PAD * SB_T, dh), BF16)
    return pl.pallas_call(
        _sb_attn_kernel,
        out_shape=jax.ShapeDtypeStruct((b, s, ATTN_WIDTH), BF16),
        grid=(b, ATTN_HEADS),
        in_specs=[head_spec(UCOL_Q), head_spec(UCOL_K), head_spec(UCOL_V),
                  pl.BlockSpec((2 * SB_T, 2 * SB_T), lambda bi, hi: (0, 0))],
        out_specs=pl.BlockSpec((None, s, dh), lambda bi, hi: (bi, 0, hi)),
        scratch_shapes=[padded, padded,
                        pltpu.VMEM((SB_GROUP, SB_T, dh), F32), pltpu.VMEM((SB_GROUP, SB_T, SB_T), F32)],
        compiler_params=_params("parallel", "parallel"),
        name="sb_attention",
    )(u3, u3, u3, uo)


def _out_proj_kernel(x_ref, yc_ref, ya_ref, yl_ref, gc_ref, ga0_ref, ga1_ref, gl_ref,
                     nc_ref, na_ref, nl_ref, w_ref, o_ref):
    def normed(y_ref, n_ref):
        y = y_ref[...].astype(F32)
        return y * _rms_scale(y) * n_ref[...]

    def gated(yn, gate_ref):
        return (yn * _silu(gate_ref[...].astype(F32))).astype(BF16)

    half = ATTN_WIDTH // 2
    ya = normed(ya_ref, na_ref)
    parts = [
        (gated(normed(yc_ref, nc_ref), gc_ref), 0),
        (gated(ya[:, :half], ga0_ref), CONV_WIDTH),
        (gated(ya[:, half:], ga1_ref), CONV_WIDTH + half),
        (gated(normed(yl_ref, nl_ref), gl_ref), CONV_WIDTH + ATTN_WIDTH),
    ]
    acc = x_ref[...]
    for y, r0 in parts:
        acc = acc + jnp.dot(y, w_ref[r0:r0 + y.shape[1], :], preferred_element_type=F32)
    o_ref[...] = acc


def _out_proj(x, yc, ya, yl, u, nc, na, nl, w):
    t, d = x.shape
    tm = TM_OUT_PROJ
    gw = COL_BLOCK
    return pl.pallas_call(
        _out_proj_kernel,
        out_shape=jax.ShapeDtypeStruct((t, d), F32),
        grid=(t // tm,),
        in_specs=[
            pl.BlockSpec((tm, d), lambda i: (i, 0)),
            pl.BlockSpec((tm, CONV_WIDTH), lambda i: (i, 0)),
            pl.BlockSpec((tm, ATTN_WIDTH), lambda i: (i, 0)),
            pl.BlockSpec((tm, LRU_WIDTH), lambda i: (i, 0)),
            pl.BlockSpec((tm, gw), lambda i: (i, UCOL_CGATE // gw)),
            pl.BlockSpec((tm, gw), lambda i: (i, UCOL_AGATE // gw)),
            pl.BlockSpec((tm, gw), lambda i: (i, UCOL_AGATE // gw + 1)),
            pl.BlockSpec((tm, gw), lambda i: (i, UCOL_RGATE // gw)),
            pl.BlockSpec((1, CONV_WIDTH), lambda i: (0, 0)),
            pl.BlockSpec((1, ATTN_WIDTH), lambda i: (0, 0)),
            pl.BlockSpec((1, LRU_WIDTH), lambda i: (0, 0)),
            pl.BlockSpec((d, d), lambda i: (0, 0)),
        ],
        out_specs=pl.BlockSpec((tm, d), lambda i: (i, 0)),
        compiler_params=_params("parallel"),
        name="out_proj",
    )(x, yc, ya, yl, u, u, u, u, nc, na, nl, w)


def _xattn_kernel(x_ref, g_ref, wq_ref, k_ref, v_ref, wo_ref, fg_ref, o_ref, *, final_norm):
    x = x_ref[...]
    h = (x * _rms_scale(x) * g_ref[...]).astype(BF16)
    q = jnp.dot(h, wq_ref[...], preferred_element_type=F32).astype(BF16)
    dh = XATTN_WIDTH // XATTN_HEADS
    scale = dh ** -0.5
    acc = x
    for n in range(XATTN_HEADS):
        qh = q[:, n * dh:(n + 1) * dh]
        kh = k_ref[:, n * dh:(n + 1) * dh]
        vh = v_ref[:, n * dh:(n + 1) * dh]
        s = lax.dot_general(qh, kh, (((1,), (1,)), ((), ())), preferred_element_type=F32) * scale
        e = jnp.exp(s - jnp.max(s, axis=-1, keepdims=True))
        p = (e / jnp.sum(e, axis=-1, keepdims=True)).astype(BF16)
        oh = jnp.dot(p, vh, preferred_element_type=F32).astype(BF16)
        acc = acc + jnp.dot(oh, wo_ref[n * dh:(n + 1) * dh, :], preferred_element_type=F32)
    if final_norm:
        acc = acc * _rms_scale(acc) * fg_ref[...]
    o_ref[...] = acc


def _xattn(x3, g, wq, kv, wo, fg, *, final_norm):
    b, s, d = x3.shape
    m = kv.shape[1]
    tm = TM_XATTN
    xw = XATTN_WIDTH
    return pl.pallas_call(
        functools.partial(_xattn_kernel, final_norm=final_norm),
        out_shape=jax.ShapeDtypeStruct((b, s, d), F32),
        grid=(b, s // tm),
        in_specs=[
            pl.BlockSpec((None, tm, d), lambda bi, ti: (bi, ti, 0)),
            pl.BlockSpec((1, d), lambda bi, ti: (0, 0)),
            pl.BlockSpec((d, xw), lambda bi, ti: (0, 0)),
            pl.BlockSpec((None, m, xw), lambda bi, ti: (bi, 0, 0)),
            pl.BlockSpec((None, m, xw), lambda bi, ti: (bi, 0, 1)),
            pl.BlockSpec((xw, d), lambda bi, ti: (0, 0)),
            pl.BlockSpec((1, d), lambda bi, ti: (0, 0)),
        ],
        out_specs=pl.BlockSpec((None, tm, d), lambda bi, ti: (bi, ti, 0)),
        compiler_params=_params("parallel", "parallel"),
        name="xattn_final" if final_norm else "xattn",
    )(x3, g, wq, kv, kv, wo, fg)


def kernel(x, mem, mix_norm_g, w_in, conv_dw_w, conv_dw_b, conv_ln_g, conv_ln_b, conv_pw_w,
           lru_conv_w, lru_conv_b, lru_wa, lru_ba, lru_wx, lru_bx, lru_lambda,
           out_norm_conv, out_norm_attn, out_norm_lru, w_out,
           xattn_norm_g, mem_norm_g, xattn_wq, xattn_wkv, xattn_wo, final_norm_g):
    b, s, d = x.shape
    m = mem.shape[1]
    depth = w_in.shape[0]
    t = b * s
    row = lambda a: a.reshape(1, -1).astype(F32)

    xt = x.reshape(t, d)
    memt = mem.reshape(b * m, d)
    for l in range(depth):
        conv_params = (conv_dw_w[l], row(conv_dw_b[l]), row(conv_ln_g[l]), row(conv_ln_b[l]),
                       conv_pw_w[l].astype(BF16))
        lru_params = (lru_conv_w[l], row(lru_conv_b[l]), lru_wa[l].astype(BF16), row(lru_ba[l]),
                      lru_wx[l].astype(BF16), row(lru_bx[l]), row(lru_lambda[l]))
        u, y_conv, y_lru = _mixer_in(xt, row(mix_norm_g[l]), w_in[l].astype(BF16), conv_params, lru_params,
                                     seq_len=s)
        y_attn = _sb_attention(u.reshape(b, s, U_WIDTH))
        xt = _out_proj(xt, y_conv, y_attn.reshape(t, -1), y_lru, u,
                       row(out_norm_conv[l]), row(out_norm_attn[l]), row(out_norm_lru[l]),
                       w_out[l].astype(BF16))
        kv = _norm_matmul(memt, row(mem_norm_g[l]), xattn_wkv[l].astype(BF16), tm=256, tn=1024,
                          name="mem_kv").reshape(b, m, 2 * XATTN_WIDTH)
        xt = _xattn(xt.reshape(b, s, d), row(xattn_norm_g[l]), xattn_wq[l].astype(BF16), kv,
                    xattn_wo[l].astype(BF16), row(final_norm_g),
                    final_norm=(l == depth - 1)).reshape(t, d)
    return xt.reshape(b, s, d)
```

```python
import functools

import jax
import jax.numpy as jnp
from jax import lax
from jax.experimental import pallas as pl
from jax.experimental.pallas import tpu as pltpu

F32 = jnp.float32
BF16 = jnp.bfloat16

D_MODEL = 2048
CONV_WIDTH = 512
CONV_KERNEL = 31
HEAD_DIM = 128
ATTN_WIDTH = 1024
ATTN_HEADS = 8
LRU_WIDTH = 512
LRU_HEADS = 4
LRU_CONV_KERNEL = 4
LRU_C = 8.0
XATTN_HEADS = 4
XATTN_WIDTH = 512
IN_WIDTH = 3 * CONV_WIDTH + 4 * ATTN_WIDTH + 2 * LRU_WIDTH

COL_BLOCK = 512
N_COL_BLOCKS = IN_WIDTH // COL_BLOCK
W_BLOCK_LRU_X = 11
N_BRANCH_BLOCKS = 3
U_WIDTH = IN_WIDTH - N_BRANCH_BLOCKS * COL_BLOCK
UCOL_CGATE, UCOL_Q, UCOL_K, UCOL_V, UCOL_AGATE, UCOL_RGATE = 0, 512, 1536, 2560, 3584, 4608

VMEM_LIMIT_BYTES = 56 * 1024 * 1024
SUBLANES = 8
LANES = 128

TM_MIXER_IN = 1024
TM_OUT_PROJ = 512
TM_XATTN = 512

RMS_EPS = 1e-6
LN_EPS = 1e-5


def _params(*sem):
    return pltpu.CompilerParams(dimension_semantics=sem, vmem_limit_bytes=VMEM_LIMIT_BYTES)


def _rms_scale(x):
    return lax.rsqrt(jnp.mean(x * x, axis=-1, keepdims=True) + RMS_EPS)


def _softplus(x):
    return jnp.maximum(x, 0.0) + jnp.log(1.0 + jnp.exp(-jnp.abs(x)))


def _silu(x):
    return x * jax.nn.sigmoid(x)


def _norm_matmul_kernel(x_ref, g_ref, w_ref, o_ref, h_ref):
    @pl.when(pl.program_id(1) == 0)
    def _():
        x = x_ref[...]
        h_ref[...] = (x * _rms_scale(x) * g_ref[...]).astype(BF16)

    o_ref[...] = jnp.dot(h_ref[...], w_ref[...], preferred_element_type=F32).astype(o_ref.dtype)


def _norm_matmul(x, g, w, layer, *, tm, tn, name):
    t, d = x.shape
    n = w.shape[2]
    return pl.pallas_call(
        _norm_matmul_kernel,
        out_shape=jax.ShapeDtypeStruct((t, n), BF16),
        grid=(t // tm, n // tn),
        in_specs=[
            pl.BlockSpec((tm, d), lambda i, j: (i, 0)),
            pl.BlockSpec((1, d), lambda i, j: (0, 0)),
            pl.BlockSpec((None, d, tn), lambda i, j: (layer, 0, j)),
        ],
        out_specs=pl.BlockSpec((tm, tn), lambda i, j: (i, j)),
        scratch_shapes=[pltpu.VMEM((tm, d), BF16)],
        compiler_params=_params("parallel", "arbitrary"),
        name=name,
    )(x, g, w)


CONV_HALO = 32
CONV_ROW_CHUNK = 64
BRANCH_STEP_ROWS = 128
LRU_HALO = SUBLANES


def _depthwise_conv_rows(buf_ref, dww_ref, dwb_ref, r0):
    base = CONV_HALO - (CONV_KERNEL - 1)
    rows = CONV_ROW_CHUNK + SUBLANES
    n_m = (base + CONV_KERNEL - 1) // SUBLANES + 1
    out = []
    for lb in range(CONV_WIDTH // LANES):
        lanes = slice(lb * LANES, (lb + 1) * LANES)
        window = buf_ref[pl.ds(r0, rows + SUBLANES * (n_m - 1)), lanes]
        a = None
        for rho in reversed(range(SUBLANES)):
            q = None
            for m in range(n_m):
                k = SUBLANES * m + rho - base
                if 0 <= k < CONV_KERNEL:
                    term = dww_ref[k:k + 1, lanes] * window[SUBLANES * m:SUBLANES * m + rows]
                    q = term if q is None else q + term
            a = q if a is None else q + pltpu.roll(a, rows - 1, 0)
        out.append(a[:CONV_ROW_CHUNK] + dwb_ref[:, lanes])
    return jnp.concatenate(out, axis=1)


def _conv_branch_rows(cbuf_ref, dww_ref, dwb_ref, lng_ref, lnb_ref, r0):
    u = jnp.concatenate([_depthwise_conv_rows(cbuf_ref, dww_ref, dwb_ref, r0 + c * CONV_ROW_CHUNK)
                         for c in range(BRANCH_STEP_ROWS // CONV_ROW_CHUNK)], axis=0)
    mu = jnp.mean(u, axis=-1, keepdims=True)
    uc = u - mu
    var = jnp.mean(uc * uc, axis=-1, keepdims=True)
    y = uc * lax.rsqrt(var + LN_EPS) * lng_ref[...] + lnb_ref[...]
    return _silu(y).astype(BF16)


def _lru_branch_rows(rbuf_ref, cw_ref, cb_ref, wg_ref, ba_ref, bx_ref, lam_ref, hstate_ref, a_ref, b_ref,
                     yl_ref, r0):
    w = LRU_WIDTH
    n = BRANCH_STEP_ROWS
    window = rbuf_ref[pl.ds(r0, n + LRU_HALO), :]
    xc = jnp.broadcast_to(cb_ref[...], (n, w))
    for k in range(LRU_CONV_KERNEL):
        shift = LRU_HALO - (LRU_CONV_KERNEL - 1) + k
        if shift % SUBLANES == 0:
            tap = window[shift:shift + n]
        else:
            tap = pltpu.roll(window, n + LRU_HALO - shift, 0)[:n]
        xc = xc + cw_ref[k:k + 1, :] * tap

    xcb = xc.astype(BF16)
    hd = w // LRU_HEADS
    pre = [jnp.dot(xcb[:, hix * hd:(hix + 1) * hd], wg_ref[hix], preferred_element_type=F32)
           for hix in range(LRU_HEADS)]
    r = jax.nn.sigmoid(jnp.concatenate([p[:, :hd] for p in pre], axis=-1) + ba_ref[...])
    gate = jax.nn.sigmoid(jnp.concatenate([p[:, hd:] for p in pre], axis=-1) + bx_ref[...])
    log_a = (-LRU_C) * r * _softplus(-lam_ref[...])
    a = jnp.exp(log_a)
    a_ref[...] = a
    b_ref[...] = jnp.sqrt(jnp.tanh(-log_a) * (1.0 + a * a)) * (gate * xc)

    row = lax.broadcasted_iota(jnp.int32, (SUBLANES, w), 0)
    h_prev = hstate_ref[...]
    for g in range(n // SUBLANES):
        rows = slice(g * SUBLANES, (g + 1) * SUBLANES)
        ag = a_ref[rows, :]
        bg = b_ref[rows, :]
        d = 1
        while d < SUBLANES:
            a_sh = pltpu.roll(ag, d, 0)
            b_sh = pltpu.roll(bg, d, 0)
            m = row >= d
            bg = jnp.where(m, ag * b_sh + bg, bg)
            ag = jnp.where(m, ag * a_sh, ag)
            d *= 2
        h = ag * h_prev + bg
        b_ref[rows, :] = h
        h_prev = jnp.broadcast_to(h[SUBLANES - 1:SUBLANES, :], (SUBLANES, w))
    hstate_ref[...] = h_prev
    yl_ref[pl.ds(r0, n), :] = b_ref[...].astype(yl_ref.dtype)


def _mixer_in_kernel(x0_ref, x1_ref, x2_ref, x3_ref, g_ref, w_ref,
                     dww_ref, dwb_ref, lng_ref, lnb_ref,
                     cw_ref, cb_ref, wg_ref, ba_ref, bx_ref, lam_ref,
                     u_ref, yc_ref, yl_ref,
                     h_ref, val_ref, cbuf_ref, rbuf_ref, hstate_ref, a_ref, b_ref, *, tiles_per_seq):
    i = pl.program_id(0)
    j = pl.program_id(1)
    x_refs = (x0_ref, x1_ref, x2_ref, x3_ref)
    tm = x0_ref.shape[0]
    first_branch_step = N_BRANCH_BLOCKS
    n_branch_steps = tm // BRANCH_STEP_ROWS

    def project():
        return jnp.dot(h_ref[...], w_ref[...], preferred_element_type=F32)

    @pl.when(j == 0)
    def _():
        @pl.when(i % tiles_per_seq == 0)
        def _():
            cbuf_ref[0:CONV_HALO, :] = jnp.zeros((CONV_HALO, CONV_WIDTH), F32)
            rbuf_ref[0:LRU_HALO, :] = jnp.zeros((LRU_HALO, LRU_WIDTH), F32)
            hstate_ref[...] = jnp.zeros_like(hstate_ref)
            cbuf_ref[CONV_HALO + tm:CONV_HALO + tm + SUBLANES, :] = jnp.zeros((SUBLANES, CONV_WIDTH), F32)

        @pl.when(i % tiles_per_seq != 0)
        def _():
            cbuf_ref[0:CONV_HALO, :] = cbuf_ref[tm:tm + CONV_HALO, :]
            rbuf_ref[0:LRU_HALO, :] = rbuf_ref[tm:tm + LRU_HALO, :]

        d = h_ref.shape[1]
        cw = d // len(x_refs)
        sq = None
        for xr in x_refs:
            xc = xr[...]
            part = jnp.sum(xc * xc, axis=-1, keepdims=True)
            sq = part if sq is None else sq + part
        inv = lax.rsqrt(sq * (1.0 / d) + RMS_EPS)
        for c, xr in enumerate(x_refs):
            h_ref[:, c * cw:(c + 1) * cw] = (xr[...] * inv * g_ref[:, c * cw:(c + 1) * cw]).astype(BF16)
        val_ref[...] = project()

    @pl.when(j == 1)
    def _():
        cbuf_ref[CONV_HALO:CONV_HALO + tm, :] = val_ref[...] * jax.nn.sigmoid(project())

    @pl.when(j == 2)
    def _():
        rbuf_ref[LRU_HALO:LRU_HALO + tm, :] = project()

    @pl.when(jnp.logical_and(j >= first_branch_step, j < first_branch_step + n_branch_steps))
    def _():
        r0 = pl.multiple_of((j - first_branch_step) * BRANCH_STEP_ROWS, BRANCH_STEP_ROWS)
        _lru_branch_rows(rbuf_ref, cw_ref, cb_ref, wg_ref, ba_ref, bx_ref, lam_ref, hstate_ref, a_ref, b_ref,
                         yl_ref, r0)
        yc_ref[pl.ds(r0, BRANCH_STEP_ROWS), :] = _conv_branch_rows(cbuf_ref, dww_ref, dwb_ref, lng_ref, lnb_ref, r0)
        u_ref[...] = project().astype(u_ref.dtype)

    @pl.when(j >= first_branch_step + n_branch_steps)
    def _():
        u_ref[...] = project().astype(u_ref.dtype)


X_CHUNKS = 4


def _mixer_in(x, g, w, layer, conv_params, lru_params, *, seq_len):
    t, d = x.shape
    tm = TM_MIXER_IN
    n_tiles = t // tm
    assert seq_len % tm == 0 and tm % BRANCH_STEP_ROWS == 0 and BRANCH_STEP_ROWS % CONV_ROW_CHUNK == 0
    assert N_BRANCH_BLOCKS + tm // BRANCH_STEP_ROWS <= N_COL_BLOCKS and d % X_CHUNKS == 0
    dww, dwb, lng, lnb = conv_params
    cw, cb, wg, ba, bx, lam = lru_params
    hd = LRU_WIDTH // LRU_HEADS

    def w_block(i, j):
        blk = jnp.where(j == 2, W_BLOCK_LRU_X, jnp.where(jnp.logical_and(j > 2, j <= W_BLOCK_LRU_X), j - 1, j))
        return (layer, 0, blk)

    def x_chunk(c):
        def index(i, j):
            ahead = (j >= N_COL_BLOCKS - X_CHUNKS + c).astype(jnp.int32)
            return (jnp.minimum(i + ahead, n_tiles - 1), c)
        return pl.BlockSpec((tm, d // X_CHUNKS), index)

    const = lambda shape: pl.BlockSpec(shape, lambda i, j: (0,) * len(shape))
    return pl.pallas_call(
        functools.partial(_mixer_in_kernel, tiles_per_seq=seq_len // tm),
        out_shape=(jax.ShapeDtypeStruct((t, U_WIDTH), BF16),
                   jax.ShapeDtypeStruct((t, CONV_WIDTH), BF16),
                   jax.ShapeDtypeStruct((t, LRU_WIDTH), BF16)),
        grid=(n_tiles, N_COL_BLOCKS),
        in_specs=[
            *[x_chunk(c) for c in range(X_CHUNKS)],
            const((1, d)),
            pl.BlockSpec((None, d, COL_BLOCK), w_block),
            const((CONV_KERNEL, CONV_WIDTH)), const((1, CONV_WIDTH)), const((1, CONV_WIDTH)),
            const((1, CONV_WIDTH)),
            const((LRU_CONV_KERNEL, LRU_WIDTH)), const((1, LRU_WIDTH)),
            const((LRU_HEADS, hd, 2 * hd)), const((1, LRU_WIDTH)), const((1, LRU_WIDTH)), const((1, LRU_WIDTH)),
        ],
        out_specs=(
            pl.BlockSpec((tm, COL_BLOCK), lambda i, j: (i, jnp.maximum(j - N_BRANCH_BLOCKS, 0))),
            pl.BlockSpec((tm, CONV_WIDTH), lambda i, j: (i, 0)),
            pl.BlockSpec((tm, LRU_WIDTH), lambda i, j: (i, 0)),
        ),
        scratch_shapes=[
            pltpu.VMEM((tm, d), BF16),
            pltpu.VMEM((tm, CONV_WIDTH), F32),
            pltpu.VMEM((CONV_HALO + tm + SUBLANES, CONV_WIDTH), F32),
            pltpu.VMEM((LRU_HALO + tm, LRU_WIDTH), F32),
            pltpu.VMEM((SUBLANES, LRU_WIDTH), F32),
            pltpu.VMEM((BRANCH_STEP_ROWS, LRU_WIDTH), F32),
            pltpu.VMEM((BRANCH_STEP_ROWS, LRU_WIDTH), F32),
        ],
        compiler_params=_params("arbitrary", "arbitrary"),
        name="mixer_in",
    )(*([x] * X_CHUNKS), g, w, dww, dwb, lng, lnb, cw, cb, wg, ba, bx, lam)


SB_T = 128
SB_GROUP = 4
SB_EAGER = 3
SB_PAD = SB_GROUP - 1
SB_LOG_WEIGHT_FLOOR = -104.0


def _sb_split(log_1mb):
    hi = log_1mb.astype(BF16)
    lo = (log_1mb - hi.astype(F32)).astype(BF16)
    return jnp.concatenate([hi, lo], axis=1)


def _sb_attn_kernel(q_ref, k_ref, v_ref, uo_ref, o_ref, kp_ref, vp_ref, acc_ref, carry_ref):
    s_len = q_ref.shape[0]
    scale = HEAD_DIM ** -0.5
    t_sz, n_grp = SB_T, SB_GROUP
    pad_rows = SB_PAD * t_sz
    kp_ref[0:pad_rows, :] = jnp.zeros((pad_rows, HEAD_DIM), BF16)
    vp_ref[0:pad_rows, :] = jnp.zeros((pad_rows, HEAD_DIM), BF16)
    kp_ref[pad_rows:pad_rows + s_len, :] = k_ref[...]
    vp_ref[pad_rows:pad_rows + s_len, :] = v_ref[...]
    causal = (lax.broadcasted_iota(jnp.int32, (t_sz, t_sz), 1)
              < lax.broadcasted_iota(jnp.int32, (t_sz, t_sz), 0))

    def q_tile(i):
        return q_ref[pl.ds(pl.multiple_of(i * t_sz, t_sz), t_sz), :]

    def neg_scores(q, k):
        zn = lax.dot_general(q, k, (((1,), (1,)), ((), ())), preferred_element_type=F32) * (-scale)
        log_1mb = jnp.minimum(zn, 0.0) - jnp.log(1.0 + jnp.exp(-jnp.abs(zn)))
        return zn, log_1mb

    def group(g, _):
        i0 = g * n_grp

        def eager():
            width = SB_EAGER * t_sz
            zns, ls, starts, lhs = [], [], [], []
            for r in range(n_grp):
                start = pl.multiple_of((i0 + r + SB_PAD - (SB_EAGER - 1)) * t_sz, t_sz)
                zn, log_1mb = neg_scores(q_tile(i0 + r), kp_ref[pl.ds(start, width), :])
                pieces = []
                for c in range(SB_EAGER):
                    lc = log_1mb[:, c * t_sz:(c + 1) * t_sz]
                    if c == SB_EAGER - 1:
                        lc = jnp.where(causal, lc, 0.0)
                    pieces.append(_sb_split(lc))
                lhs.append(jnp.concatenate(pieces, axis=0))
                zns.append(zn)
                ls.append(log_1mb)
                starts.append(start)
            sums = [jnp.dot(x, uo_ref[...], preferred_element_type=F32) for x in lhs]
            m = None
            for r in range(n_grp):
                carry = None
                ws = [None] * SB_EAGER
                for c in reversed(range(SB_EAGER)):
                    cols = slice(c * t_sz, (c + 1) * t_sz)
                    sc = sums[r][cols]
                    after = sc[:, :t_sz] if carry is None else carry + sc[:, :t_sz]
                    w = jnp.exp((ls[r][:, cols] - zns[r][:, cols]) + after)
                    if c == SB_EAGER - 1:
                        w = jnp.where(causal, w, 0.0)
                    ws[c] = w.astype(BF16)
                    carry = sc[:, t_sz:] if carry is None else carry + sc[:, t_sz:]
                v = vp_ref[pl.ds(starts[r], width), :]
                acc_ref[r] = jnp.dot(jnp.concatenate(ws, axis=1), v, preferred_element_type=F32)
                carry_ref[r] = carry
                m = carry if m is None else jnp.maximum(m, carry)
            return jnp.max(m)

        def step(d):
            zns, ls, starts, lhs = [], [], [], []
            for r in range(n_grp):
                start = pl.multiple_of((i0 + r + SB_PAD - d) * t_sz, t_sz)
                zn, log_1mb = neg_scores(q_tile(i0 + r), kp_ref[pl.ds(start, t_sz), :])
                lhs.append(_sb_split(log_1mb))
                zns.append(zn)
                ls.append(log_1mb)
                starts.append(start)
            sums = jnp.dot(jnp.concatenate(lhs, axis=0), uo_ref[...], preferred_element_type=F32)
            m = None
            for r in range(n_grp):
                sr = sums[r * t_sz:(r + 1) * t_sz]
                w = jnp.exp((ls[r] - zns[r]) + (carry_ref[r] + sr[:, :t_sz]))
                v = vp_ref[pl.ds(starts[r], t_sz), :]
                acc_ref[r] += jnp.dot(w.astype(BF16), v, preferred_element_type=F32)
                carry = carry_ref[r] + sr[:, t_sz:]
                carry_ref[r] = carry
                m = carry if m is None else jnp.maximum(m, carry)
            return jnp.max(m)

        def more(c):
            d, m = c
            return jnp.logical_and(d <= i0 + n_grp - 1, m > SB_LOG_WEIGHT_FLOOR)

        lax.while_loop(more, lambda c: (c[0] + 1, step(c[0])), (jnp.int32(SB_EAGER), eager()))
        for r in range(n_grp):
            o_ref[pl.ds(pl.multiple_of((i0 + r) * t_sz, t_sz), t_sz), :] = acc_ref[r].astype(o_ref.dtype)
        return 0

    lax.fori_loop(0, s_len // (t_sz * n_grp), group, 0)


def _sb_attention(u3):
    b, s, _ = u3.shape
    dh = HEAD_DIM
    assert SB_EAGER - 1 <= SB_PAD and s % (SB_T * SB_GROUP) == 0
    row = lax.broadcasted_iota(jnp.int32, (2 * SB_T, 2 * SB_T), 0) % SB_T
    col = lax.broadcasted_iota(jnp.int32, (2 * SB_T, 2 * SB_T), 1)
    uo = jnp.where((col >= SB_T) | (row > col), 1.0, 0.0).astype(BF16)

    def head_spec(col0):
        return pl.BlockSpec((None, s, dh), lambda bi, hi: (bi, 0, col0 // dh + hi))

    padded = pltpu.VMEM((s + SB_PAD * SB_T, dh), BF16)
    return pl.pallas_call(
        _sb_attn_kernel,
        out_shape=jax.ShapeDtypeStruct((b, s, ATTN_WIDTH), BF16),
        grid=(b, ATTN_HEADS),
        in_specs=[head_spec(UCOL_Q), head_spec(UCOL_K), head_spec(UCOL_V),
                  pl.BlockSpec((2 * SB_T, 2 * SB_T), lambda bi, hi: (0, 0))],
        out_specs=pl.BlockSpec((None, s, dh), lambda bi, hi: (bi, 0, hi)),
        scratch_shapes=[padded, padded,
                        pltpu.VMEM((SB_GROUP, SB_T, dh), F32), pltpu.VMEM((SB_GROUP, SB_T, SB_T), F32)],
        compiler_params=_params("parallel", "parallel"),
        name="sb_attention",
    )(u3, u3, u3, uo)


def _out_proj_kernel(x_ref, yc_ref, ya_ref, yl_ref, gc_ref, ga0_ref, ga1_ref, gl_ref,
                     pw_ref, nc_ref, na_ref, nl_ref, w_ref, o_ref):
    def normed(y, n_ref):
        return y * _rms_scale(y) * n_ref[...]

    def gated(yn, gate_ref):
        return (yn * _silu(gate_ref[...].astype(F32))).astype(BF16)

    half = ATTN_WIDTH // 2
    y_conv = jnp.dot(yc_ref[...], pw_ref[...], preferred_element_type=F32)
    ya = normed(ya_ref[...].astype(F32), na_ref)
    parts = [
        (gated(normed(y_conv, nc_ref), gc_ref), 0),
        (gated(ya[:, :half], ga0_ref), CONV_WIDTH),
        (gated(ya[:, half:], ga1_ref), CONV_WIDTH + half),
        (gated(normed(yl_ref[...].astype(F32), nl_ref), gl_ref), CONV_WIDTH + ATTN_WIDTH),
    ]
    acc = x_ref[...]
    for y, r0 in parts:
        acc = acc + jnp.dot(y, w_ref[r0:r0 + y.shape[1], :], preferred_element_type=F32)
    o_ref[...] = acc


def _out_proj(x, yc, ya, yl, u, pw, nc, na, nl, w, layer):
    t, d = x.shape
    tm = TM_OUT_PROJ
    gw = COL_BLOCK
    return pl.pallas_call(
        _out_proj_kernel,
        out_shape=jax.ShapeDtypeStruct((t, d), F32),
        grid=(t // tm,),
        in_specs=[
            pl.BlockSpec((tm, d), lambda i: (i, 0)),
            pl.BlockSpec((tm, CONV_WIDTH), lambda i: (i, 0)),
            pl.BlockSpec((tm, ATTN_WIDTH), lambda i: (i, 0)),
            pl.BlockSpec((tm, LRU_WIDTH), lambda i: (i, 0)),
            pl.BlockSpec((tm, gw), lambda i: (i, UCOL_CGATE // gw)),
            pl.BlockSpec((tm, gw), lambda i: (i, UCOL_AGATE // gw)),
            pl.BlockSpec((tm, gw), lambda i: (i, UCOL_AGATE // gw + 1)),
            pl.BlockSpec((tm, gw), lambda i: (i, UCOL_RGATE // gw)),
            pl.BlockSpec((CONV_WIDTH, CONV_WIDTH), lambda i: (0, 0)),
            pl.BlockSpec((1, CONV_WIDTH), lambda i: (0, 0)),
            pl.BlockSpec((1, ATTN_WIDTH), lambda i: (0, 0)),
            pl.BlockSpec((1, LRU_WIDTH), lambda i: (0, 0)),
            pl.BlockSpec((None, d, d), lambda i: (layer, 0, 0)),
        ],
        out_specs=pl.BlockSpec((tm, d), lambda i: (i, 0)),
        compiler_params=_params("parallel"),
        name="out_proj",
    )(x, yc, ya, yl, u, u, u, u, pw, nc, na, nl, w)


def _xattn_kernel(x_ref, g_ref, wq_ref, k_ref, v_ref, wo_ref, fg_ref, o_ref, *, final_norm):
    x = x_ref[...]
    h = (x * _rms_scale(x) * g_ref[...]).astype(BF16)
    q = jnp.dot(h, wq_ref[...], preferred_element_type=F32).astype(BF16)
    dh = XATTN_WIDTH // XATTN_HEADS
    scale = dh ** -0.5
    heads = []
    for n in range(XATTN_HEADS):
        qh = q[:, n * dh:(n + 1) * dh]
        kh = k_ref[:, n * dh:(n + 1) * dh]
        vh = v_ref[:, n * dh:(n + 1) * dh]
        s = lax.dot_general(qh, kh, (((1,), (1,)), ((), ())), preferred_element_type=F32) * scale
        e = jnp.exp(s - jnp.max(s, axis=-1, keepdims=True))
        p = (e / jnp.sum(e, axis=-1, keepdims=True)).astype(BF16)
        heads.append(jnp.dot(p, vh, preferred_element_type=F32).astype(BF16))
    acc = x + jnp.dot(jnp.concatenate(heads, axis=-1), wo_ref[...], preferred_element_type=F32)
    if final_norm:
        acc = acc * _rms_scale(acc) * fg_ref[...]
    o_ref[...] = acc


def _xattn(x3, g, wq, kv, wo, fg, layer, *, final_norm):
    b, s, d = x3.shape
    m = kv.shape[1]
    tm = TM_XATTN
    xw = XATTN_WIDTH
    return pl.pallas_call(
        functools.partial(_xattn_kernel, final_norm=final_norm),
        out_shape=jax.ShapeDtypeStruct((b, s, d), F32),
        grid=(b, s // tm),
        in_specs=[
            pl.BlockSpec((None, tm, d), lambda bi, ti: (bi, ti, 0)),
            pl.BlockSpec((1, d), lambda bi, ti: (0, 0)),
            pl.BlockSpec((None, d, xw), lambda bi, ti: (layer, 0, 0)),
            pl.BlockSpec((None, m, xw), lambda bi, ti: (bi, 0, 0)),
            pl.BlockSpec((None, m, xw), lambda bi, ti: (bi, 0, 1)),
            pl.BlockSpec((None, xw, d), lambda bi, ti: (layer, 0, 0)),
            pl.BlockSpec((1, d), lambda bi, ti: (0, 0)),
        ],
        out_specs=pl.BlockSpec((None, tm, d), lambda bi, ti: (bi, ti, 0)),
        compiler_params=_params("parallel", "parallel"),
        name="xattn_final" if final_norm else "xattn",
    )(x3, g, wq, kv, kv, wo, fg)


def kernel(x, mem, mix_norm_g, w_in, conv_dw_w, conv_dw_b, conv_ln_g, conv_ln_b, conv_pw_w,
           lru_conv_w, lru_conv_b, lru_wa, lru_ba, lru_wx, lru_bx, lru_lambda,
           out_norm_conv, out_norm_attn, out_norm_lru, w_out,
           xattn_norm_g, mem_norm_g, xattn_wq, xattn_wkv, xattn_wo, final_norm_g):
    b, s, d = x.shape
    m = mem.shape[1]
    depth = w_in.shape[0]
    t = b * s
    row = lambda a: a.reshape(1, -1).astype(F32)

    xt = x.reshape(t, d)
    memt = mem.reshape(b * m, d)
    w_in_b, w_out_b = w_in.astype(BF16), w_out.astype(BF16)
    wq_b, wkv_b, wo_b = xattn_wq.astype(BF16), xattn_wkv.astype(BF16), xattn_wo.astype(BF16)
    for l in range(depth):
        conv_params = (conv_dw_w[l], row(conv_dw_b[l]), row(conv_ln_g[l]), row(conv_ln_b[l]))
        lru_gates = jnp.concatenate([lru_wa[l], lru_wx[l]], axis=-1).astype(BF16)
        lru_params = (lru_conv_w[l], row(lru_conv_b[l]), lru_gates, row(lru_ba[l]), row(lru_bx[l]),
                      row(lru_lambda[l]))
        u, conv_act, y_lru = _mixer_in(xt, row(mix_norm_g[l]), w_in_b, l, conv_params, lru_params, seq_len=s)
        y_attn = _sb_attention(u.reshape(b, s, U_WIDTH))
        xt = _out_proj(xt, conv_act, y_attn.reshape(t, -1), y_lru, u, conv_pw_w[l].astype(BF16),
                       row(out_norm_conv[l]), row(out_norm_attn[l]), row(out_norm_lru[l]), w_out_b, l)
        kv = _norm_matmul(memt, row(mem_norm_g[l]), wkv_b, l, tm=256, tn=1024,
                          name="mem_kv").reshape(b, m, 2 * XATTN_WIDTH)
        xt = _xattn(xt.reshape(b, s, d), row(xattn_norm_g[l]), wq_b, kv, wo_b, row(final_norm_g), l,
                    final_norm=(l == depth - 1)).reshape(t, d)
    return xt.reshape(b, s, d)
```

```python
import functools

import jax
import jax.numpy as jnp
from jax import lax
from jax.experimental import pallas as pl
from jax.experimental.pallas import tpu as pltpu

F32 = jnp.float32
BF16 = jnp.bfloat16

D_MODEL = 2048
CONV_WIDTH = 512
CONV_KERNEL = 31
HEAD_DIM = 128
ATTN_WIDTH = 1024
ATTN_HEADS = 8
LRU_WIDTH = 512
LRU_HEADS = 4
LRU_CONV_KERNEL = 4
LRU_C = 8.0
XATTN_HEADS = 4
XATTN_WIDTH = 512
IN_WIDTH = 3 * CONV_WIDTH + 4 * ATTN_WIDTH + 2 * LRU_WIDTH

COL_BLOCK = 512
N_COL_BLOCKS = IN_WIDTH // COL_BLOCK
W_BLOCK_LRU_X = 11
N_BRANCH_BLOCKS = 3
U_WIDTH = IN_WIDTH - N_BRANCH_BLOCKS * COL_BLOCK
UCOL_CGATE, UCOL_Q, UCOL_K, UCOL_V, UCOL_AGATE, UCOL_RGATE = 0, 512, 1536, 2560, 3584, 4608

VMEM_LIMIT_BYTES = 56 * 1024 * 1024
SUBLANES = 8
LANES = 128

TM_MIXER_IN = 1024
TM_OUT_PROJ = 512
TM_XATTN = 1024

RMS_EPS = 1e-6
LN_EPS = 1e-5


def _params(*sem):
    return pltpu.CompilerParams(dimension_semantics=sem, vmem_limit_bytes=VMEM_LIMIT_BYTES)


def _rms_scale(x):
    return lax.rsqrt(jnp.mean(x * x, axis=-1, keepdims=True) + RMS_EPS)


def _softplus(x):
    return jnp.maximum(x, 0.0) + jnp.log(1.0 + jnp.exp(-jnp.abs(x)))


def _silu(x):
    return x * jax.nn.sigmoid(x)


def _norm_matmul_kernel(x_ref, g_ref, w_ref, o_ref, h_ref):
    @pl.when(pl.program_id(1) == 0)
    def _():
        x = x_ref[...]
        h_ref[...] = (x * _rms_scale(x) * g_ref[...]).astype(BF16)

    o_ref[...] = jnp.dot(h_ref[...], w_ref[...], preferred_element_type=F32).astype(o_ref.dtype)


def _norm_matmul(x, g, w, layer, *, tm, tn, name):
    t, d = x.shape
    n = w.shape[2]
    return pl.pallas_call(
        _norm_matmul_kernel,
        out_shape=jax.ShapeDtypeStruct((t, n), BF16),
        grid=(t // tm, n // tn),
        in_specs=[
            pl.BlockSpec((tm, d), lambda i, j: (i, 0)),
            pl.BlockSpec((1, d), lambda i, j: (0, 0)),
            pl.BlockSpec((None, d, tn), lambda i, j: (layer, 0, j)),
        ],
        out_specs=pl.BlockSpec((tm, tn), lambda i, j: (i, j)),
        scratch_shapes=[pltpu.VMEM((tm, d), BF16)],
        compiler_params=_params("parallel", "arbitrary"),
        name=name,
    )(x, g, w)


CONV_HALO = 32
CONV_ROW_CHUNK = 64
BRANCH_STEP_ROWS = 128
LRU_HALO = SUBLANES


def _depthwise_conv_rows(buf_ref, dww_ref, dwb_ref, r0):
    base = CONV_HALO - (CONV_KERNEL - 1)
    rows = CONV_ROW_CHUNK + SUBLANES
    n_m = (base + CONV_KERNEL - 1) // SUBLANES + 1
    out = []
    for lb in range(CONV_WIDTH // LANES):
        lanes = slice(lb * LANES, (lb + 1) * LANES)
        window = buf_ref[pl.ds(r0, rows + SUBLANES * (n_m - 1)), lanes]
        a = None
        for rho in reversed(range(SUBLANES)):
            q = None
            for m in range(n_m):
                k = SUBLANES * m + rho - base
                if 0 <= k < CONV_KERNEL:
                    term = dww_ref[k:k + 1, lanes] * window[SUBLANES * m:SUBLANES * m + rows]
                    q = term if q is None else q + term
            a = q if a is None else q + pltpu.roll(a, rows - 1, 0)
        out.append(a[:CONV_ROW_CHUNK] + dwb_ref[:, lanes])
    return jnp.concatenate(out, axis=1)


def _conv_branch_rows(cbuf_ref, dww_ref, dwb_ref, lng_ref, lnb_ref, r0):
    u = jnp.concatenate([_depthwise_conv_rows(cbuf_ref, dww_ref, dwb_ref, r0 + c * CONV_ROW_CHUNK)
                         for c in range(BRANCH_STEP_ROWS // CONV_ROW_CHUNK)], axis=0)
    mu = jnp.mean(u, axis=-1, keepdims=True)
    uc = u - mu
    var = jnp.mean(uc * uc, axis=-1, keepdims=True)
    y = uc * lax.rsqrt(var + LN_EPS) * lng_ref[...] + lnb_ref[...]
    return _silu(y).astype(BF16)


def _lru_branch_rows(rbuf_ref, cw_ref, cb_ref, wg_ref, ba_ref, bx_ref, lam_ref, hstate_ref, a_ref, b_ref,
                     yl_ref, r0):
    w = LRU_WIDTH
    n = BRANCH_STEP_ROWS
    window = rbuf_ref[pl.ds(r0, n + LRU_HALO), :]
    xc = jnp.broadcast_to(cb_ref[...], (n, w))
    for k in range(LRU_CONV_KERNEL):
        shift = LRU_HALO - (LRU_CONV_KERNEL - 1) + k
        if shift % SUBLANES == 0:
            tap = window[shift:shift + n]
        else:
            tap = pltpu.roll(window, n + LRU_HALO - shift, 0)[:n]
        xc = xc + cw_ref[k:k + 1, :] * tap

    xcb = xc.astype(BF16)
    hd = w // LRU_HEADS
    pre = [jnp.dot(xcb[:, hix * hd:(hix + 1) * hd], wg_ref[hix], preferred_element_type=F32)
           for hix in range(LRU_HEADS)]
    r = jax.nn.sigmoid(jnp.concatenate([p[:, :hd] for p in pre], axis=-1) + ba_ref[...])
    gate = jax.nn.sigmoid(jnp.concatenate([p[:, hd:] for p in pre], axis=-1) + bx_ref[...])
    log_a = (-LRU_C) * r * _softplus(-lam_ref[...])
    a = jnp.exp(log_a)
    a_ref[...] = a
    b_ref[...] = jnp.sqrt(jnp.tanh(-log_a) * (1.0 + a * a)) * (gate * xc)

    row = lax.broadcasted_iota(jnp.int32, (SUBLANES, w), 0)
    h_prev = hstate_ref[...]
    for g in range(n // SUBLANES):
        rows = slice(g * SUBLANES, (g + 1) * SUBLANES)
        ag = a_ref[rows, :]
        bg = b_ref[rows, :]
        d = 1
        while d < SUBLANES:
            a_sh = pltpu.roll(ag, d, 0)
            b_sh = pltpu.roll(bg, d, 0)
            m = row >= d
            bg = jnp.where(m, ag * b_sh + bg, bg)
            ag = jnp.where(m, ag * a_sh, ag)
            d *= 2
        h = ag * h_prev + bg
        b_ref[rows, :] = h
        h_prev = jnp.broadcast_to(h[SUBLANES - 1:SUBLANES, :], (SUBLANES, w))
    hstate_ref[...] = h_prev
    yl_ref[pl.ds(r0, n), :] = b_ref[...].astype(yl_ref.dtype)


def _mixer_in_kernel(x0_ref, x1_ref, x2_ref, x3_ref, g_ref, w_ref,
                     dww_ref, dwb_ref, lng_ref, lnb_ref,
                     cw_ref, cb_ref, wg_ref, ba_ref, bx_ref, lam_ref,
                     u_ref, yc_ref, yl_ref,
                     h_ref, val_ref, cbuf_ref, rbuf_ref, hstate_ref, a_ref, b_ref, *, tiles_per_seq):
    i = pl.program_id(0)
    j = pl.program_id(1)
    x_refs = (x0_ref, x1_ref, x2_ref, x3_ref)
    tm = x0_ref.shape[0]
    first_branch_step = N_BRANCH_BLOCKS
    n_branch_steps = tm // BRANCH_STEP_ROWS

    def project():
        return jnp.dot(h_ref[...], w_ref[...], preferred_element_type=F32)

    @pl.when(j == 0)
    def _():
        @pl.when(i % tiles_per_seq == 0)
        def _():
            cbuf_ref[0:CONV_HALO, :] = jnp.zeros((CONV_HALO, CONV_WIDTH), F32)
            rbuf_ref[0:LRU_HALO, :] = jnp.zeros((LRU_HALO, LRU_WIDTH), F32)
            hstate_ref[...] = jnp.zeros_like(hstate_ref)
            cbuf_ref[CONV_HALO + tm:CONV_HALO + tm + SUBLANES, :] = jnp.zeros((SUBLANES, CONV_WIDTH), F32)

        @pl.when(i % tiles_per_seq != 0)
        def _():
            cbuf_ref[0:CONV_HALO, :] = cbuf_ref[tm:tm + CONV_HALO, :]
            rbuf_ref[0:LRU_HALO, :] = rbuf_ref[tm:tm + LRU_HALO, :]

        d = h_ref.shape[1]
        cw = d // len(x_refs)
        sq = None
        for xr in x_refs:
            xc = xr[...]
            part = jnp.sum(xc * xc, axis=-1, keepdims=True)
            sq = part if sq is None else sq + part
        inv = lax.rsqrt(sq * (1.0 / d) + RMS_EPS)
        for c, xr in enumerate(x_refs):
            h_ref[:, c * cw:(c + 1) * cw] = (xr[...] * inv * g_ref[:, c * cw:(c + 1) * cw]).astype(BF16)
        val_ref[...] = project()

    @pl.when(j == 1)
    def _():
        cbuf_ref[CONV_HALO:CONV_HALO + tm, :] = val_ref[...] * jax.nn.sigmoid(project())

    @pl.when(j == 2)
    def _():
        rbuf_ref[LRU_HALO:LRU_HALO + tm, :] = project()

    @pl.when(jnp.logical_and(j >= first_branch_step, j < first_branch_step + n_branch_steps))
    def _():
        r0 = pl.multiple_of((j - first_branch_step) * BRANCH_STEP_ROWS, BRANCH_STEP_ROWS)
        _lru_branch_rows(rbuf_ref, cw_ref, cb_ref, wg_ref, ba_ref, bx_ref, lam_ref, hstate_ref, a_ref, b_ref,
                         yl_ref, r0)
        yc_ref[pl.ds(r0, BRANCH_STEP_ROWS), :] = _conv_branch_rows(cbuf_ref, dww_ref, dwb_ref, lng_ref, lnb_ref, r0)
        u_ref[...] = project().astype(u_ref.dtype)

    @pl.when(j >= first_branch_step + n_branch_steps)
    def _():
        u_ref[...] = project().astype(u_ref.dtype)


X_CHUNKS = 4


def _mixer_in(x, g, w, layer, conv_params, lru_params, *, seq_len):
    t, d = x.shape
    tm = TM_MIXER_IN
    n_tiles = t // tm
    assert seq_len % tm == 0 and tm % BRANCH_STEP_ROWS == 0 and BRANCH_STEP_ROWS % CONV_ROW_CHUNK == 0
    assert N_BRANCH_BLOCKS + tm // BRANCH_STEP_ROWS <= N_COL_BLOCKS and d % X_CHUNKS == 0
    dww, dwb, lng, lnb = conv_params
    cw, cb, wg, ba, bx, lam = lru_params
    hd = LRU_WIDTH // LRU_HEADS

    def w_block(i, j):
        blk = jnp.where(j == 2, W_BLOCK_LRU_X, jnp.where(jnp.logical_and(j > 2, j <= W_BLOCK_LRU_X), j - 1, j))
        return (layer, 0, blk)

    def x_chunk(c):
        def index(i, j):
            ahead = (j >= N_COL_BLOCKS - X_CHUNKS + c).astype(jnp.int32)
            return (jnp.minimum(i + ahead, n_tiles - 1), c)
        return pl.BlockSpec((tm, d // X_CHUNKS), index)

    const = lambda shape: pl.BlockSpec(shape, lambda i, j: (0,) * len(shape))
    return pl.pallas_call(
        functools.partial(_mixer_in_kernel, tiles_per_seq=seq_len // tm),
        out_shape=(jax.ShapeDtypeStruct((t, U_WIDTH), BF16),
                   jax.ShapeDtypeStruct((t, CONV_WIDTH), BF16),
                   jax.ShapeDtypeStruct((t, LRU_WIDTH), BF16)),
        grid=(n_tiles, N_COL_BLOCKS),
        in_specs=[
            *[x_chunk(c) for c in range(X_CHUNKS)],
            const((1, d)),
            pl.BlockSpec((None, d, COL_BLOCK), w_block),
            const((CONV_KERNEL, CONV_WIDTH)), const((1, CONV_WIDTH)), const((1, CONV_WIDTH)),
            const((1, CONV_WIDTH)),
            const((LRU_CONV_KERNEL, LRU_WIDTH)), const((1, LRU_WIDTH)),
            const((LRU_HEADS, hd, 2 * hd)), const((1, LRU_WIDTH)), const((1, LRU_WIDTH)), const((1, LRU_WIDTH)),
        ],
        out_specs=(
            pl.BlockSpec((tm, COL_BLOCK), lambda i, j: (i, jnp.maximum(j - N_BRANCH_BLOCKS, 0))),
            pl.BlockSpec((tm, CONV_WIDTH), lambda i, j: (i, 0)),
            pl.BlockSpec((tm, LRU_WIDTH), lambda i, j: (i, 0)),
        ),
        scratch_shapes=[
            pltpu.VMEM((tm, d), BF16),
            pltpu.VMEM((tm, CONV_WIDTH), F32),
            pltpu.VMEM((CONV_HALO + tm + SUBLANES, CONV_WIDTH), F32),
            pltpu.VMEM((LRU_HALO + tm, LRU_WIDTH), F32),
            pltpu.VMEM((SUBLANES, LRU_WIDTH), F32),
            pltpu.VMEM((BRANCH_STEP_ROWS, LRU_WIDTH), F32),
            pltpu.VMEM((BRANCH_STEP_ROWS, LRU_WIDTH), F32),
        ],
        compiler_params=_params("arbitrary", "arbitrary"),
        name="mixer_in",
    )(*([x] * X_CHUNKS), g, w, dww, dwb, lng, lnb, cw, cb, wg, ba, bx, lam)


SB_T = 128
SB_GROUP = 8
SB_EAGER = 3
SB_PAD = SB_GROUP - 1
SB_LOG_WEIGHT_FLOOR = -104.0


def _sb_split(log_1mb):
    hi = log_1mb.astype(BF16)
    lo = (log_1mb - hi.astype(F32)).astype(BF16)
    return jnp.concatenate([hi, lo], axis=1)


def _sb_attn_kernel(q_ref, k_ref, v_ref, uo_ref, o_ref, kp_ref, vp_ref, acc_ref, carry_ref):
    s_len = q_ref.shape[0]
    scale = HEAD_DIM ** -0.5
    t_sz, n_grp = SB_T, SB_GROUP
    pad_rows = SB_PAD * t_sz
    kp_ref[0:pad_rows, :] = jnp.zeros((pad_rows, HEAD_DIM), BF16)
    vp_ref[0:pad_rows, :] = jnp.zeros((pad_rows, HEAD_DIM), BF16)
    kp_ref[pad_rows:pad_rows + s_len, :] = k_ref[...]
    vp_ref[pad_rows:pad_rows + s_len, :] = v_ref[...]
    causal = (lax.broadcasted_iota(jnp.int32, (t_sz, t_sz), 1)
              < lax.broadcasted_iota(jnp.int32, (t_sz, t_sz), 0))

    def q_tile(i):
        return q_ref[pl.ds(pl.multiple_of(i * t_sz, t_sz), t_sz), :]

    def neg_scores(q, k):
        zn = lax.dot_general(q, k, (((1,), (1,)), ((), ())), preferred_element_type=F32) * (-scale)
        log_1mb = jnp.minimum(zn, 0.0) - jnp.log(1.0 + jnp.exp(-jnp.abs(zn)))
        return zn, log_1mb

    def group(g, _):
        i0 = g * n_grp

        def eager():
            width = SB_EAGER * t_sz
            zns, ls, starts, lhs = [], [], [], []
            for r in range(n_grp):
                start = pl.multiple_of((i0 + r + SB_PAD - (SB_EAGER - 1)) * t_sz, t_sz)
                zn, log_1mb = neg_scores(q_tile(i0 + r), kp_ref[pl.ds(start, width), :])
                pieces = []
                for c in range(SB_EAGER):
                    lc = log_1mb[:, c * t_sz:(c + 1) * t_sz]
                    if c == SB_EAGER - 1:
                        lc = jnp.where(causal, lc, 0.0)
                    pieces.append(_sb_split(lc))
                lhs.append(jnp.concatenate(pieces, axis=0))
                zns.append(zn)
                ls.append(log_1mb)
                starts.append(start)
            sums = [jnp.dot(x, uo_ref[...], preferred_element_type=F32) for x in lhs]
            m = None
            for r in range(n_grp):
                carry = None
                ws = [None] * SB_EAGER
                for c in reversed(range(SB_EAGER)):
                    cols = slice(c * t_sz, (c + 1) * t_sz)
                    sc = sums[r][cols]
                    after = sc[:, :t_sz] if carry is None else carry + sc[:, :t_sz]
                    w = jnp.exp((ls[r][:, cols] - zns[r][:, cols]) + after)
                    if c == SB_EAGER - 1:
                        w = jnp.where(causal, w, 0.0)
                    ws[c] = w.astype(BF16)
                    carry = sc[:, t_sz:] if carry is None else carry + sc[:, t_sz:]
                v = vp_ref[pl.ds(starts[r], width), :]
                acc_ref[r] = jnp.dot(jnp.concatenate(ws, axis=1), v, preferred_element_type=F32)
                carry_ref[r] = carry
                m = carry if m is None else jnp.maximum(m, carry)
            return jnp.max(m)

        def step(d):
            zns, ls, starts, lhs = [], [], [], []
            for r in range(n_grp):
                start = pl.multiple_of((i0 + r + SB_PAD - d) * t_sz, t_sz)
                zn, log_1mb = neg_scores(q_tile(i0 + r), kp_ref[pl.ds(start, t_sz), :])
                lhs.append(_sb_split(log_1mb))
                zns.append(zn)
                ls.append(log_1mb)
                starts.append(start)
            sums = jnp.dot(jnp.concatenate(lhs, axis=0), uo_ref[...], preferred_element_type=F32)
            m = None
            for r in range(n_grp):
                sr = sums[r * t_sz:(r + 1) * t_sz]
                w = jnp.exp((ls[r] - zns[r]) + (carry_ref[r] + sr[:, :t_sz]))
                v = vp_ref[pl.ds(starts[r], t_sz), :]
                acc_ref[r] += jnp.dot(w.astype(BF16), v, preferred_element_type=F32)
                carry = carry_ref[r] + sr[:, t_sz:]
                carry_ref[r] = carry
                m = carry if m is None else jnp.maximum(m, carry)
            return jnp.max(m)

        def more(c):
            d, m = c
            return jnp.logical_and(d <= i0 + n_grp - 1, m > SB_LOG_WEIGHT_FLOOR)

        lax.while_loop(more, lambda c: (c[0] + 1, step(c[0])), (jnp.int32(SB_EAGER), eager()))
        for r in range(n_grp):
            o_ref[pl.ds(pl.multiple_of((i0 + r) * t_sz, t_sz), t_sz), :] = acc_ref[r].astype(o_ref.dtype)
        return 0

    lax.fori_loop(0, s_len // (t_sz * n_grp), group, 0)


def _sb_attention(u3):
    b, s, _ = u3.shape
    dh = HEAD_DIM
    assert SB_EAGER - 1 <= SB_PAD and s % (SB_T * SB_GROUP) == 0
    row = lax.broadcasted_iota(jnp.int32, (2 * SB_T, 2 * SB_T), 0) % SB_T
    col = lax.broadcasted_iota(jnp.int32, (2 * SB_T, 2 * SB_T), 1)
    uo = jnp.where((col >= SB_T) | (row > col), 1.0, 0.0).astype(BF16)

    def head_spec(col0):
        return pl.BlockSpec((None, s, dh), lambda bi, hi: (bi, 0, col0 // dh + hi))

    padded = pltpu.VMEM((s + SB_PAD * SB_T, dh), BF16)
    return pl.pallas_call(
        _sb_attn_kernel,
        out_shape=jax.ShapeDtypeStruct((b, s, ATTN_WIDTH), BF16),
        grid=(b, ATTN_HEADS),
        in_specs=[head_spec(UCOL_Q), head_spec(UCOL_K), head_spec(UCOL_V),
                  pl.BlockSpec((2 * SB_T, 2 * SB_T), lambda bi, hi: (0, 0))],
        out_specs=pl.BlockSpec((None, s, dh), lambda bi, hi: (bi, 0, hi)),
        scratch_shapes=[padded, padded,
                        pltpu.VMEM((SB_GROUP, SB_T, dh), F32), pltpu.VMEM((SB_GROUP, SB_T, SB_T), F32)],
        compiler_params=_params("parallel", "parallel"),
        name="sb_attention",
    )(u3, u3, u3, uo)


def _out_proj_kernel(x_ref, yc_ref, ya_ref, yl_ref, gc_ref, ga0_ref, ga1_ref, gl_ref,
                     pw_ref, nc_ref, na_ref, nl_ref, w_ref, o_ref):
    def normed(y, n_ref):
        return y * _rms_scale(y) * n_ref[...]

    def gated(yn, gate_ref):
        return (yn * _silu(gate_ref[...].astype(F32))).astype(BF16)

    half = ATTN_WIDTH // 2
    y_conv = jnp.dot(yc_ref[...], pw_ref[...], preferred_element_type=F32)
    ya = normed(ya_ref[...].astype(F32), na_ref)
    parts = [
        (gated(normed(y_conv, nc_ref), gc_ref), 0),
        (gated(ya[:, :half], ga0_ref), CONV_WIDTH),
        (gated(ya[:, half:], ga1_ref), CONV_WIDTH + half),
        (gated(normed(yl_ref[...].astype(F32), nl_ref), gl_ref), CONV_WIDTH + ATTN_WIDTH),
    ]
    acc = x_ref[...]
    for y, r0 in parts:
        acc = acc + jnp.dot(y, w_ref[r0:r0 + y.shape[1], :], preferred_element_type=F32)
    o_ref[...] = acc


def _out_proj(x, yc, ya, yl, u, pw, nc, na, nl, w, layer):
    t, d = x.shape
    tm = TM_OUT_PROJ
    gw = COL_BLOCK
    return pl.pallas_call(
        _out_proj_kernel,
        out_shape=jax.ShapeDtypeStruct((t, d), F32),
        grid=(t // tm,),
        in_specs=[
            pl.BlockSpec((tm, d), lambda i: (i, 0)),
            pl.BlockSpec((tm, CONV_WIDTH), lambda i: (i, 0)),
            pl.BlockSpec((tm, ATTN_WIDTH), lambda i: (i, 0)),
            pl.BlockSpec((tm, LRU_WIDTH), lambda i: (i, 0)),
            pl.BlockSpec((tm, gw), lambda i: (i, UCOL_CGATE // gw)),
            pl.BlockSpec((tm, gw), lambda i: (i, UCOL_AGATE // gw)),
            pl.BlockSpec((tm, gw), lambda i: (i, UCOL_AGATE // gw + 1)),
            pl.BlockSpec((tm, gw), lambda i: (i, UCOL_RGATE // gw)),
            pl.BlockSpec((CONV_WIDTH, CONV_WIDTH), lambda i: (0, 0)),
            pl.BlockSpec((1, CONV_WIDTH), lambda i: (0, 0)),
            pl.BlockSpec((1, ATTN_WIDTH), lambda i: (0, 0)),
            pl.BlockSpec((1, LRU_WIDTH), lambda i: (0, 0)),
            pl.BlockSpec((None, d, d), lambda i: (layer, 0, 0)),
        ],
        out_specs=pl.BlockSpec((tm, d), lambda i: (i, 0)),
        compiler_params=_params("parallel"),
        name="out_proj",
    )(x, yc, ya, yl, u, u, u, u, pw, nc, na, nl, w)


def _xattn_kernel(x_ref, g_ref, wq_ref, k_ref, v_ref, wo_ref, fg_ref, o_ref, *, final_norm):
    x = x_ref[...]
    h = (x * _rms_scale(x) * g_ref[...]).astype(BF16)
    q = jnp.dot(h, wq_ref[...], preferred_element_type=F32).astype(BF16)
    dh = XATTN_WIDTH // XATTN_HEADS
    scale = dh ** -0.5
    heads = []
    for n in range(XATTN_HEADS):
        qh = q[:, n * dh:(n + 1) * dh]
        kh = k_ref[:, n * dh:(n + 1) * dh]
        vh = v_ref[:, n * dh:(n + 1) * dh]
        s = lax.dot_general(qh, kh, (((1,), (1,)), ((), ())), preferred_element_type=F32) * scale
        e = jnp.exp(s - jnp.max(s, axis=-1, keepdims=True))
        p = (e / jnp.sum(e, axis=-1, keepdims=True)).astype(BF16)
        heads.append(jnp.dot(p, vh, preferred_element_type=F32).astype(BF16))
    acc = x + jnp.dot(jnp.concatenate(heads, axis=-1), wo_ref[...], preferred_element_type=F32)
    if final_norm:
        acc = acc * _rms_scale(acc) * fg_ref[...]
    o_ref[...] = acc


def _xattn(x3, g, wq, kv, wo, fg, layer, *, final_norm):
    b, s, d = x3.shape
    m = kv.shape[1]
    tm = TM_XATTN
    xw = XATTN_WIDTH
    return pl.pallas_call(
        functools.partial(_xattn_kernel, final_norm=final_norm),
        out_shape=jax.ShapeDtypeStruct((b, s, d), F32),
        grid=(b, s // tm),
        in_specs=[
            pl.BlockSpec((None, tm, d), lambda bi, ti: (bi, ti, 0)),
            pl.BlockSpec((1, d), lambda bi, ti: (0, 0)),
            pl.BlockSpec((None, d, xw), lambda bi, ti: (layer, 0, 0)),
            pl.BlockSpec((None, m, xw), lambda bi, ti: (bi, 0, 0)),
            pl.BlockSpec((None, m, xw), lambda bi, ti: (bi, 0, 1)),
            pl.BlockSpec((None, xw, d), lambda bi, ti: (layer, 0, 0)),
            pl.BlockSpec((1, d), lambda bi, ti: (0, 0)),
        ],
        out_specs=pl.BlockSpec((None, tm, d), lambda bi, ti: (bi, ti, 0)),
        compiler_params=_params("parallel", "parallel"),
        name="xattn_final" if final_norm else "xattn",
    )(x3, g, wq, kv, kv, wo, fg)


def kernel(x, mem, mix_norm_g, w_in, conv_dw_w, conv_dw_b, conv_ln_g, conv_ln_b, conv_pw_w,
           lru_conv_w, lru_conv_b, lru_wa, lru_ba, lru_wx, lru_bx, lru_lambda,
           out_norm_conv, out_norm_attn, out_norm_lru, w_out,
           xattn_norm_g, mem_norm_g, xattn_wq, xattn_wkv, xattn_wo, final_norm_g):
    b, s, d = x.shape
    m = mem.shape[1]
    depth = w_in.shape[0]
    t = b * s
    row = lambda a: a.reshape(1, -1).astype(F32)

    xt = x.reshape(t, d)
    memt = mem.reshape(b * m, d)
    w_in_b, w_out_b = w_in.astype(BF16), w_out.astype(BF16)
    wq_b, wkv_b, wo_b = xattn_wq.astype(BF16), xattn_wkv.astype(BF16), xattn_wo.astype(BF16)
    for l in range(depth):
        conv_params = (conv_dw_w[l], row(conv_dw_b[l]), row(conv_ln_g[l]), row(conv_ln_b[l]))
        lru_gates = jnp.concatenate([lru_wa[l], lru_wx[l]], axis=-1).astype(BF16)
        lru_params = (lru_conv_w[l], row(lru_conv_b[l]), lru_gates, row(lru_ba[l]), row(lru_bx[l]),
                      row(lru_lambda[l]))
        u, conv_act, y_lru = _mixer_in(xt, row(mix_norm_g[l]), w_in_b, l, conv_params, lru_params, seq_len=s)
        y_attn = _sb_attention(u.reshape(b, s, U_WIDTH))
        xt = _out_proj(xt, conv_act, y_attn.reshape(t, -1), y_lru, u, conv_pw_w[l].astype(BF16),
                       row(out_norm_conv[l]), row(out_norm_attn[l]), row(out_norm_lru[l]), w_out_b, l)
        kv = _norm_matmul(memt, row(mem_norm_g[l]), wkv_b, l, tm=256, tn=1024,
                          name="mem_kv").reshape(b, m, 2 * XATTN_WIDTH)
        xt = _xattn(xt.reshape(b, s, d), row(xattn_norm_g[l]), wq_b, kv, wo_b, row(final_norm_g), l,
                    final_norm=(l == depth - 1)).reshape(t, d)
    return xt.reshape(b, s, d)
```

```python
import functools

import jax
import jax.numpy as jnp
from jax import lax
from jax.experimental import pallas as pl
from jax.experimental.pallas import tpu as pltpu

F32 = jnp.float32
BF16 = jnp.bfloat16

D_MODEL = 2048
CONV_WIDTH = 512
CONV_KERNEL = 31
HEAD_DIM = 128
ATTN_WIDTH = 1024
ATTN_HEADS = 8
LRU_WIDTH = 512
LRU_HEADS = 4
LRU_CONV_KERNEL = 4
LRU_C = 8.0
XATTN_HEADS = 4
XATTN_WIDTH = 512
IN_WIDTH = 3 * CONV_WIDTH + 4 * ATTN_WIDTH + 2 * LRU_WIDTH

COL_BLOCK = 512
N_COL_BLOCKS = IN_WIDTH // COL_BLOCK
W_BLOCK_LRU_X = 11
N_BRANCH_BLOCKS = 3
U_WIDTH = IN_WIDTH - N_BRANCH_BLOCKS * COL_BLOCK
UCOL_CGATE, UCOL_Q, UCOL_K, UCOL_V, UCOL_AGATE, UCOL_RGATE = 0, 512, 1536, 2560, 3584, 4608

VMEM_LIMIT_BYTES = 56 * 1024 * 1024
SUBLANES = 8
LANES = 128

TM_MIXER_IN = 1024
TM_OUT_PROJ = 512
TM_XATTN = 1024

RMS_EPS = 1e-6
LN_EPS = 1e-5


def _params(*sem):
    return pltpu.CompilerParams(dimension_semantics=sem, vmem_limit_bytes=VMEM_LIMIT_BYTES)


def _rms_scale(x):
    return lax.rsqrt(jnp.mean(x * x, axis=-1, keepdims=True) + RMS_EPS)


def _softplus(x):
    return jnp.maximum(x, 0.0) + jnp.log(1.0 + jnp.exp(-jnp.abs(x)))


def _silu(x):
    return x * jax.nn.sigmoid(x)


def _norm_matmul_kernel(x_ref, g_ref, w_ref, o_ref, h_ref):
    @pl.when(pl.program_id(1) == 0)
    def _():
        x = x_ref[...]
        h_ref[...] = (x * _rms_scale(x) * g_ref[...]).astype(BF16)

    o_ref[...] = jnp.dot(h_ref[...], w_ref[...], preferred_element_type=F32).astype(o_ref.dtype)


def _norm_matmul(x, g, w, layer, *, tm, tn, name):
    t, d = x.shape
    n = w.shape[2]
    return pl.pallas_call(
        _norm_matmul_kernel,
        out_shape=jax.ShapeDtypeStruct((t, n), BF16),
        grid=(t // tm, n // tn),
        in_specs=[
            pl.BlockSpec((tm, d), lambda i, j: (i, 0)),
            pl.BlockSpec((1, d), lambda i, j: (0, 0)),
            pl.BlockSpec((None, d, tn), lambda i, j: (layer, 0, j)),
        ],
        out_specs=pl.BlockSpec((tm, tn), lambda i, j: (i, j)),
        scratch_shapes=[pltpu.VMEM((tm, d), BF16)],
        compiler_params=_params("parallel", "arbitrary"),
        name=name,
    )(x, g, w)


CONV_HALO = 32
CONV_ROW_CHUNK = 64
BRANCH_STEP_ROWS = 128
LRU_HALO = SUBLANES


def _depthwise_conv_rows(buf_ref, dww_ref, dwb_ref, r0):
    base = CONV_HALO - (CONV_KERNEL - 1)
    rows = CONV_ROW_CHUNK + SUBLANES
    n_m = (base + CONV_KERNEL - 1) // SUBLANES + 1
    out = []
    for lb in range(CONV_WIDTH // LANES):
        lanes = slice(lb * LANES, (lb + 1) * LANES)
        window = buf_ref[pl.ds(r0, rows + SUBLANES * (n_m - 1)), lanes]
        a = None
        for rho in reversed(range(SUBLANES)):
            q = None
            for m in range(n_m):
                k = SUBLANES * m + rho - base
                if 0 <= k < CONV_KERNEL:
                    term = dww_ref[k:k + 1, lanes] * window[SUBLANES * m:SUBLANES * m + rows]
                    q = term if q is None else q + term
            a = q if a is None else q + pltpu.roll(a, rows - 1, 0)
        out.append(a[:CONV_ROW_CHUNK] + dwb_ref[:, lanes])
    return jnp.concatenate(out, axis=1)


def _conv_branch_rows(cbuf_ref, dww_ref, dwb_ref, lng_ref, lnb_ref, r0):
    u = jnp.concatenate([_depthwise_conv_rows(cbuf_ref, dww_ref, dwb_ref, r0 + c * CONV_ROW_CHUNK)
                         for c in range(BRANCH_STEP_ROWS // CONV_ROW_CHUNK)], axis=0)
    mu = jnp.mean(u, axis=-1, keepdims=True)
    uc = u - mu
    var = jnp.mean(uc * uc, axis=-1, keepdims=True)
    y = uc * lax.rsqrt(var + LN_EPS) * lng_ref[...] + lnb_ref[...]
    return _silu(y).astype(BF16)


def _lru_branch_rows(rbuf_ref, cw_ref, cb_ref, wg_ref, ba_ref, bx_ref, lam_ref, hstate_ref, a_ref, b_ref,
                     yl_ref, r0):
    w = LRU_WIDTH
    n = BRANCH_STEP_ROWS
    window = rbuf_ref[pl.ds(r0, n + LRU_HALO), :]
    xc = jnp.broadcast_to(cb_ref[...], (n, w))
    for k in range(LRU_CONV_KERNEL):
        shift = LRU_HALO - (LRU_CONV_KERNEL - 1) + k
        if shift % SUBLANES == 0:
            tap = window[shift:shift + n]
        else:
            tap = pltpu.roll(window, n + LRU_HALO - shift, 0)[:n]
        xc = xc + cw_ref[k:k + 1, :] * tap

    xcb = xc.astype(BF16)
    hd = w // LRU_HEADS
    pre = [jnp.dot(xcb[:, hix * hd:(hix + 1) * hd], wg_ref[hix], preferred_element_type=F32)
           for hix in range(LRU_HEADS)]
    r = jax.nn.sigmoid(jnp.concatenate([p[:, :hd] for p in pre], axis=-1) + ba_ref[...])
    gate = jax.nn.sigmoid(jnp.concatenate([p[:, hd:] for p in pre], axis=-1) + bx_ref[...])
    log_a = (-LRU_C) * r * _softplus(-lam_ref[...])
    a = jnp.exp(log_a)
    a_ref[...] = a
    b_ref[...] = jnp.sqrt(jnp.tanh(-log_a) * (1.0 + a * a)) * (gate * xc)

    row = lax.broadcasted_iota(jnp.int32, (SUBLANES, w), 0)
    h_prev = hstate_ref[...]
    for g in range(n // SUBLANES):
        rows = slice(g * SUBLANES, (g + 1) * SUBLANES)
        ag = a_ref[rows, :]
        bg = b_ref[rows, :]
        d = 1
        while d < SUBLANES:
            a_sh = pltpu.roll(ag, d, 0)
            b_sh = pltpu.roll(bg, d, 0)
            m = row >= d
            bg = jnp.where(m, ag * b_sh + bg, bg)
            ag = jnp.where(m, ag * a_sh, ag)
            d *= 2
        h = ag * h_prev + bg
        b_ref[rows, :] = h
        h_prev = jnp.broadcast_to(h[SUBLANES - 1:SUBLANES, :], (SUBLANES, w))
    hstate_ref[...] = h_prev
    yl_ref[pl.ds(r0, n), :] = b_ref[...].astype(yl_ref.dtype)


def _mixer_in_kernel(x0_ref, x1_ref, x2_ref, x3_ref, g_ref, w_ref,
                     dww_ref, dwb_ref, lng_ref, lnb_ref,
                     cw_ref, cb_ref, wg_ref, ba_ref, bx_ref, lam_ref,
                     u_ref, yc_ref, yl_ref,
                     h_ref, val_ref, cbuf_ref, rbuf_ref, hstate_ref, a_ref, b_ref, *, tiles_per_seq):
    i = pl.program_id(0)
    j = pl.program_id(1)
    x_refs = (x0_ref, x1_ref, x2_ref, x3_ref)
    tm = x0_ref.shape[0]
    first_branch_step = N_BRANCH_BLOCKS
    n_branch_steps = tm // BRANCH_STEP_ROWS

    def project():
        return jnp.dot(h_ref[...], w_ref[...], preferred_element_type=F32)

    @pl.when(j == 0)
    def _():
        @pl.when(i % tiles_per_seq == 0)
        def _():
            cbuf_ref[0:CONV_HALO, :] = jnp.zeros((CONV_HALO, CONV_WIDTH), F32)
            rbuf_ref[0:LRU_HALO, :] = jnp.zeros((LRU_HALO, LRU_WIDTH), F32)
            hstate_ref[...] = jnp.zeros_like(hstate_ref)
            cbuf_ref[CONV_HALO + tm:CONV_HALO + tm + SUBLANES, :] = jnp.zeros((SUBLANES, CONV_WIDTH), F32)

        @pl.when(i % tiles_per_seq != 0)
        def _():
            cbuf_ref[0:CONV_HALO, :] = cbuf_ref[tm:tm + CONV_HALO, :]
            rbuf_ref[0:LRU_HALO, :] = rbuf_ref[tm:tm + LRU_HALO, :]

        d = h_ref.shape[1]
        cw = d // len(x_refs)
        sq = None
        for xr in x_refs:
            xc = xr[...]
            part = jnp.sum(xc * xc, axis=-1, keepdims=True)
            sq = part if sq is None else sq + part
        inv = lax.rsqrt(sq * (1.0 / d) + RMS_EPS)
        for c, xr in enumerate(x_refs):
            h_ref[:, c * cw:(c + 1) * cw] = (xr[...] * inv * g_ref[:, c * cw:(c + 1) * cw]).astype(BF16)
        val_ref[...] = project()

    @pl.when(j == 1)
    def _():
        cbuf_ref[CONV_HALO:CONV_HALO + tm, :] = val_ref[...] * jax.nn.sigmoid(project())

    @pl.when(j == 2)
    def _():
        rbuf_ref[LRU_HALO:LRU_HALO + tm, :] = project()

    @pl.when(jnp.logical_and(j >= first_branch_step, j < first_branch_step + n_branch_steps))
    def _():
        r0 = pl.multiple_of((j - first_branch_step) * BRANCH_STEP_ROWS, BRANCH_STEP_ROWS)
        _lru_branch_rows(rbuf_ref, cw_ref, cb_ref, wg_ref, ba_ref, bx_ref, lam_ref, hstate_ref, a_ref, b_ref,
                         yl_ref, r0)
        yc_ref[pl.ds(r0, BRANCH_STEP_ROWS), :] = _conv_branch_rows(cbuf_ref, dww_ref, dwb_ref, lng_ref, lnb_ref, r0)
        u_ref[...] = project().astype(u_ref.dtype)

    @pl.when(j >= first_branch_step + n_branch_steps)
    def _():
        u_ref[...] = project().astype(u_ref.dtype)


X_CHUNKS = 4


def _mixer_in(x, g, w, layer, conv_params, lru_params, *, seq_len):
    t, d = x.shape
    tm = TM_MIXER_IN
    n_tiles = t // tm
    assert seq_len % tm == 0 and tm % BRANCH_STEP_ROWS == 0 and BRANCH_STEP_ROWS % CONV_ROW_CHUNK == 0
    assert N_BRANCH_BLOCKS + tm // BRANCH_STEP_ROWS <= N_COL_BLOCKS and d % X_CHUNKS == 0
    dww, dwb, lng, lnb = conv_params
    cw, cb, wg, ba, bx, lam = lru_params
    hd = LRU_WIDTH // LRU_HEADS

    def w_block(i, j):
        blk = jnp.where(j == 2, W_BLOCK_LRU_X, jnp.where(jnp.logical_and(j > 2, j <= W_BLOCK_LRU_X), j - 1, j))
        return (layer, 0, blk)

    def x_chunk(c):
        def index(i, j):
            ahead = (j >= N_COL_BLOCKS - X_CHUNKS + c).astype(jnp.int32)
            return (jnp.minimum(i + ahead, n_tiles - 1), c)
        return pl.BlockSpec((tm, d // X_CHUNKS), index)

    const = lambda shape: pl.BlockSpec(shape, lambda i, j: (0,) * len(shape))
    return pl.pallas_call(
        functools.partial(_mixer_in_kernel, tiles_per_seq=seq_len // tm),
        out_shape=(jax.ShapeDtypeStruct((t, U_WIDTH), BF16),
                   jax.ShapeDtypeStruct((t, CONV_WIDTH), BF16),
                   jax.ShapeDtypeStruct((t, LRU_WIDTH), BF16)),
        grid=(n_tiles, N_COL_BLOCKS),
        in_specs=[
            *[x_chunk(c) for c in range(X_CHUNKS)],
            const((1, d)),
            pl.BlockSpec((None, d, COL_BLOCK), w_block),
            const((CONV_KERNEL, CONV_WIDTH)), const((1, CONV_WIDTH)), const((1, CONV_WIDTH)),
            const((1, CONV_WIDTH)),
            const((LRU_CONV_KERNEL, LRU_WIDTH)), const((1, LRU_WIDTH)),
            const((LRU_HEADS, hd, 2 * hd)), const((1, LRU_WIDTH)), const((1, LRU_WIDTH)), const((1, LRU_WIDTH)),
        ],
        out_specs=(
            pl.BlockSpec((tm, COL_BLOCK), lambda i, j: (i, jnp.maximum(j - N_BRANCH_BLOCKS, 0))),
            pl.BlockSpec((tm, CONV_WIDTH), lambda i, j: (i, 0)),
            pl.BlockSpec((tm, LRU_WIDTH), lambda i, j: (i, 0)),
        ),
        scratch_shapes=[
            pltpu.VMEM((tm, d), BF16),
            pltpu.VMEM((tm, CONV_WIDTH), F32),
            pltpu.VMEM((CONV_HALO + tm + SUBLANES, CONV_WIDTH), F32),
            pltpu.VMEM((LRU_HALO + tm, LRU_WIDTH), F32),
            pltpu.VMEM((SUBLANES, LRU_WIDTH), F32),
            pltpu.VMEM((BRANCH_STEP_ROWS, LRU_WIDTH), F32),
            pltpu.VMEM((BRANCH_STEP_ROWS, LRU_WIDTH), F32),
        ],
        compiler_params=_params("arbitrary", "arbitrary"),
        name="mixer_in",
    )(*([x] * X_CHUNKS), g, w, dww, dwb, lng, lnb, cw, cb, wg, ba, bx, lam)


SB_T = 128
SB_GROUP = 16
SB_EAGER = 3
SB_PAD = SB_GROUP - 1
SB_LOG_WEIGHT_FLOOR = -104.0


def _sb_split(log_1mb):
    hi = log_1mb.astype(BF16)
    lo = (log_1mb - hi.astype(F32)).astype(BF16)
    return jnp.concatenate([hi, lo], axis=1)


def _sb_attn_kernel(q_ref, k_ref, v_ref, uo_ref, o_ref, kp_ref, vp_ref, acc_ref, carry_ref):
    s_len = q_ref.shape[0]
    scale = HEAD_DIM ** -0.5
    t_sz, n_grp = SB_T, SB_GROUP
    pad_rows = SB_PAD * t_sz
    kp_ref[0:pad_rows, :] = jnp.zeros((pad_rows, HEAD_DIM), BF16)
    vp_ref[0:pad_rows, :] = jnp.zeros((pad_rows, HEAD_DIM), BF16)
    kp_ref[pad_rows:pad_rows + s_len, :] = k_ref[...]
    vp_ref[pad_rows:pad_rows + s_len, :] = v_ref[...]
    causal = (lax.broadcasted_iota(jnp.int32, (t_sz, t_sz), 1)
              < lax.broadcasted_iota(jnp.int32, (t_sz, t_sz), 0))

    def q_tile(i):
        return q_ref[pl.ds(pl.multiple_of(i * t_sz, t_sz), t_sz), :]

    def neg_scores(q, k):
        zn = lax.dot_general(q, k, (((1,), (1,)), ((), ())), preferred_element_type=F32) * (-scale)
        log_1mb = jnp.minimum(zn, 0.0) - jnp.log(1.0 + jnp.exp(-jnp.abs(zn)))
        return zn, log_1mb

    def group(g, _):
        i0 = g * n_grp

        def eager():
            width = SB_EAGER * t_sz
            zns, ls, starts, lhs = [], [], [], []
            for r in range(n_grp):
                start = pl.multiple_of((i0 + r + SB_PAD - (SB_EAGER - 1)) * t_sz, t_sz)
                zn, log_1mb = neg_scores(q_tile(i0 + r), kp_ref[pl.ds(start, width), :])
                pieces = []
                for c in range(SB_EAGER):
                    lc = log_1mb[:, c * t_sz:(c + 1) * t_sz]
                    if c == SB_EAGER - 1:
                        lc = jnp.where(causal, lc, 0.0)
                    pieces.append(_sb_split(lc))
                lhs.append(jnp.concatenate(pieces, axis=0))
                zns.append(zn)
                ls.append(log_1mb)
                starts.append(start)
            sums = [jnp.dot(x, uo_ref[...], preferred_element_type=F32) for x in lhs]
            m = None
            for r in range(n_grp):
                carry = None
                ws = [None] * SB_EAGER
                for c in reversed(range(SB_EAGER)):
                    cols = slice(c * t_sz, (c + 1) * t_sz)
                    sc = sums[r][cols]
                    after = sc[:, :t_sz] if carry is None else carry + sc[:, :t_sz]
                    w = jnp.exp((ls[r][:, cols] - zns[r][:, cols]) + after)
                    if c == SB_EAGER - 1:
                        w = jnp.where(causal, w, 0.0)
                    ws[c] = w.astype(BF16)
                    carry = sc[:, t_sz:] if carry is None else carry + sc[:, t_sz:]
                v = vp_ref[pl.ds(starts[r], width), :]
                acc_ref[r] = jnp.dot(jnp.concatenate(ws, axis=1), v, preferred_element_type=F32)
                carry_ref[r] = carry
                m = carry if m is None else jnp.maximum(m, carry)
            return jnp.max(m)

        def step(d):
            zns, ls, starts, lhs = [], [], [], []
            for r in range(n_grp):
                start = pl.multiple_of((i0 + r + SB_PAD - d) * t_sz, t_sz)
                zn, log_1mb = neg_scores(q_tile(i0 + r), kp_ref[pl.ds(start, t_sz), :])
                lhs.append(_sb_split(log_1mb))
                zns.append(zn)
                ls.append(log_1mb)
                starts.append(start)
            sums = jnp.dot(jnp.concatenate(lhs, axis=0), uo_ref[...], preferred_element_type=F32)
            m = None
            for r in range(n_grp):
                sr = sums[r * t_sz:(r + 1) * t_sz]
                w = jnp.exp((ls[r] - zns[r]) + (carry_ref[r] + sr[:, :t_sz]))
                v = vp_ref[pl.ds(starts[r], t_sz), :]
                acc_ref[r] += jnp.dot(w.astype(BF16), v, preferred_element_type=F32)
                carry = carry_ref[r] + sr[:, t_sz:]
                carry_ref[r] = carry
                m = carry if m is None else jnp.maximum(m, carry)
            return jnp.max(m)

        def more(c):
            d, m = c
            return jnp.logical_and(d <= i0 + n_grp - 1, m > SB_LOG_WEIGHT_FLOOR)

        lax.while_loop(more, lambda c: (c[0] + 1, step(c[0])), (jnp.int32(SB_EAGER), eager()))
        for r in range(n_grp):
            o_ref[pl.ds(pl.multiple_of((i0 + r) * t_sz, t_sz), t_sz), :] = acc_ref[r].astype(o_ref.dtype)
        return 0

    lax.fori_loop(0, s_len // (t_sz * n_grp), group, 0)


def _sb_attention(u3):
    b, s, _ = u3.shape
    dh = HEAD_DIM
    assert SB_EAGER - 1 <= SB_PAD and s % (SB_T * SB_GROUP) == 0
    row = lax.broadcasted_iota(jnp.int32, (2 * SB_T, 2 * SB_T), 0) % SB_T
    col = lax.broadcasted_iota(jnp.int32, (2 * SB_T, 2 * SB_T), 1)
    uo = jnp.where((col >= SB_T) | (row > col), 1.0, 0.0).astype(BF16)

    def head_spec(col0):
        return pl.BlockSpec((None, s, dh), lambda bi, hi: (bi, 0, col0 // dh + hi))

    padded = pltpu.VMEM((s + SB_PAD * SB_T, dh), BF16)
    return pl.pallas_call(
        _sb_attn_kernel,
        out_shape=jax.ShapeDtypeStruct((b, s, ATTN_WIDTH), BF16),
        grid=(b, ATTN_HEADS),
        in_specs=[head_spec(UCOL_Q), head_spec(UCOL_K), head_spec(UCOL_V),
                  pl.BlockSpec((2 * SB_T, 2 * SB_T), lambda bi, hi: (0, 0))],
        out_specs=pl.BlockSpec((None, s, dh), lambda bi, hi: (bi, 0, hi)),
        scratch_shapes=[padded, padded,
                        pltpu.VMEM((SB_GROUP, SB_T, dh), F32), pltpu.VMEM((SB_GROUP, SB_T, SB_T), F32)],
        compiler_params=_params("parallel", "parallel"),
        name="sb_attention",
    )(u3, u3, u3, uo)


def _out_proj_kernel(x_ref, yc_ref, ya_ref, yl_ref, gc_ref, ga0_ref, ga1_ref, gl_ref,
                     pw_ref, nc_ref, na_ref, nl_ref, w_ref, o_ref):
    def normed(y, n_ref):
        return y * _rms_scale(y) * n_ref[...]

    def gated(yn, gate_ref):
        return (yn * _silu(gate_ref[...].astype(F32))).astype(BF16)

    half = ATTN_WIDTH // 2
    y_conv = jnp.dot(yc_ref[...], pw_ref[...], preferred_element_type=F32)
    ya = normed(ya_ref[...].astype(F32), na_ref)
    parts = [
        (gated(normed(y_conv, nc_ref), gc_ref), 0),
        (gated(ya[:, :half], ga0_ref), CONV_WIDTH),
        (gated(ya[:, half:], ga1_ref), CONV_WIDTH + half),
        (gated(normed(yl_ref[...].astype(F32), nl_ref), gl_ref), CONV_WIDTH + ATTN_WIDTH),
    ]
    acc = x_ref[...]
    for y, r0 in parts:
        acc = acc + jnp.dot(y, w_ref[r0:r0 + y.shape[1], :], preferred_element_type=F32)
    o_ref[...] = acc


def _out_proj(x, yc, ya, yl, u, pw, nc, na, nl, w, layer):
    t, d = x.shape
    tm = TM_OUT_PROJ
    gw = COL_BLOCK
    return pl.pallas_call(
        _out_proj_kernel,
        out_shape=jax.ShapeDtypeStruct((t, d), F32),
        grid=(t // tm,),
        in_specs=[
            pl.BlockSpec((tm, d), lambda i: (i, 0)),
            pl.BlockSpec((tm, CONV_WIDTH), lambda i: (i, 0)),
            pl.BlockSpec((tm, ATTN_WIDTH), lambda i: (i, 0)),
            pl.BlockSpec((tm, LRU_WIDTH), lambda i: (i, 0)),
            pl.BlockSpec((tm, gw), lambda i: (i, UCOL_CGATE // gw)),
            pl.BlockSpec((tm, gw), lambda i: (i, UCOL_AGATE // gw)),
            pl.BlockSpec((tm, gw), lambda i: (i, UCOL_AGATE // gw + 1)),
            pl.BlockSpec((tm, gw), lambda i: (i, UCOL_RGATE // gw)),
            pl.BlockSpec((CONV_WIDTH, CONV_WIDTH), lambda i: (0, 0)),
            pl.BlockSpec((1, CONV_WIDTH), lambda i: (0, 0)),
            pl.BlockSpec((1, ATTN_WIDTH), lambda i: (0, 0)),
            pl.BlockSpec((1, LRU_WIDTH), lambda i: (0, 0)),
            pl.BlockSpec((None, d, d), lambda i: (layer, 0, 0)),
        ],
        out_specs=pl.BlockSpec((tm, d), lambda i: (i, 0)),
        compiler_params=_params("parallel"),
        name="out_proj",
    )(x, yc, ya, yl, u, u, u, u, pw, nc, na, nl, w)


def _xattn_kernel(x_ref, g_ref, wq_ref, k_ref, v_ref, wo_ref, fg_ref, o_ref, *, final_norm):
    x = x_ref[...]
    h = (x * _rms_scale(x) * g_ref[...]).astype(BF16)
    q = jnp.dot(h, wq_ref[...], preferred_element_type=F32).astype(BF16)
    dh = XATTN_WIDTH // XATTN_HEADS
    scale = dh ** -0.5
    heads = []
    for n in range(XATTN_HEADS):
        qh = q[:, n * dh:(n + 1) * dh]
        kh = k_ref[:, n * dh:(n + 1) * dh]
        vh = v_ref[:, n * dh:(n + 1) * dh]
        s = lax.dot_general(qh, kh, (((1,), (1,)), ((), ())), preferred_element_type=F32) * scale
        e = jnp.exp(s - jnp.max(s, axis=-1, keepdims=True))
        p = (e / jnp.sum(e, axis=-1, keepdims=True)).astype(BF16)
        heads.append(jnp.dot(p, vh, preferred_element_type=F32).astype(BF16))
    acc = x + jnp.dot(jnp.concatenate(heads, axis=-1), wo_ref[...], preferred_element_type=F32)
    if final_norm:
        acc = acc * _rms_scale(acc) * fg_ref[...]
    o_ref[...] = acc


def _xattn(x3, g, wq, kv, wo, fg, layer, *, final_norm):
    b, s, d = x3.shape
    m = kv.shape[1]
    tm = TM_XATTN
    xw = XATTN_WIDTH
    return pl.pallas_call(
        functools.partial(_xattn_kernel, final_norm=final_norm),
        out_shape=jax.ShapeDtypeStruct((b, s, d), F32),
        grid=(b, s // tm),
        in_specs=[
            pl.BlockSpec((None, tm, d), lambda bi, ti: (bi, ti, 0)),
            pl.BlockSpec((1, d), lambda bi, ti: (0, 0)),
            pl.BlockSpec((None, d, xw), lambda bi, ti: (layer, 0, 0)),
            pl.BlockSpec((None, m, xw), lambda bi, ti: (bi, 0, 0)),
            pl.BlockSpec((None, m, xw), lambda bi, ti: (bi, 0, 1)),
            pl.BlockSpec((None, xw, d), lambda bi, ti: (layer, 0, 0)),
            pl.BlockSpec((1, d), lambda bi, ti: (0, 0)),
        ],
        out_specs=pl.BlockSpec((None, tm, d), lambda bi, ti: (bi, ti, 0)),
        compiler_params=_params("parallel", "parallel"),
        name="xattn_final" if final_norm else "xattn",
    )(x3, g, wq, kv, kv, wo, fg)


def kernel(x, mem, mix_norm_g, w_in, conv_dw_w, conv_dw_b, conv_ln_g, conv_ln_b, conv_pw_w,
           lru_conv_w, lru_conv_b, lru_wa, lru_ba, lru_wx, lru_bx, lru_lambda,
           out_norm_conv, out_norm_attn, out_norm_lru, w_out,
           xattn_norm_g, mem_norm_g, xattn_wq, xattn_wkv, xattn_wo, final_norm_g):
    b, s, d = x.shape
    m = mem.shape[1]
    depth = w_in.shape[0]
    t = b * s
    row = lambda a: a.reshape(1, -1).astype(F32)

    xt = x.reshape(t, d)
    memt = mem.reshape(b * m, d)
    w_in_b, w_out_b = w_in.astype(BF16), w_out.astype(BF16)
    wq_b, wkv_b, wo_b = xattn_wq.astype(BF16), xattn_wkv.astype(BF16), xattn_wo.astype(BF16)
    for l in range(depth):
        conv_params = (conv_dw_w[l], row(conv_dw_b[l]), row(conv_ln_g[l]), row(conv_ln_b[l]))
        lru_gates = jnp.concatenate([lru_wa[l], lru_wx[l]], axis=-1).astype(BF16)
        lru_params = (lru_conv_w[l], row(lru_conv_b[l]), lru_gates, row(lru_ba[l]), row(lru_bx[l]),
                      row(lru_lambda[l]))
        u, conv_act, y_lru = _mixer_in(xt, row(mix_norm_g[l]), w_in_b, l, conv_params, lru_params, seq_len=s)
        y_attn = _sb_attention(u.reshape(b, s, U_WIDTH))
        xt = _out_proj(xt, conv_act, y_attn.reshape(t, -1), y_lru, u, conv_pw_w[l].astype(BF16),
                       row(out_norm_conv[l]), row(out_norm_attn[l]), row(out_norm_lru[l]), w_out_b, l)
        kv = _norm_matmul(memt, row(mem_norm_g[l]), wkv_b, l, tm=256, tn=1024,
                          name="mem_kv").reshape(b, m, 2 * XATTN_WIDTH)
        xt = _xattn(xt.reshape(b, s, d), row(xattn_norm_g[l]), wq_b, kv, wo_b, row(final_norm_g), l,
                    final_norm=(l == depth - 1)).reshape(t, d)
    return xt.reshape(b, s, d)
```

```python
import functools

import jax
import jax.numpy as jnp
from jax import lax
from jax.experimental import pallas as pl
from jax.experimental.pallas import tpu as pltpu

F32 = jnp.float32
BF16 = jnp.bfloat16

D_MODEL = 2048
CONV_WIDTH = 512
CONV_KERNEL = 31
HEAD_DIM = 128
ATTN_WIDTH = 1024
ATTN_HEADS = 8
LRU_WIDTH = 512
LRU_HEADS = 4
LRU_CONV_KERNEL = 4
LRU_C = 8.0
XATTN_HEADS = 4
XATTN_WIDTH = 512
IN_WIDTH = 3 * CONV_WIDTH + 4 * ATTN_WIDTH + 2 * LRU_WIDTH

COL_BLOCK = 512
N_COL_BLOCKS = IN_WIDTH // COL_BLOCK
W_BLOCK_LRU_X = 11
N_BRANCH_BLOCKS = 3
U_WIDTH = IN_WIDTH - N_BRANCH_BLOCKS * COL_BLOCK
UCOL_CGATE, UCOL_Q, UCOL_K, UCOL_V, UCOL_AGATE, UCOL_RGATE = 0, 512, 1536, 2560, 3584, 4608

VMEM_LIMIT_BYTES = 56 * 1024 * 1024
SUBLANES = 8
LANES = 128

TM_MIXER_IN = 1024
TM_OUT_PROJ = 512
TM_XATTN = 1024

RMS_EPS = 1e-6
LN_EPS = 1e-5


def _params(*sem):
    return pltpu.CompilerParams(dimension_semantics=sem, vmem_limit_bytes=VMEM_LIMIT_BYTES)


def _rms_scale(x):
    return lax.rsqrt(jnp.mean(x * x, axis=-1, keepdims=True) + RMS_EPS)


def _softplus(x):
    return jnp.maximum(x, 0.0) + jnp.log(1.0 + jnp.exp(-jnp.abs(x)))


def _silu(x):
    return x * jax.nn.sigmoid(x)


def _norm_matmul_kernel(x_ref, g_ref, w_ref, o_ref, h_ref):
    @pl.when(pl.program_id(1) == 0)
    def _():
        x = x_ref[...]
        h_ref[...] = (x * _rms_scale(x) * g_ref[...]).astype(BF16)

    o_ref[...] = jnp.dot(h_ref[...], w_ref[...], preferred_element_type=F32).astype(o_ref.dtype)


def _norm_matmul(x, g, w, layer, *, tm, tn, name):
    t, d = x.shape
    n = w.shape[2]
    return pl.pallas_call(
        _norm_matmul_kernel,
        out_shape=jax.ShapeDtypeStruct((t, n), BF16),
        grid=(t // tm, n // tn),
        in_specs=[
            pl.BlockSpec((tm, d), lambda i, j: (i, 0)),
            pl.BlockSpec((1, d), lambda i, j: (0, 0)),
            pl.BlockSpec((None, d, tn), lambda i, j: (layer, 0, j)),
        ],
        out_specs=pl.BlockSpec((tm, tn), lambda i, j: (i, j)),
        scratch_shapes=[pltpu.VMEM((tm, d), BF16)],
        compiler_params=_params("parallel", "arbitrary"),
        name=name,
    )(x, g, w)


CONV_HALO = 32
CONV_ROW_CHUNK = 64
BRANCH_STEP_ROWS = 128
LRU_HALO = SUBLANES


def _depthwise_conv_rows(buf_ref, dww_ref, dwb_ref, r0):
    base = CONV_HALO - (CONV_KERNEL - 1)
    rows = CONV_ROW_CHUNK + SUBLANES
    n_m = (base + CONV_KERNEL - 1) // SUBLANES + 1
    out = []
    for lb in range(CONV_WIDTH // LANES):
        lanes = slice(lb * LANES, (lb + 1) * LANES)
        window = buf_ref[pl.ds(r0, rows + SUBLANES * (n_m - 1)), lanes]
        a = None
        for rho in reversed(range(SUBLANES)):
            q = None
            for m in range(n_m):
                k = SUBLANES * m + rho - base
                if 0 <= k < CONV_KERNEL:
                    term = dww_ref[k:k + 1, lanes] * window[SUBLANES * m:SUBLANES * m + rows]
                    q = term if q is None else q + term
            a = q if a is None else q + pltpu.roll(a, rows - 1, 0)
        out.append(a[:CONV_ROW_CHUNK] + dwb_ref[:, lanes])
    return jnp.concatenate(out, axis=1)


def _conv_branch_rows(cbuf_ref, dww_ref, dwb_ref, lng_ref, lnb_ref, r0):
    u = jnp.concatenate([_depthwise_conv_rows(cbuf_ref, dww_ref, dwb_ref, r0 + c * CONV_ROW_CHUNK)
                         for c in range(BRANCH_STEP_ROWS // CONV_ROW_CHUNK)], axis=0)
    mu = jnp.mean(u, axis=-1, keepdims=True)
    uc = u - mu
    var = jnp.mean(uc * uc, axis=-1, keepdims=True)
    y = uc * lax.rsqrt(var + LN_EPS) * lng_ref[...] + lnb_ref[...]
    return _silu(y).astype(BF16)


def _lru_branch_rows(rbuf_ref, cw_ref, cb_ref, wg_ref, ba_ref, bx_ref, lam_ref, hstate_ref, a_ref, b_ref,
                     yl_ref, r0):
    w = LRU_WIDTH
    n = BRANCH_STEP_ROWS
    window = rbuf_ref[pl.ds(r0, n + LRU_HALO), :]
    xc = jnp.broadcast_to(cb_ref[...], (n, w))
    for k in range(LRU_CONV_KERNEL):
        shift = LRU_HALO - (LRU_CONV_KERNEL - 1) + k
        if shift % SUBLANES == 0:
            tap = window[shift:shift + n]
        else:
            tap = pltpu.roll(window, n + LRU_HALO - shift, 0)[:n]
        xc = xc + cw_ref[k:k + 1, :] * tap

    xcb = xc.astype(BF16)
    hd = w // LRU_HEADS
    pre = [jnp.dot(xcb[:, hix * hd:(hix + 1) * hd], wg_ref[hix], preferred_element_type=F32)
           for hix in range(LRU_HEADS)]
    r = jax.nn.sigmoid(jnp.concatenate([p[:, :hd] for p in pre], axis=-1) + ba_ref[...])
    gate = jax.nn.sigmoid(jnp.concatenate([p[:, hd:] for p in pre], axis=-1) + bx_ref[...])
    log_a = (-LRU_C) * r * _softplus(-lam_ref[...])
    a = jnp.exp(log_a)
    a_ref[...] = a
    b_ref[...] = jnp.sqrt(jnp.tanh(-log_a) * (1.0 + a * a)) * (gate * xc)

    row = lax.broadcasted_iota(jnp.int32, (SUBLANES, w), 0)
    h_prev = hstate_ref[...]
    for g in range(n // SUBLANES):
        rows = slice(g * SUBLANES, (g + 1) * SUBLANES)
        ag = a_ref[rows, :]
        bg = b_ref[rows, :]
        d = 1
        while d < SUBLANES:
            a_sh = pltpu.roll(ag, d, 0)
            b_sh = pltpu.roll(bg, d, 0)
            m = row >= d
            bg = jnp.where(m, ag * b_sh + bg, bg)
            ag = jnp.where(m, ag * a_sh, ag)
            d *= 2
        h = ag * h_prev + bg
        b_ref[rows, :] = h
        h_prev = jnp.broadcast_to(h[SUBLANES - 1:SUBLANES, :], (SUBLANES, w))
    hstate_ref[...] = h_prev
    yl_ref[pl.ds(r0, n), :] = b_ref[...].astype(yl_ref.dtype)


def _mixer_in_kernel(x0_ref, x1_ref, x2_ref, x3_ref, g_ref, w_ref,
                     dww_ref, dwb_ref, lng_ref, lnb_ref,
                     cw_ref, cb_ref, wg_ref, ba_ref, bx_ref, lam_ref,
                     u_ref, yc_ref, yl_ref,
                     h_ref, val_ref, cbuf_ref, rbuf_ref, hstate_ref, a_ref, b_ref, *, tiles_per_seq):
    i = pl.program_id(0)
    j = pl.program_id(1)
    x_refs = (x0_ref, x1_ref, x2_ref, x3_ref)
    tm = x0_ref.shape[0]
    first_branch_step = N_BRANCH_BLOCKS
    n_branch_steps = tm // BRANCH_STEP_ROWS

    def project():
        return jnp.dot(h_ref[...], w_ref[...], preferred_element_type=F32)

    @pl.when(j == 0)
    def _():
        @pl.when(i % tiles_per_seq == 0)
        def _():
            cbuf_ref[0:CONV_HALO, :] = jnp.zeros((CONV_HALO, CONV_WIDTH), F32)
            rbuf_ref[0:LRU_HALO, :] = jnp.zeros((LRU_HALO, LRU_WIDTH), F32)
            hstate_ref[...] = jnp.zeros_like(hstate_ref)
            cbuf_ref[CONV_HALO + tm:CONV_HALO + tm + SUBLANES, :] = jnp.zeros((SUBLANES, CONV_WIDTH), F32)

        @pl.when(i % tiles_per_seq != 0)
        def _():
            cbuf_ref[0:CONV_HALO, :] = cbuf_ref[tm:tm + CONV_HALO, :]
            rbuf_ref[0:LRU_HALO, :] = rbuf_ref[tm:tm + LRU_HALO, :]

        d = h_ref.shape[1]
        cw = d // len(x_refs)
        sq = None
        for xr in x_refs:
            xc = xr[...]
            part = jnp.sum(xc * xc, axis=-1, keepdims=True)
            sq = part if sq is None else sq + part
        inv = lax.rsqrt(sq * (1.0 / d) + RMS_EPS)
        for c, xr in enumerate(x_refs):
            h_ref[:, c * cw:(c + 1) * cw] = (xr[...] * inv * g_ref[:, c * cw:(c + 1) * cw]).astype(BF16)
        val_ref[...] = project()

    @pl.when(j == 1)
    def _():
        cbuf_ref[CONV_HALO:CONV_HALO + tm, :] = val_ref[...] * jax.nn.sigmoid(project())

    @pl.when(j == 2)
    def _():
        rbuf_ref[LRU_HALO:LRU_HALO + tm, :] = project()

    @pl.when(jnp.logical_and(j >= first_branch_step, j < first_branch_step + n_branch_steps))
    def _():
        r0 = pl.multiple_of((j - first_branch_step) * BRANCH_STEP_ROWS, BRANCH_STEP_ROWS)
        _lru_branch_rows(rbuf_ref, cw_ref, cb_ref, wg_ref, ba_ref, bx_ref, lam_ref, hstate_ref, a_ref, b_ref,
                         yl_ref, r0)
        yc_ref[pl.ds(r0, BRANCH_STEP_ROWS), :] = _conv_branch_rows(cbuf_ref, dww_ref, dwb_ref, lng_ref, lnb_ref, r0)
        u_ref[...] = project().astype(u_ref.dtype)

    @pl.when(j >= first_branch_step + n_branch_steps)
    def _():
        u_ref[...] = project().astype(u_ref.dtype)


X_CHUNKS = 4


def _mixer_in(x, g, w, layer, conv_params, lru_params, *, seq_len):
    t, d = x.shape
    tm = TM_MIXER_IN
    n_tiles = t // tm
    assert seq_len % tm == 0 and tm % BRANCH_STEP_ROWS == 0 and BRANCH_STEP_ROWS % CONV_ROW_CHUNK == 0
    assert N_BRANCH_BLOCKS + tm // BRANCH_STEP_ROWS <= N_COL_BLOCKS and d % X_CHUNKS == 0
    dww, dwb, lng, lnb = conv_params
    cw, cb, wg, ba, bx, lam = lru_params
    hd = LRU_WIDTH // LRU_HEADS

    def w_block(i, j):
        blk = jnp.where(j == 2, W_BLOCK_LRU_X, jnp.where(jnp.logical_and(j > 2, j <= W_BLOCK_LRU_X), j - 1, j))
        return (layer, 0, blk)

    def x_chunk(c):
        def index(i, j):
            ahead = (j >= N_COL_BLOCKS - X_CHUNKS + c).astype(jnp.int32)
            return (jnp.minimum(i + ahead, n_tiles - 1), c)
        return pl.BlockSpec((tm, d // X_CHUNKS), index)

    const = lambda shape: pl.BlockSpec(shape, lambda i, j: (0,) * len(shape))
    return pl.pallas_call(
        functools.partial(_mixer_in_kernel, tiles_per_seq=seq_len // tm),
        out_shape=(jax.ShapeDtypeStruct((t, U_WIDTH), BF16),
                   jax.ShapeDtypeStruct((t, CONV_WIDTH), BF16),
                   jax.ShapeDtypeStruct((t, LRU_WIDTH), BF16)),
        grid=(n_tiles, N_COL_BLOCKS),
        in_specs=[
            *[x_chunk(c) for c in range(X_CHUNKS)],
            const((1, d)),
            pl.BlockSpec((None, d, COL_BLOCK), w_block),
            const((CONV_KERNEL, CONV_WIDTH)), const((1, CONV_WIDTH)), const((1, CONV_WIDTH)),
            const((1, CONV_WIDTH)),
            const((LRU_CONV_KERNEL, LRU_WIDTH)), const((1, LRU_WIDTH)),
            const((LRU_HEADS, hd, 2 * hd)), const((1, LRU_WIDTH)), const((1, LRU_WIDTH)), const((1, LRU_WIDTH)),
        ],
        out_specs=(
            pl.BlockSpec((tm, COL_BLOCK), lambda i, j: (i, jnp.maximum(j - N_BRANCH_BLOCKS, 0))),
            pl.BlockSpec((tm, CONV_WIDTH), lambda i, j: (i, 0)),
            pl.BlockSpec((tm, LRU_WIDTH), lambda i, j: (i, 0)),
        ),
        scratch_shapes=[
            pltpu.VMEM((tm, d), BF16),
            pltpu.VMEM((tm, CONV_WIDTH), F32),
            pltpu.VMEM((CONV_HALO + tm + SUBLANES, CONV_WIDTH), F32),
            pltpu.VMEM((LRU_HALO + tm, LRU_WIDTH), F32),
            pltpu.VMEM((SUBLANES, LRU_WIDTH), F32),
            pltpu.VMEM((BRANCH_STEP_ROWS, LRU_WIDTH), F32),
            pltpu.VMEM((BRANCH_STEP_ROWS, LRU_WIDTH), F32),
        ],
        compiler_params=_params("arbitrary", "arbitrary"),
        name="mixer_in",
    )(*([x] * X_CHUNKS), g, w, dww, dwb, lng, lnb, cw, cb, wg, ba, bx, lam)


SB_T = 128
SB_GROUP = 32
SB_EAGER = 3
SB_PAD = SB_GROUP - 1
SB_LOG_WEIGHT_FLOOR = -104.0


def _sb_split(log_1mb):
    hi = log_1mb.astype(BF16)
    lo = (log_1mb - hi.astype(F32)).astype(BF16)
    return jnp.concatenate([hi, lo], axis=1)


def _sb_attn_kernel(q_ref, k_ref, v_ref, uo_ref, o_ref, kp_ref, vp_ref, acc_ref, carry_ref):
    s_len = q_ref.shape[0]
    scale = HEAD_DIM ** -0.5
    t_sz, n_grp = SB_T, SB_GROUP
    pad_rows = SB_PAD * t_sz
    kp_ref[0:pad_rows, :] = jnp.zeros((pad_rows, HEAD_DIM), BF16)
    vp_ref[0:pad_rows, :] = jnp.zeros((pad_rows, HEAD_DIM), BF16)
    kp_ref[pad_rows:pad_rows + s_len, :] = k_ref[...]
    vp_ref[pad_rows:pad_rows + s_len, :] = v_ref[...]
    causal = (lax.broadcasted_iota(jnp.int32, (t_sz, t_sz), 1)
              < lax.broadcasted_iota(jnp.int32, (t_sz, t_sz), 0))

    def q_tile(i):
        return q_ref[pl.ds(pl.multiple_of(i * t_sz, t_sz), t_sz), :]

    def neg_scores(q, k):
        zn = lax.dot_general(q, k, (((1,), (1,)), ((), ())), preferred_element_type=F32) * (-scale)
        log_1mb = jnp.minimum(zn, 0.0) - jnp.log(1.0 + jnp.exp(-jnp.abs(zn)))
        return zn, log_1mb

    def group(g, _):
        i0 = g * n_grp

        def eager():
            width = SB_EAGER * t_sz
            zns, ls, starts, lhs = [], [], [], []
            for r in range(n_grp):
                start = pl.multiple_of((i0 + r + SB_PAD - (SB_EAGER - 1)) * t_sz, t_sz)
                zn, log_1mb = neg_scores(q_tile(i0 + r), kp_ref[pl.ds(start, width), :])
                pieces = []
                for c in range(SB_EAGER):
                    lc = log_1mb[:, c * t_sz:(c + 1) * t_sz]
                    if c == SB_EAGER - 1:
                        lc = jnp.where(causal, lc, 0.0)
                    pieces.append(_sb_split(lc))
                lhs.append(jnp.concatenate(pieces, axis=0))
                zns.append(zn)
                ls.append(log_1mb)
                starts.append(start)
            sums = [jnp.dot(x, uo_ref[...], preferred_element_type=F32) for x in lhs]
            m = None
            for r in range(n_grp):
                carry = None
                ws = [None] * SB_EAGER
                for c in reversed(range(SB_EAGER)):
                    cols = slice(c * t_sz, (c + 1) * t_sz)
                    sc = sums[r][cols]
                    after = sc[:, :t_sz] if carry is None else carry + sc[:, :t_sz]
                    w = jnp.exp((ls[r][:, cols] - zns[r][:, cols]) + after)
                    if c == SB_EAGER - 1:
                        w = jnp.where(causal, w, 0.0)
                    ws[c] = w.astype(BF16)
                    carry = sc[:, t_sz:] if carry is None else carry + sc[:, t_sz:]
                v = vp_ref[pl.ds(starts[r], width), :]
                acc_ref[r] = jnp.dot(jnp.concatenate(ws, axis=1), v, preferred_element_type=F32)
                carry_ref[r] = carry
                m = carry if m is None else jnp.maximum(m, carry)
            return jnp.max(m)

        def step(d):
            zns, ls, starts, lhs = [], [], [], []
            for r in range(n_grp):
                start = pl.multiple_of((i0 + r + SB_PAD - d) * t_sz, t_sz)
                zn, log_1mb = neg_scores(q_tile(i0 + r), kp_ref[pl.ds(start, t_sz), :])
                lhs.append(_sb_split(log_1mb))
                zns.append(zn)
                ls.append(log_1mb)
                starts.append(start)
            sums = jnp.dot(jnp.concatenate(lhs, axis=0), uo_ref[...], preferred_element_type=F32)
            m = None
            for r in range(n_grp):
                sr = sums[r * t_sz:(r + 1) * t_sz]
                w = jnp.exp((ls[r] - zns[r]) + (carry_ref[r] + sr[:, :t_sz]))
                v = vp_ref[pl.ds(starts[r], t_sz), :]
                acc_ref[r] += jnp.dot(w.astype(BF16), v, preferred_element_type=F32)
                carry = carry_ref[r] + sr[:, t_sz:]
                carry_ref[r] = carry
                m = carry if m is None else jnp.maximum(m, carry)
            return jnp.max(m)

        def more(c):
            d, m = c
            return jnp.logical_and(d <= i0 + n_grp - 1, m > SB_LOG_WEIGHT_FLOOR)

        lax.while_loop(more, lambda c: (c[0] + 1, step(c[0])), (jnp.int32(SB_EAGER), eager()))
        for r in range(n_grp):
            o_ref[pl.ds(pl.multiple_of((i0 + r) * t_sz, t_sz), t_sz), :] = acc_ref[r].astype(o_ref.dtype)
        return 0

    lax.fori_loop(0, s_len // (t_sz * n_grp), group, 0)


def _sb_attention(u3):
    b, s, _ = u3.shape
    dh = HEAD_DIM
    assert SB_EAGER - 1 <= SB_PAD and s % (SB_T * SB_GROUP) == 0
    row = lax.broadcasted_iota(jnp.int32, (2 * SB_T, 2 * SB_T), 0) % SB_T
    col = lax.broadcasted_iota(jnp.int32, (2 * SB_T, 2 * SB_T), 1)
    uo = jnp.where((col >= SB_T) | (row > col), 1.0, 0.0).astype(BF16)

    def head_spec(col0):
        return pl.BlockSpec((None, s, dh), lambda bi, hi: (bi, 0, col0 // dh + hi))

    padded = pltpu.VMEM((s + SB_PAD * SB_T, dh), BF16)
    return pl.pallas_call(
        _sb_attn_kernel,
        out_shape=jax.ShapeDtypeStruct((b, s, ATTN_WIDTH), BF16),
        grid=(b, ATTN_HEADS),
        in_specs=[head_spec(UCOL_Q), head_spec(UCOL_K), head_spec(UCOL_V),
                  pl.BlockSpec((2 * SB_T, 2 * SB_T), lambda bi, hi: (0, 0))],
        out_specs=pl.BlockSpec((None, s, dh), lambda bi, hi: (bi, 0, hi)),
        scratch_shapes=[padded, padded,
                        pltpu.VMEM((SB_GROUP, SB_T, dh), F32), pltpu.VMEM((SB_GROUP, SB_T, SB_T), F32)],
        compiler_params=_params("parallel", "parallel"),
        name="sb_attention",
    )(u3, u3, u3, uo)


def _out_proj_kernel(x_ref, yc_ref, ya_ref, yl_ref, gc_ref, ga0_ref, ga1_ref, gl_ref,
                     pw_ref, nc_ref, na_ref, nl_ref, w_ref, o_ref):
    def normed(y, n_ref):
        return y * _rms_scale(y) * n_ref[...]

    def gated(yn, gate_ref):
        return (yn * _silu(gate_ref[...].astype(F32))).astype(BF16)

    half = ATTN_WIDTH // 2
    y_conv = jnp.dot(yc_ref[...], pw_ref[...], preferred_element_type=F32)
    ya = normed(ya_ref[...].astype(F32), na_ref)
    parts = [
        (gated(normed(y_conv, nc_ref), gc_ref), 0),
        (gated(ya[:, :half], ga0_ref), CONV_WIDTH),
        (gated(ya[:, half:], ga1_ref), CONV_WIDTH + half),
        (gated(normed(yl_ref[...].astype(F32), nl_ref), gl_ref), CONV_WIDTH + ATTN_WIDTH),
    ]
    acc = x_ref[...]
    for y, r0 in parts:
        acc = acc + jnp.dot(y, w_ref[r0:r0 + y.shape[1], :], preferred_element_type=F32)
    o_ref[...] = acc


def _out_proj(x, yc, ya, yl, u, pw, nc, na, nl, w, layer):
    t, d = x.shape
    tm = TM_OUT_PROJ
    gw = COL_BLOCK
    return pl.pallas_call(
        _out_proj_kernel,
        out_shape=jax.ShapeDtypeStruct((t, d), F32),
        grid=(t // tm,),
        in_specs=[
            pl.BlockSpec((tm, d), lambda i: (i, 0)),
            pl.BlockSpec((tm, CONV_WIDTH), lambda i: (i, 0)),
            pl.BlockSpec((tm, ATTN_WIDTH), lambda i: (i, 0)),
            pl.BlockSpec((tm, LRU_WIDTH), lambda i: (i, 0)),
            pl.BlockSpec((tm, gw), lambda i: (i, UCOL_CGATE // gw)),
            pl.BlockSpec((tm, gw), lambda i: (i, UCOL_AGATE // gw)),
            pl.BlockSpec((tm, gw), lambda i: (i, UCOL_AGATE // gw + 1)),
            pl.BlockSpec((tm, gw), lambda i: (i, UCOL_RGATE // gw)),
            pl.BlockSpec((CONV_WIDTH, CONV_WIDTH), lambda i: (0, 0)),
            pl.BlockSpec((1, CONV_WIDTH), lambda i: (0, 0)),
            pl.BlockSpec((1, ATTN_WIDTH), lambda i: (0, 0)),
            pl.BlockSpec((1, LRU_WIDTH), lambda i: (0, 0)),
            pl.BlockSpec((None, d, d), lambda i: (layer, 0, 0)),
        ],
        out_specs=pl.BlockSpec((tm, d), lambda i: (i, 0)),
        compiler_params=_params("parallel"),
        name="out_proj",
    )(x, yc, ya, yl, u, u, u, u, pw, nc, na, nl, w)


def _xattn_kernel(x_ref, g_ref, wq_ref, k_ref, v_ref, wo_ref, fg_ref, o_ref, *, final_norm):
    x = x_ref[...]
    h = (x * _rms_scale(x) * g_ref[...]).astype(BF16)
    q = jnp.dot(h, wq_ref[...], preferred_element_type=F32).astype(BF16)
    dh = XATTN_WIDTH // XATTN_HEADS
    scale = dh ** -0.5
    heads = []
    for n in range(XATTN_HEADS):
        qh = q[:, n * dh:(n + 1) * dh]
        kh = k_ref[:, n * dh:(n + 1) * dh]
        vh = v_ref[:, n * dh:(n + 1) * dh]
        s = lax.dot_general(qh, kh, (((1,), (1,)), ((), ())), preferred_element_type=F32) * scale
        e = jnp.exp(s - jnp.max(s, axis=-1, keepdims=True))
        p = (e / jnp.sum(e, axis=-1, keepdims=True)).astype(BF16)
        heads.append(jnp.dot(p, vh, preferred_element_type=F32).astype(BF16))
    acc = x + jnp.dot(jnp.concatenate(heads, axis=-1), wo_ref[...], preferred_element_type=F32)
    if final_norm:
        acc = acc * _rms_scale(acc) * fg_ref[...]
    o_ref[...] = acc


def _xattn(x3, g, wq, kv, wo, fg, layer, *, final_norm):
    b, s, d = x3.shape
    m = kv.shape[1]
    tm = TM_XATTN
    xw = XATTN_WIDTH
    return pl.pallas_call(
        functools.partial(_xattn_kernel, final_norm=final_norm),
        out_shape=jax.ShapeDtypeStruct((b, s, d), F32),
        grid=(b, s // tm),
        in_specs=[
            pl.BlockSpec((None, tm, d), lambda bi, ti: (bi, ti, 0)),
            pl.BlockSpec((1, d), lambda bi, ti: (0, 0)),
            pl.BlockSpec((None, d, xw), lambda bi, ti: (layer, 0, 0)),
            pl.BlockSpec((None, m, xw), lambda bi, ti: (bi, 0, 0)),
            pl.BlockSpec((None, m, xw), lambda bi, ti: (bi, 0, 1)),
            pl.BlockSpec((None, xw, d), lambda bi, ti: (layer, 0, 0)),
            pl.BlockSpec((1, d), lambda bi, ti: (0, 0)),
        ],
        out_specs=pl.BlockSpec((None, tm, d), lambda bi, ti: (bi, ti, 0)),
        compiler_params=_params("parallel", "parallel"),
        name="xattn_final" if final_norm else "xattn",
    )(x3, g, wq, kv, kv, wo, fg)


def kernel(x, mem, mix_norm_g, w_in, conv_dw_w, conv_dw_b, conv_ln_g, conv_ln_b, conv_pw_w,
           lru_conv_w, lru_conv_b, lru_wa, lru_ba, lru_wx, lru_bx, lru_lambda,
           out_norm_conv, out_norm_attn, out_norm_lru, w_out,
           xattn_norm_g, mem_norm_g, xattn_wq, xattn_wkv, xattn_wo, final_norm_g):
    b, s, d = x.shape
    m = mem.shape[1]
    depth = w_in.shape[0]
    t = b * s
    row = lambda a: a.reshape(1, -1).astype(F32)

    xt = x.reshape(t, d)
    memt = mem.reshape(b * m, d)
    w_in_b, w_out_b = w_in.astype(BF16), w_out.astype(BF16)
    wq_b, wkv_b, wo_b = xattn_wq.astype(BF16), xattn_wkv.astype(BF16), xattn_wo.astype(BF16)
    for l in range(depth):
        conv_params = (conv_dw_w[l], row(conv_dw_b[l]), row(conv_ln_g[l]), row(conv_ln_b[l]))
        lru_gates = jnp.concatenate([lru_wa[l], lru_wx[l]], axis=-1).astype(BF16)
        lru_params = (lru_conv_w[l], row(lru_conv_b[l]), lru_gates, row(lru_ba[l]), row(lru_bx[l]),
                      row(lru_lambda[l]))
        u, conv_act, y_lru = _mixer_in(xt, row(mix_norm_g[l]), w_in_b, l, conv_params, lru_params, seq_len=s)
        y_attn = _sb_attention(u.reshape(b, s, U_WIDTH))
        xt = _out_proj(xt, conv_act, y_attn.reshape(t, -1), y_lru, u, conv_pw_w[l].astype(BF16),
                       row(out_norm_conv[l]), row(out_norm_attn[l]), row(out_norm_lru[l]), w_out_b, l)
        kv = _norm_matmul(memt, row(mem_norm_g[l]), wkv_b, l, tm=256, tn=1024,
                          name="mem_kv").reshape(b, m, 2 * XATTN_WIDTH)
        xt = _xattn(xt.reshape(b, s, d), row(xattn_norm_g[l]), wq_b, kv, wo_b, row(final_norm_g), l,
                    final_norm=(l == depth - 1)).reshape(t, d)
    return xt.reshape(b, s, d)
```

```python
import functools

import jax
import jax.numpy as jnp
from jax import lax
from jax.experimental import pallas as pl
from jax.experimental.pallas import tpu as pltpu

F32 = jnp.float32
BF16 = jnp.bfloat16

D_MODEL = 2048
CONV_WIDTH = 512
CONV_KERNEL = 31
HEAD_DIM = 128
ATTN_WIDTH = 1024
ATTN_HEADS = 8
LRU_WIDTH = 512
LRU_HEADS = 4
LRU_CONV_KERNEL = 4
LRU_C = 8.0
XATTN_HEADS = 4
XATTN_WIDTH = 512
IN_WIDTH = 3 * CONV_WIDTH + 4 * ATTN_WIDTH + 2 * LRU_WIDTH

COL_BLOCK = 512
N_COL_BLOCKS = IN_WIDTH // COL_BLOCK
W_BLOCK_LRU_X = 11
N_BRANCH_BLOCKS = 3
U_WIDTH = IN_WIDTH - N_BRANCH_BLOCKS * COL_BLOCK
UCOL_CGATE, UCOL_Q, UCOL_K, UCOL_V, UCOL_AGATE, UCOL_RGATE = 0, 512, 1536, 2560, 3584, 4608

VMEM_LIMIT_BYTES = 56 * 1024 * 1024
SUBLANES = 8
LANES = 128

TM_MIXER_IN = 1024
TM_OUT_PROJ = 512
TM_XATTN = 1024

RMS_EPS = 1e-6
LN_EPS = 1e-5


def _params(*sem):
    return pltpu.CompilerParams(dimension_semantics=sem, vmem_limit_bytes=VMEM_LIMIT_BYTES)


def _rms_scale(x):
    return lax.rsqrt(jnp.mean(x * x, axis=-1, keepdims=True) + RMS_EPS)


def _softplus(x):
    return jnp.maximum(x, 0.0) + jnp.log(1.0 + jnp.exp(-jnp.abs(x)))


def _silu(x):
    return x * jax.nn.sigmoid(x)


def _norm_matmul_kernel(x_ref, g_ref, w_ref, o_ref, h_ref):
    @pl.when(pl.program_id(1) == 0)
    def _():
        x = x_ref[...]
        h_ref[...] = (x * _rms_scale(x) * g_ref[...]).astype(BF16)

    o_ref[...] = jnp.dot(h_ref[...], w_ref[...], preferred_element_type=F32).astype(o_ref.dtype)


def _norm_matmul(x, g, w, layer, *, tm, tn, name):
    t, d = x.shape
    n = w.shape[2]
    return pl.pallas_call(
        _norm_matmul_kernel,
        out_shape=jax.ShapeDtypeStruct((t, n), BF16),
        grid=(t // tm, n // tn),
        in_specs=[
            pl.BlockSpec((tm, d), lambda i, j: (i, 0)),
            pl.BlockSpec((1, d), lambda i, j: (0, 0)),
            pl.BlockSpec((None, d, tn), lambda i, j: (layer, 0, j)),
        ],
        out_specs=pl.BlockSpec((tm, tn), lambda i, j: (i, j)),
        scratch_shapes=[pltpu.VMEM((tm, d), BF16)],
        compiler_params=_params("parallel", "arbitrary"),
        name=name,
    )(x, g, w)


CONV_HALO = 32
CONV_ROW_CHUNK = 64
BRANCH_STEP_ROWS = 128
LRU_HALO = SUBLANES


def _depthwise_conv_rows(buf_ref, dww_ref, dwb_ref, r0):
    base = CONV_HALO - (CONV_KERNEL - 1)
    rows = CONV_ROW_CHUNK + SUBLANES
    n_m = (base + CONV_KERNEL - 1) // SUBLANES + 1
    out = []
    for lb in range(CONV_WIDTH // LANES):
        lanes = slice(lb * LANES, (lb + 1) * LANES)
        window = buf_ref[pl.ds(r0, rows + SUBLANES * (n_m - 1)), lanes]
        a = None
        for rho in reversed(range(SUBLANES)):
            q = None
            for m in range(n_m):
                k = SUBLANES * m + rho - base
                if 0 <= k < CONV_KERNEL:
                    term = dww_ref[k:k + 1, lanes] * window[SUBLANES * m:SUBLANES * m + rows]
                    q = term if q is None else q + term
            a = q if a is None else q + pltpu.roll(a, rows - 1, 0)
        out.append(a[:CONV_ROW_CHUNK] + dwb_ref[:, lanes])
    return jnp.concatenate(out, axis=1)


def _conv_branch_rows(cbuf_ref, dww_ref, dwb_ref, lng_ref, lnb_ref, r0):
    u = jnp.concatenate([_depthwise_conv_rows(cbuf_ref, dww_ref, dwb_ref, r0 + c * CONV_ROW_CHUNK)
                         for c in range(BRANCH_STEP_ROWS // CONV_ROW_CHUNK)], axis=0)
    mu = jnp.mean(u, axis=-1, keepdims=True)
    uc = u - mu
    var = jnp.mean(uc * uc, axis=-1, keepdims=True)
    y = uc * lax.rsqrt(var + LN_EPS) * lng_ref[...] + lnb_ref[...]
    return _silu(y).astype(BF16)


def _lru_branch_rows(rbuf_ref, cw_ref, cb_ref, wg_ref, ba_ref, bx_ref, lam_ref, hstate_ref, a_ref, b_ref,
                     yl_ref, r0):
    w = LRU_WIDTH
    n = BRANCH_STEP_ROWS
    window = rbuf_ref[pl.ds(r0, n + LRU_HALO), :]
    xc = jnp.broadcast_to(cb_ref[...], (n, w))
    for k in range(LRU_CONV_KERNEL):
        shift = LRU_HALO - (LRU_CONV_KERNEL - 1) + k
        if shift % SUBLANES == 0:
            tap = window[shift:shift + n]
        else:
            tap = pltpu.roll(window, n + LRU_HALO - shift, 0)[:n]
        xc = xc + cw_ref[k:k + 1, :] * tap

    xcb = xc.astype(BF16)
    hd = w // LRU_HEADS
    pre = [jnp.dot(xcb[:, hix * hd:(hix + 1) * hd], wg_ref[hix], preferred_element_type=F32)
           for hix in range(LRU_HEADS)]
    r = jax.nn.sigmoid(jnp.concatenate([p[:, :hd] for p in pre], axis=-1) + ba_ref[...])
    gate = jax.nn.sigmoid(jnp.concatenate([p[:, hd:] for p in pre], axis=-1) + bx_ref[...])
    log_a = (-LRU_C) * r * _softplus(-lam_ref[...])
    a = jnp.exp(log_a)
    a_ref[...] = a
    b_ref[...] = jnp.sqrt(jnp.tanh(-log_a) * (1.0 + a * a)) * (gate * xc)

    row = lax.broadcasted_iota(jnp.int32, (SUBLANES, w), 0)
    h_prev = hstate_ref[...]
    for g in range(n // SUBLANES):
        rows = slice(g * SUBLANES, (g + 1) * SUBLANES)
        ag = a_ref[rows, :]
        bg = b_ref[rows, :]
        d = 1
        while d < SUBLANES:
            a_sh = pltpu.roll(ag, d, 0)
            b_sh = pltpu.roll(bg, d, 0)
            m = row >= d
            bg = jnp.where(m, ag * b_sh + bg, bg)
            ag = jnp.where(m, ag * a_sh, ag)
            d *= 2
        h = ag * h_prev + bg
        b_ref[rows, :] = h
        h_prev = jnp.broadcast_to(h[SUBLANES - 1:SUBLANES, :], (SUBLANES, w))
    hstate_ref[...] = h_prev
    yl_ref[pl.ds(r0, n), :] = b_ref[...].astype(yl_ref.dtype)


def _mixer_in_kernel(x0_ref, x1_ref, x2_ref, x3_ref, g_ref, w_ref,
                     dww_ref, dwb_ref, lng_ref, lnb_ref,
                     cw_ref, cb_ref, wg_ref, ba_ref, bx_ref, lam_ref,
                     u_ref, yc_ref, yl_ref,
                     h_ref, val_ref, cbuf_ref, rbuf_ref, hstate_ref, a_ref, b_ref, *, tiles_per_seq):
    i = pl.program_id(0)
    j = pl.program_id(1)
    x_refs = (x0_ref, x1_ref, x2_ref, x3_ref)
    tm = x0_ref.shape[0]
    first_branch_step = N_BRANCH_BLOCKS
    n_branch_steps = tm // BRANCH_STEP_ROWS

    def project():
        return jnp.dot(h_ref[...], w_ref[...], preferred_element_type=F32)

    @pl.when(j == 0)
    def _():
        @pl.when(i % tiles_per_seq == 0)
        def _():
            cbuf_ref[0:CONV_HALO, :] = jnp.zeros((CONV_HALO, CONV_WIDTH), F32)
            rbuf_ref[0:LRU_HALO, :] = jnp.zeros((LRU_HALO, LRU_WIDTH), F32)
            hstate_ref[...] = jnp.zeros_like(hstate_ref)
            cbuf_ref[CONV_HALO + tm:CONV_HALO + tm + SUBLANES, :] = jnp.zeros((SUBLANES, CONV_WIDTH), F32)

        @pl.when(i % tiles_per_seq != 0)
        def _():
            cbuf_ref[0:CONV_HALO, :] = cbuf_ref[tm:tm + CONV_HALO, :]
            rbuf_ref[0:LRU_HALO, :] = rbuf_ref[tm:tm + LRU_HALO, :]

        d = h_ref.shape[1]
        cw = d // len(x_refs)
        sq = None
        for xr in x_refs:
            xc = xr[...]
            part = jnp.sum(xc * xc, axis=-1, keepdims=True)
            sq = part if sq is None else sq + part
        inv = lax.rsqrt(sq * (1.0 / d) + RMS_EPS)
        for c, xr in enumerate(x_refs):
            h_ref[:, c * cw:(c + 1) * cw] = (xr[...] * inv * g_ref[:, c * cw:(c + 1) * cw]).astype(BF16)
        val_ref[...] = project()

    @pl.when(j == 1)
    def _():
        cbuf_ref[CONV_HALO:CONV_HALO + tm, :] = val_ref[...] * jax.nn.sigmoid(project())

    @pl.when(j == 2)
    def _():
        rbuf_ref[LRU_HALO:LRU_HALO + tm, :] = project()

    @pl.when(jnp.logical_and(j >= first_branch_step, j < first_branch_step + n_branch_steps))
    def _():
        r0 = pl.multiple_of((j - first_branch_step) * BRANCH_STEP_ROWS, BRANCH_STEP_ROWS)
        _lru_branch_rows(rbuf_ref, cw_ref, cb_ref, wg_ref, ba_ref, bx_ref, lam_ref, hstate_ref, a_ref, b_ref,
                         yl_ref, r0)
        yc_ref[pl.ds(r0, BRANCH_STEP_ROWS), :] = _conv_branch_rows(cbuf_ref, dww_ref, dwb_ref, lng_ref, lnb_ref, r0)
        u_ref[...] = project().astype(u_ref.dtype)

    @pl.when(j >= first_branch_step + n_branch_steps)
    def _():
        u_ref[...] = project().astype(u_ref.dtype)


X_CHUNKS = 4


def _mixer_in(x, g, w, layer, conv_params, lru_params, *, seq_len):
    t, d = x.shape
    tm = TM_MIXER_IN
    n_tiles = t // tm
    assert seq_len % tm == 0 and tm % BRANCH_STEP_ROWS == 0 and BRANCH_STEP_ROWS % CONV_ROW_CHUNK == 0
    assert N_BRANCH_BLOCKS + tm // BRANCH_STEP_ROWS <= N_COL_BLOCKS and d % X_CHUNKS == 0
    dww, dwb, lng, lnb = conv_params
    cw, cb, wg, ba, bx, lam = lru_params
    hd = LRU_WIDTH // LRU_HEADS

    def w_block(i, j):
        blk = jnp.where(j == 2, W_BLOCK_LRU_X, jnp.where(jnp.logical_and(j > 2, j <= W_BLOCK_LRU_X), j - 1, j))
        return (layer, 0, blk)

    def x_chunk(c):
        def index(i, j):
            ahead = (j >= N_COL_BLOCKS - X_CHUNKS + c).astype(jnp.int32)
            return (jnp.minimum(i + ahead, n_tiles - 1), c)
        return pl.BlockSpec((tm, d // X_CHUNKS), index)

    const = lambda shape: pl.BlockSpec(shape, lambda i, j: (0,) * len(shape))
    return pl.pallas_call(
        functools.partial(_mixer_in_kernel, tiles_per_seq=seq_len // tm),
        out_shape=(jax.ShapeDtypeStruct((t, U_WIDTH), BF16),
                   jax.ShapeDtypeStruct((t, CONV_WIDTH), BF16),
                   jax.ShapeDtypeStruct((t, LRU_WIDTH), BF16)),
        grid=(n_tiles, N_COL_BLOCKS),
        in_specs=[
            *[x_chunk(c) for c in range(X_CHUNKS)],
            const((1, d)),
            pl.BlockSpec((None, d, COL_BLOCK), w_block),
            const((CONV_KERNEL, CONV_WIDTH)), const((1, CONV_WIDTH)), const((1, CONV_WIDTH)),
            const((1, CONV_WIDTH)),
            const((LRU_CONV_KERNEL, LRU_WIDTH)), const((1, LRU_WIDTH)),
            const((LRU_HEADS, hd, 2 * hd)), const((1, LRU_WIDTH)), const((1, LRU_WIDTH)), const((1, LRU_WIDTH)),
        ],
        out_specs=(
            pl.BlockSpec((tm, COL_BLOCK), lambda i, j: (i, jnp.maximum(j - N_BRANCH_BLOCKS, 0))),
            pl.BlockSpec((tm, CONV_WIDTH), lambda i, j: (i, 0)),
            pl.BlockSpec((tm, LRU_WIDTH), lambda i, j: (i, 0)),
        ),
        scratch_shapes=[
            pltpu.VMEM((tm, d), BF16),
            pltpu.VMEM((tm, CONV_WIDTH), F32),
            pltpu.VMEM((CONV_HALO + tm + SUBLANES, CONV_WIDTH), F32),
            pltpu.VMEM((LRU_HALO + tm, LRU_WIDTH), F32),
            pltpu.VMEM((SUBLANES, LRU_WIDTH), F32),
            pltpu.VMEM((BRANCH_STEP_ROWS, LRU_WIDTH), F32),
            pltpu.VMEM((BRANCH_STEP_ROWS, LRU_WIDTH), F32),
        ],
        compiler_params=_params("arbitrary", "arbitrary"),
        name="mixer_in",
    )(*([x] * X_CHUNKS), g, w, dww, dwb, lng, lnb, cw, cb, wg, ba, bx, lam)


SB_T = 128
SB_GROUP = 32
SB_EAGER = 3
SB_NARROW = 64
SB_PAD = SB_GROUP - 1
SB_LOG_WEIGHT_FLOOR = -104.0


def _sb_split(log_1mb):
    hi = log_1mb.astype(BF16)
    lo = (log_1mb - hi.astype(F32)).astype(BF16)
    return jnp.concatenate([hi, lo], axis=1)


def _sb_attn_kernel(q_ref, k_ref, v_ref, uo_ref, o_ref, kp_ref, vp_ref, acc_ref, carry_ref):
    s_len = q_ref.shape[0]
    scale = HEAD_DIM ** -0.5
    t_sz, n_grp = SB_T, SB_GROUP
    pad_rows = SB_PAD * t_sz
    kp_ref[0:pad_rows, :] = jnp.zeros((pad_rows, HEAD_DIM), BF16)
    vp_ref[0:pad_rows, :] = jnp.zeros((pad_rows, HEAD_DIM), BF16)
    kp_ref[pad_rows:pad_rows + s_len, :] = k_ref[...]
    vp_ref[pad_rows:pad_rows + s_len, :] = v_ref[...]
    causal = (lax.broadcasted_iota(jnp.int32, (t_sz, t_sz), 1)
              < lax.broadcasted_iota(jnp.int32, (t_sz, t_sz), 0))

    def q_tile(i):
        return q_ref[pl.ds(pl.multiple_of(i * t_sz, t_sz), t_sz), :]

    def neg_scores(q, k):
        zn = lax.dot_general(q, k, (((1,), (1,)), ((), ())), preferred_element_type=F32) * (-scale)
        log_1mb = jnp.minimum(zn, 0.0) - jnp.log(1.0 + jnp.exp(-jnp.abs(zn)))
        return zn, log_1mb

    def group(g, _):
        i0 = g * n_grp

        def eager():
            n_lo = SB_NARROW
            zns, ls, zn0s, l0s, starts, lhs = [], [], [], [], [], []
            for r in range(n_grp):
                start = pl.multiple_of((i0 + r + SB_PAD - 2) * t_sz, t_sz)
                q = q_tile(i0 + r)
                zn, log_1mb = neg_scores(q, kp_ref[pl.ds(start + t_sz, 2 * t_sz), :])
                zn0, log_1mb0 = neg_scores(q[:n_lo], kp_ref[pl.ds(start, t_sz), :])
                lhs.append(jnp.concatenate([
                    _sb_split(log_1mb0),
                    _sb_split(log_1mb[:, :t_sz]),
                    _sb_split(jnp.where(causal, log_1mb[:, t_sz:], 0.0)),
                ], axis=0))
                zns.append(zn)
                ls.append(log_1mb)
                zn0s.append(zn0)
                l0s.append(log_1mb0)
                starts.append(start)
            sums = [jnp.dot(x, uo_ref[...], preferred_element_type=F32) for x in lhs]
            m_lo = m_hi = None
            for r in range(n_grp):
                s0, s1, s2 = sums[r][:n_lo], sums[r][n_lo:n_lo + t_sz], sums[r][n_lo + t_sz:]
                zn, log_1mb = zns[r], ls[r]
                w2 = jnp.where(causal, jnp.exp((log_1mb[:, t_sz:] - zn[:, t_sz:]) + s2[:, :t_sz]), 0.0)
                carry = s2[:, t_sz:]
                w1 = jnp.exp((log_1mb[:, :t_sz] - zn[:, :t_sz]) + (carry + s1[:, :t_sz]))
                carry = carry + s1[:, t_sz:]
                w0 = jnp.exp((l0s[r] - zn0s[r]) + (carry[:n_lo] + s0[:, :t_sz]))
                carry_lo = carry[:n_lo] + s0[:, t_sz:]
                w12 = jnp.concatenate([w1.astype(BF16), w2.astype(BF16)], axis=1)
                acc = jnp.dot(w12, vp_ref[pl.ds(starts[r] + t_sz, 2 * t_sz), :], preferred_element_type=F32)
                acc_lo = jnp.dot(w0.astype(BF16), vp_ref[pl.ds(starts[r], t_sz), :], preferred_element_type=F32)
                acc_ref[r, 0:n_lo, :] = acc[:n_lo] + acc_lo
                acc_ref[r, n_lo:t_sz, :] = acc[n_lo:]
                carry_ref[r, 0:n_lo, :] = carry_lo
                carry_ref[r, n_lo:t_sz, :] = carry[n_lo:]
                m_lo = carry_lo if m_lo is None else jnp.maximum(m_lo, carry_lo)
                m_hi = carry[n_lo:] if m_hi is None else jnp.maximum(m_hi, carry[n_lo:])
            return jnp.max(m_lo), jnp.max(m_hi)

        def rest_of_third_block():
            n_lo, n_hi = SB_NARROW, t_sz - SB_NARROW
            zns, ls, starts, lhs = [], [], [], []
            for r in range(n_grp):
                start = pl.multiple_of((i0 + r + SB_PAD - 2) * t_sz, t_sz)
                zn, log_1mb = neg_scores(q_tile(i0 + r)[n_lo:], kp_ref[pl.ds(start, t_sz), :])
                lhs.append(_sb_split(log_1mb))
                zns.append(zn)
                ls.append(log_1mb)
                starts.append(start)
            sums = jnp.dot(jnp.concatenate(lhs, axis=0), uo_ref[...], preferred_element_type=F32)
            m = None
            for r in range(n_grp):
                sr = sums[r * n_hi:(r + 1) * n_hi]
                carry = carry_ref[r, n_lo:t_sz, :]
                w = jnp.exp((ls[r] - zns[r]) + (carry + sr[:, :t_sz]))
                v = vp_ref[pl.ds(starts[r], t_sz), :]
                acc_ref[r, n_lo:t_sz, :] += jnp.dot(w.astype(BF16), v, preferred_element_type=F32)
                carry = carry + sr[:, t_sz:]
                carry_ref[r, n_lo:t_sz, :] = carry
                m = carry if m is None else jnp.maximum(m, carry)
            return jnp.max(m)

        def step(d):
            zns, ls, starts, lhs = [], [], [], []
            for r in range(n_grp):
                start = pl.multiple_of((i0 + r + SB_PAD - d) * t_sz, t_sz)
                zn, log_1mb = neg_scores(q_tile(i0 + r), kp_ref[pl.ds(start, t_sz), :])
                lhs.append(_sb_split(log_1mb))
                zns.append(zn)
                ls.append(log_1mb)
                starts.append(start)
            sums = jnp.dot(jnp.concatenate(lhs, axis=0), uo_ref[...], preferred_element_type=F32)
            m = None
            for r in range(n_grp):
                sr = sums[r * t_sz:(r + 1) * t_sz]
                w = jnp.exp((ls[r] - zns[r]) + (carry_ref[r] + sr[:, :t_sz]))
                v = vp_ref[pl.ds(starts[r], t_sz), :]
                acc_ref[r] += jnp.dot(w.astype(BF16), v, preferred_element_type=F32)
                carry = carry_ref[r] + sr[:, t_sz:]
                carry_ref[r] = carry
                m = carry if m is None else jnp.maximum(m, carry)
            return jnp.max(m)

        def more(c):
            d, m = c
            return jnp.logical_and(d <= i0 + n_grp - 1, m > SB_LOG_WEIGHT_FLOOR)

        m_lo, m_hi = eager()
        m = lax.cond(m_hi > SB_LOG_WEIGHT_FLOOR,
                     lambda: jnp.maximum(m_lo, rest_of_third_block()), lambda: m_lo)
        lax.while_loop(more, lambda c: (c[0] + 1, step(c[0])), (jnp.int32(SB_EAGER), m))
        for r in range(n_grp):
            o_ref[pl.ds(pl.multiple_of((i0 + r) * t_sz, t_sz), t_sz), :] = acc_ref[r].astype(o_ref.dtype)
        return 0

    lax.fori_loop(0, s_len // (t_sz * n_grp), group, 0)


def _sb_attention(u3):
    b, s, _ = u3.shape
    dh = HEAD_DIM
    assert SB_EAGER - 1 <= SB_PAD and s % (SB_T * SB_GROUP) == 0
    row = lax.broadcasted_iota(jnp.int32, (2 * SB_T, 2 * SB_T), 0) % SB_T
    col = lax.broadcasted_iota(jnp.int32, (2 * SB_T, 2 * SB_T), 1)
    uo = jnp.where((col >= SB_T) | (row > col), 1.0, 0.0).astype(BF16)

    def head_spec(col0):
        return pl.BlockSpec((None, s, dh), lambda bi, hi: (bi, 0, col0 // dh + hi))

    padded = pltpu.VMEM((s + SB_PAD * SB_T, dh), BF16)
    return pl.pallas_call(
        _sb_attn_kernel,
        out_shape=jax.ShapeDtypeStruct((b, s, ATTN_WIDTH), BF16),
        grid=(b, ATTN_HEADS),
        in_specs=[head_spec(UCOL_Q), head_spec(UCOL_K), head_spec(UCOL_V),
                  pl.BlockSpec((2 * SB_T, 2 * SB_T), lambda bi, hi: (0, 0))],
        out_specs=pl.BlockSpec((None, s, dh), lambda bi, hi: (bi, 0, hi)),
        scratch_shapes=[padded, padded,
                        pltpu.VMEM((SB_GROUP, SB_T, dh), F32), pltpu.VMEM((SB_GROUP, SB_T, SB_T), F32)],
        compiler_params=_params("parallel", "parallel"),
        name="sb_attention",
    )(u3, u3, u3, uo)


def _out_proj_kernel(x_ref, yc_ref, ya_ref, yl_ref, gc_ref, ga0_ref, ga1_ref, gl_ref,
                     pw_ref, nc_ref, na_ref, nl_ref, w_ref, o_ref):
    def normed(y, n_ref):
        return y * _rms_scale(y) * n_ref[...]

    def gated(yn, gate_ref):
        return (yn * _silu(gate_ref[...].astype(F32))).astype(BF16)

    half = ATTN_WIDTH // 2
    y_conv = jnp.dot(yc_ref[...], pw_ref[...], preferred_element_type=F32)
    ya = normed(ya_ref[...].astype(F32), na_ref)
    parts = [
        (gated(normed(y_conv, nc_ref), gc_ref), 0),
        (gated(ya[:, :half], ga0_ref), CONV_WIDTH),
        (gated(ya[:, half:], ga1_ref), CONV_WIDTH + half),
        (gated(normed(yl_ref[...].astype(F32), nl_ref), gl_ref), CONV_WIDTH + ATTN_WIDTH),
    ]
    acc = x_ref[...]
    for y, r0 in parts:
        acc = acc + jnp.dot(y, w_ref[r0:r0 + y.shape[1], :], preferred_element_type=F32)
    o_ref[...] = acc


def _out_proj(x, yc, ya, yl, u, pw, nc, na, nl, w, layer):
    t, d = x.shape
    tm = TM_OUT_PROJ
    gw = COL_BLOCK
    return pl.pallas_call(
        _out_proj_kernel,
        out_shape=jax.ShapeDtypeStruct((t, d), F32),
        grid=(t // tm,),
        in_specs=[
            pl.BlockSpec((tm, d), lambda i: (i, 0)),
            pl.BlockSpec((tm, CONV_WIDTH), lambda i: (i, 0)),
            pl.BlockSpec((tm, ATTN_WIDTH), lambda i: (i, 0)),
            pl.BlockSpec((tm, LRU_WIDTH), lambda i: (i, 0)),
            pl.BlockSpec((tm, gw), lambda i: (i, UCOL_CGATE // gw)),
            pl.BlockSpec((tm, gw), lambda i: (i, UCOL_AGATE // gw)),
            pl.BlockSpec((tm, gw), lambda i: (i, UCOL_AGATE // gw + 1)),
            pl.BlockSpec((tm, gw), lambda i: (i, UCOL_RGATE // gw)),
            pl.BlockSpec((CONV_WIDTH, CONV_WIDTH), lambda i: (0, 0)),
            pl.BlockSpec((1, CONV_WIDTH), lambda i: (0, 0)),
            pl.BlockSpec((1, ATTN_WIDTH), lambda i: (0, 0)),
            pl.BlockSpec((1, LRU_WIDTH), lambda i: (0, 0)),
            pl.BlockSpec((None, d, d), lambda i: (layer, 0, 0)),
        ],
        out_specs=pl.BlockSpec((tm, d), lambda i: (i, 0)),
        compiler_params=_params("parallel"),
        name="out_proj",
    )(x, yc, ya, yl, u, u, u, u, pw, nc, na, nl, w)


def _xattn_kernel(x_ref, g_ref, wq_ref, k_ref, v_ref, wo_ref, fg_ref, o_ref, *, final_norm):
    x = x_ref[...]
    h = (x * _rms_scale(x) * g_ref[...]).astype(BF16)
    q = jnp.dot(h, wq_ref[...], preferred_element_type=F32).astype(BF16)
    dh = XATTN_WIDTH // XATTN_HEADS
    scale = dh ** -0.5
    heads = []
    for n in range(XATTN_HEADS):
        qh = q[:, n * dh:(n + 1) * dh]
        kh = k_ref[:, n * dh:(n + 1) * dh]
        vh = v_ref[:, n * dh:(n + 1) * dh]
        s = lax.dot_general(qh, kh, (((1,), (1,)), ((), ())), preferred_element_type=F32) * scale
        e = jnp.exp(s - jnp.max(s, axis=-1, keepdims=True))
        p = (e / jnp.sum(e, axis=-1, keepdims=True)).astype(BF16)
        heads.append(jnp.dot(p, vh, preferred_element_type=F32).astype(BF16))
    acc = x + jnp.dot(jnp.concatenate(heads, axis=-1), wo_ref[...], preferred_element_type=F32)
    if final_norm:
        acc = acc * _rms_scale(acc) * fg_ref[...]
    o_ref[...] = acc


def _xattn(x3, g, wq, kv, wo, fg, layer, *, final_norm):
    b, s, d = x3.shape
    m = kv.shape[1]
    tm = TM_XATTN
    xw = XATTN_WIDTH
    return pl.pallas_call(
        functools.partial(_xattn_kernel, final_norm=final_norm),
        out_shape=jax.ShapeDtypeStruct((b, s, d), F32),
        grid=(b, s // tm),
        in_specs=[
            pl.BlockSpec((None, tm, d), lambda bi, ti: (bi, ti, 0)),
            pl.BlockSpec((1, d), lambda bi, ti: (0, 0)),
            pl.BlockSpec((None, d, xw), lambda bi, ti: (layer, 0, 0)),
            pl.BlockSpec((None, m, xw), lambda bi, ti: (bi, 0, 0)),
            pl.BlockSpec((None, m, xw), lambda bi, ti: (bi, 0, 1)),
            pl.BlockSpec((None, xw, d), lambda bi, ti: (layer, 0, 0)),
            pl.BlockSpec((1, d), lambda bi, ti: (0, 0)),
        ],
        out_specs=pl.BlockSpec((None, tm, d), lambda bi, ti: (bi, ti, 0)),
        compiler_params=_params("parallel", "parallel"),
        name="xattn_final" if final_norm else "xattn",
    )(x3, g, wq, kv, kv, wo, fg)


def kernel(x, mem, mix_norm_g, w_in, conv_dw_w, conv_dw_b, conv_ln_g, conv_ln_b, conv_pw_w,
           lru_conv_w, lru_conv_b, lru_wa, lru_ba, lru_wx, lru_bx, lru_lambda,
           out_norm_conv, out_norm_attn, out_norm_lru, w_out,
           xattn_norm_g, mem_norm_g, xattn_wq, xattn_wkv, xattn_wo, final_norm_g):
    b, s, d = x.shape
    m = mem.shape[1]
    depth = w_in.shape[0]
    t = b * s
    row = lambda a: a.reshape(1, -1).astype(F32)

    xt = x.reshape(t, d)
    memt = mem.reshape(b * m, d)
    w_in_b, w_out_b = w_in.astype(BF16), w_out.astype(BF16)
    wq_b, wkv_b, wo_b = xattn_wq.astype(BF16), xattn_wkv.astype(BF16), xattn_wo.astype(BF16)
    for l in range(depth):
        conv_params = (conv_dw_w[l], row(conv_dw_b[l]), row(conv_ln_g[l]), row(conv_ln_b[l]))
        lru_gates = jnp.concatenate([lru_wa[l], lru_wx[l]], axis=-1).astype(BF16)
        lru_params = (lru_conv_w[l], row(lru_conv_b[l]), lru_gates, row(lru_ba[l]), row(lru_bx[l]),
                      row(lru_lambda[l]))
        u, conv_act, y_lru = _mixer_in(xt, row(mix_norm_g[l]), w_in_b, l, conv_params, lru_params, seq_len=s)
        y_attn = _sb_attention(u.reshape(b, s, U_WIDTH))
        xt = _out_proj(xt, conv_act, y_attn.reshape(t, -1), y_lru, u, conv_pw_w[l].astype(BF16),
                       row(out_norm_conv[l]), row(out_norm_attn[l]), row(out_norm_lru[l]), w_out_b, l)
        kv = _norm_matmul(memt, row(mem_norm_g[l]), wkv_b, l, tm=256, tn=1024,
                          name="mem_kv").reshape(b, m, 2 * XATTN_WIDTH)
        xt = _xattn(xt.reshape(b, s, d), row(xattn_norm_g[l]), wq_b, kv, wo_b, row(final_norm_g), l,
                    final_norm=(l == depth - 1)).reshape(t, d)
    return xt.reshape(b, s, d)
```

```python
import functools

import jax
import jax.numpy as jnp
from jax import lax
from jax.experimental import pallas as pl
from jax.experimental.pallas import tpu as pltpu

F32 = jnp.float32
BF16 = jnp.bfloat16

D_MODEL = 2048
CONV_WIDTH = 512
CONV_KERNEL = 31
HEAD_DIM = 128
ATTN_WIDTH = 1024
ATTN_HEADS = 8
LRU_WIDTH = 512
LRU_HEADS = 4
LRU_CONV_KERNEL = 4
LRU_C = 8.0
XATTN_HEADS = 4
XATTN_WIDTH = 512
IN_WIDTH = 3 * CONV_WIDTH + 4 * ATTN_WIDTH + 2 * LRU_WIDTH

COL_BLOCK = 512
N_COL_BLOCKS = IN_WIDTH // COL_BLOCK
W_BLOCK_LRU_X = 11
N_BRANCH_BLOCKS = 3
U_WIDTH = IN_WIDTH - N_BRANCH_BLOCKS * COL_BLOCK
UCOL_CGATE, UCOL_Q, UCOL_K, UCOL_V, UCOL_AGATE, UCOL_RGATE = 0, 512, 1536, 2560, 3584, 4608

VMEM_LIMIT_BYTES = 56 * 1024 * 1024
SUBLANES = 8
LANES = 128

TM_MIXER_IN = 1024
TM_OUT_PROJ = 512
TM_XATTN = 1024

RMS_EPS = 1e-6
LN_EPS = 1e-5


def _params(*sem):
    return pltpu.CompilerParams(dimension_semantics=sem, vmem_limit_bytes=VMEM_LIMIT_BYTES)


def _rms_scale(x):
    return lax.rsqrt(jnp.mean(x * x, axis=-1, keepdims=True) + RMS_EPS)


def _softplus(x):
    return jnp.maximum(x, 0.0) + jnp.log(1.0 + jnp.exp(-jnp.abs(x)))


def _silu(x):
    return x * jax.nn.sigmoid(x)


def _norm_matmul_kernel(x_ref, g_ref, w_ref, o_ref, h_ref):
    @pl.when(pl.program_id(1) == 0)
    def _():
        x = x_ref[...]
        h_ref[...] = (x * _rms_scale(x) * g_ref[...]).astype(BF16)

    o_ref[...] = jnp.dot(h_ref[...], w_ref[...], preferred_element_type=F32).astype(o_ref.dtype)


def _norm_matmul(x, g, w, layer, *, tm, tn, name):
    t, d = x.shape
    n = w.shape[2]
    return pl.pallas_call(
        _norm_matmul_kernel,
        out_shape=jax.ShapeDtypeStruct((t, n), BF16),
        grid=(t // tm, n // tn),
        in_specs=[
            pl.BlockSpec((tm, d), lambda i, j: (i, 0)),
            pl.BlockSpec((1, d), lambda i, j: (0, 0)),
            pl.BlockSpec((None, d, tn), lambda i, j: (layer, 0, j)),
        ],
        out_specs=pl.BlockSpec((tm, tn), lambda i, j: (i, j)),
        scratch_shapes=[pltpu.VMEM((tm, d), BF16)],
        compiler_params=_params("parallel", "arbitrary"),
        name=name,
    )(x, g, w)


CONV_HALO = 32
CONV_ROW_CHUNK = 64
BRANCH_STEP_ROWS = 128
LRU_HALO = SUBLANES


def _depthwise_conv_rows(buf_ref, dww_ref, dwb_ref, r0):
    base = CONV_HALO - (CONV_KERNEL - 1)
    rows = CONV_ROW_CHUNK + SUBLANES
    n_m = (base + CONV_KERNEL - 1) // SUBLANES + 1
    out = []
    for lb in range(CONV_WIDTH // LANES):
        lanes = slice(lb * LANES, (lb + 1) * LANES)
        window = buf_ref[pl.ds(r0, rows + SUBLANES * (n_m - 1)), lanes]
        a = None
        for rho in reversed(range(SUBLANES)):
            q = None
            for m in range(n_m):
                k = SUBLANES * m + rho - base
                if 0 <= k < CONV_KERNEL:
                    term = dww_ref[k:k + 1, lanes] * window[SUBLANES * m:SUBLANES * m + rows]
                    q = term if q is None else q + term
            a = q if a is None else q + pltpu.roll(a, rows - 1, 0)
        out.append(a[:CONV_ROW_CHUNK] + dwb_ref[:, lanes])
    return jnp.concatenate(out, axis=1)


def _conv_branch_rows(cbuf_ref, dww_ref, dwb_ref, lng_ref, lnb_ref, r0):
    u = jnp.concatenate([_depthwise_conv_rows(cbuf_ref, dww_ref, dwb_ref, r0 + c * CONV_ROW_CHUNK)
                         for c in range(BRANCH_STEP_ROWS // CONV_ROW_CHUNK)], axis=0)
    mu = jnp.mean(u, axis=-1, keepdims=True)
    uc = u - mu
    var = jnp.mean(uc * uc, axis=-1, keepdims=True)
    y = uc * lax.rsqrt(var + LN_EPS) * lng_ref[...] + lnb_ref[...]
    return _silu(y).astype(BF16)


def _lru_branch_rows(rbuf_ref, cw_ref, cb_ref, wg_ref, ba_ref, bx_ref, lam_ref, hstate_ref, a_ref, b_ref,
                     yl_ref, r0):
    w = LRU_WIDTH
    n = BRANCH_STEP_ROWS
    window = rbuf_ref[pl.ds(r0, n + LRU_HALO), :]
    xc = jnp.broadcast_to(cb_ref[...], (n, w))
    for k in range(LRU_CONV_KERNEL):
        shift = LRU_HALO - (LRU_CONV_KERNEL - 1) + k
        if shift % SUBLANES == 0:
            tap = window[shift:shift + n]
        else:
            tap = pltpu.roll(window, n + LRU_HALO - shift, 0)[:n]
        xc = xc + cw_ref[k:k + 1, :] * tap

    xcb = xc.astype(BF16)
    hd = w // LRU_HEADS
    pre = [jnp.dot(xcb[:, hix * hd:(hix + 1) * hd], wg_ref[hix], preferred_element_type=F32)
           for hix in range(LRU_HEADS)]
    r = jax.nn.sigmoid(jnp.concatenate([p[:, :hd] for p in pre], axis=-1) + ba_ref[...])
    gate = jax.nn.sigmoid(jnp.concatenate([p[:, hd:] for p in pre], axis=-1) + bx_ref[...])
    log_a = (-LRU_C) * r * _softplus(-lam_ref[...])
    a = jnp.exp(log_a)
    a_ref[...] = a
    b_ref[...] = jnp.sqrt(jnp.tanh(-log_a) * (1.0 + a * a)) * (gate * xc)

    row = lax.broadcasted_iota(jnp.int32, (SUBLANES, w), 0)
    h_prev = hstate_ref[...]
    for g in range(n // SUBLANES):
        rows = slice(g * SUBLANES, (g + 1) * SUBLANES)
        ag = a_ref[rows, :]
        bg = b_ref[rows, :]
        d = 1
        while d < SUBLANES:
            a_sh = pltpu.roll(ag, d, 0)
            b_sh = pltpu.roll(bg, d, 0)
            m = row >= d
            bg = jnp.where(m, ag * b_sh + bg, bg)
            ag = jnp.where(m, ag * a_sh, ag)
            d *= 2
        h = ag * h_prev + bg
        b_ref[rows, :] = h
        h_prev = jnp.broadcast_to(h[SUBLANES - 1:SUBLANES, :], (SUBLANES, w))
    hstate_ref[...] = h_prev
    yl_ref[pl.ds(r0, n), :] = b_ref[...].astype(yl_ref.dtype)


def _mixer_in_kernel(x0_ref, x1_ref, x2_ref, x3_ref, g_ref, w_ref,
                     dww_ref, dwb_ref, lng_ref, lnb_ref,
                     cw_ref, cb_ref, wg_ref, ba_ref, bx_ref, lam_ref,
                     u_ref, yc_ref, yl_ref,
                     h_ref, val_ref, cbuf_ref, rbuf_ref, hstate_ref, a_ref, b_ref, *, tiles_per_seq):
    i = pl.program_id(0)
    j = pl.program_id(1)
    x_refs = (x0_ref, x1_ref, x2_ref, x3_ref)
    tm = x0_ref.shape[0]
    first_branch_step = N_BRANCH_BLOCKS
    n_branch_steps = tm // BRANCH_STEP_ROWS

    def project():
        return jnp.dot(h_ref[...], w_ref[...], preferred_element_type=F32)

    @pl.when(j == 0)
    def _():
        @pl.when(i % tiles_per_seq == 0)
        def _():
            cbuf_ref[0:CONV_HALO, :] = jnp.zeros((CONV_HALO, CONV_WIDTH), F32)
            rbuf_ref[0:LRU_HALO, :] = jnp.zeros((LRU_HALO, LRU_WIDTH), F32)
            hstate_ref[...] = jnp.zeros_like(hstate_ref)
            cbuf_ref[CONV_HALO + tm:CONV_HALO + tm + SUBLANES, :] = jnp.zeros((SUBLANES, CONV_WIDTH), F32)

        @pl.when(i % tiles_per_seq != 0)
        def _():
            cbuf_ref[0:CONV_HALO, :] = cbuf_ref[tm:tm + CONV_HALO, :]
            rbuf_ref[0:LRU_HALO, :] = rbuf_ref[tm:tm + LRU_HALO, :]

        d = h_ref.shape[1]
        cw = d // len(x_refs)
        sq = None
        for xr in x_refs:
            xc = xr[...]
            part = jnp.sum(xc * xc, axis=-1, keepdims=True)
            sq = part if sq is None else sq + part
        inv = lax.rsqrt(sq * (1.0 / d) + RMS_EPS)
        for c, xr in enumerate(x_refs):
            h_ref[:, c * cw:(c + 1) * cw] = (xr[...] * inv * g_ref[:, c * cw:(c + 1) * cw]).astype(BF16)
        val_ref[...] = project()

    @pl.when(j == 1)
    def _():
        cbuf_ref[CONV_HALO:CONV_HALO + tm, :] = val_ref[...] * jax.nn.sigmoid(project())

    @pl.when(j == 2)
    def _():
        rbuf_ref[LRU_HALO:LRU_HALO + tm, :] = project()

    @pl.when(jnp.logical_and(j >= first_branch_step, j < first_branch_step + n_branch_steps))
    def _():
        r0 = pl.multiple_of((j - first_branch_step) * BRANCH_STEP_ROWS, BRANCH_STEP_ROWS)
        _lru_branch_rows(rbuf_ref, cw_ref, cb_ref, wg_ref, ba_ref, bx_ref, lam_ref, hstate_ref, a_ref, b_ref,
                         yl_ref, r0)
        yc_ref[pl.ds(r0, BRANCH_STEP_ROWS), :] = _conv_branch_rows(cbuf_ref, dww_ref, dwb_ref, lng_ref, lnb_ref, r0)
        u_ref[...] = project().astype(u_ref.dtype)

    @pl.when(j >= first_branch_step + n_branch_steps)
    def _():
        u_ref[...] = project().astype(u_ref.dtype)


X_CHUNKS = 4


def _mixer_in(x, g, w, layer, conv_params, lru_params, *, seq_len):
    t, d = x.shape
    tm = TM_MIXER_IN
    n_tiles = t // tm
    assert seq_len % tm == 0 and tm % BRANCH_STEP_ROWS == 0 and BRANCH_STEP_ROWS % CONV_ROW_CHUNK == 0
    assert N_BRANCH_BLOCKS + tm // BRANCH_STEP_ROWS <= N_COL_BLOCKS and d % X_CHUNKS == 0
    dww, dwb, lng, lnb = conv_params
    cw, cb, wg, ba, bx, lam = lru_params
    hd = LRU_WIDTH // LRU_HEADS

    def w_block(i, j):
        blk = jnp.where(j == 2, W_BLOCK_LRU_X, jnp.where(jnp.logical_and(j > 2, j <= W_BLOCK_LRU_X), j - 1, j))
        return (layer, 0, blk)

    def x_chunk(c):
        def index(i, j):
            ahead = (j >= N_COL_BLOCKS - X_CHUNKS + c).astype(jnp.int32)
            return (jnp.minimum(i + ahead, n_tiles - 1), c)
        return pl.BlockSpec((tm, d // X_CHUNKS), index)

    const = lambda shape: pl.BlockSpec(shape, lambda i, j: (0,) * len(shape))
    return pl.pallas_call(
        functools.partial(_mixer_in_kernel, tiles_per_seq=seq_len // tm),
        out_shape=(jax.ShapeDtypeStruct((t, U_WIDTH), BF16),
                   jax.ShapeDtypeStruct((t, CONV_WIDTH), BF16),
                   jax.ShapeDtypeStruct((t, LRU_WIDTH), BF16)),
        grid=(n_tiles, N_COL_BLOCKS),
        in_specs=[
            *[x_chunk(c) for c in range(X_CHUNKS)],
            const((1, d)),
            pl.BlockSpec((None, d, COL_BLOCK), w_block),
            const((CONV_KERNEL, CONV_WIDTH)), const((1, CONV_WIDTH)), const((1, CONV_WIDTH)),
            const((1, CONV_WIDTH)),
            const((LRU_CONV_KERNEL, LRU_WIDTH)), const((1, LRU_WIDTH)),
            const((LRU_HEADS, hd, 2 * hd)), const((1, LRU_WIDTH)), const((1, LRU_WIDTH)), const((1, LRU_WIDTH)),
        ],
        out_specs=(
            pl.BlockSpec((tm, COL_BLOCK), lambda i, j: (i, jnp.maximum(j - N_BRANCH_BLOCKS, 0))),
            pl.BlockSpec((tm, CONV_WIDTH), lambda i, j: (i, 0)),
            pl.BlockSpec((tm, LRU_WIDTH), lambda i, j: (i, 0)),
        ),
        scratch_shapes=[
            pltpu.VMEM((tm, d), BF16),
            pltpu.VMEM((tm, CONV_WIDTH), F32),
            pltpu.VMEM((CONV_HALO + tm + SUBLANES, CONV_WIDTH), F32),
            pltpu.VMEM((LRU_HALO + tm, LRU_WIDTH), F32),
            pltpu.VMEM((SUBLANES, LRU_WIDTH), F32),
            pltpu.VMEM((BRANCH_STEP_ROWS, LRU_WIDTH), F32),
            pltpu.VMEM((BRANCH_STEP_ROWS, LRU_WIDTH), F32),
        ],
        compiler_params=_params("arbitrary", "arbitrary"),
        name="mixer_in",
    )(*([x] * X_CHUNKS), g, w, dww, dwb, lng, lnb, cw, cb, wg, ba, bx, lam)


SB_T = 128
SB_GROUP = 32
SB_EAGER = 3
SB_NARROW = 48
SB_PAD = SB_GROUP - 1
SB_LOG_WEIGHT_FLOOR = -104.0


def _sb_split(log_1mb):
    hi = log_1mb.astype(BF16)
    lo = (log_1mb - hi.astype(F32)).astype(BF16)
    return jnp.concatenate([hi, lo], axis=1)


def _sb_attn_kernel(q_ref, k_ref, v_ref, uo_ref, o_ref, kp_ref, vp_ref, acc_ref, carry_ref):
    s_len = q_ref.shape[0]
    scale = HEAD_DIM ** -0.5
    t_sz, n_grp = SB_T, SB_GROUP
    pad_rows = SB_PAD * t_sz
    kp_ref[0:pad_rows, :] = jnp.zeros((pad_rows, HEAD_DIM), BF16)
    vp_ref[0:pad_rows, :] = jnp.zeros((pad_rows, HEAD_DIM), BF16)
    kp_ref[pad_rows:pad_rows + s_len, :] = k_ref[...]
    vp_ref[pad_rows:pad_rows + s_len, :] = v_ref[...]
    causal = (lax.broadcasted_iota(jnp.int32, (t_sz, t_sz), 1)
              < lax.broadcasted_iota(jnp.int32, (t_sz, t_sz), 0))

    def q_tile(i):
        return q_ref[pl.ds(pl.multiple_of(i * t_sz, t_sz), t_sz), :]

    def neg_scores(q, k):
        zn = lax.dot_general(q, k, (((1,), (1,)), ((), ())), preferred_element_type=F32) * (-scale)
        log_1mb = jnp.minimum(zn, 0.0) - jnp.log(1.0 + jnp.exp(-jnp.abs(zn)))
        return zn, log_1mb

    def group(g, _):
        i0 = g * n_grp

        def eager():
            n_lo = SB_NARROW
            zns, ls, zn0s, l0s, starts, lhs = [], [], [], [], [], []
            for r in range(n_grp):
                start = pl.multiple_of((i0 + r + SB_PAD - 2) * t_sz, t_sz)
                q = q_tile(i0 + r)
                zn, log_1mb = neg_scores(q, kp_ref[pl.ds(start + t_sz, 2 * t_sz), :])
                zn0, log_1mb0 = neg_scores(q[:n_lo], kp_ref[pl.ds(start, t_sz), :])
                lhs.append(jnp.concatenate([
                    _sb_split(log_1mb0),
                    _sb_split(log_1mb[:, :t_sz]),
                    _sb_split(jnp.where(causal, log_1mb[:, t_sz:], 0.0)),
                ], axis=0))
                zns.append(zn)
                ls.append(log_1mb)
                zn0s.append(zn0)
                l0s.append(log_1mb0)
                starts.append(start)
            sums = [jnp.dot(x, uo_ref[...], preferred_element_type=F32) for x in lhs]
            m_lo = m_hi = None
            for r in range(n_grp):
                s0, s1, s2 = sums[r][:n_lo], sums[r][n_lo:n_lo + t_sz], sums[r][n_lo + t_sz:]
                zn, log_1mb = zns[r], ls[r]
                w2 = jnp.where(causal, jnp.exp((log_1mb[:, t_sz:] - zn[:, t_sz:]) + s2[:, :t_sz]), 0.0)
                carry = s2[:, t_sz:]
                w1 = jnp.exp((log_1mb[:, :t_sz] - zn[:, :t_sz]) + (carry + s1[:, :t_sz]))
                carry = carry + s1[:, t_sz:]
                w0 = jnp.exp((l0s[r] - zn0s[r]) + (carry[:n_lo] + s0[:, :t_sz]))
                carry_lo = carry[:n_lo] + s0[:, t_sz:]
                w12 = jnp.concatenate([w1.astype(BF16), w2.astype(BF16)], axis=1)
                acc = jnp.dot(w12, vp_ref[pl.ds(starts[r] + t_sz, 2 * t_sz), :], preferred_element_type=F32)
                acc_lo = jnp.dot(w0.astype(BF16), vp_ref[pl.ds(starts[r], t_sz), :], preferred_element_type=F32)
                acc_ref[r, 0:n_lo, :] = acc[:n_lo] + acc_lo
                acc_ref[r, n_lo:t_sz, :] = acc[n_lo:]
                carry_ref[r, 0:n_lo, :] = carry_lo
                carry_ref[r, n_lo:t_sz, :] = carry[n_lo:]
                m_lo = carry_lo if m_lo is None else jnp.maximum(m_lo, carry_lo)
                m_hi = carry[n_lo:] if m_hi is None else jnp.maximum(m_hi, carry[n_lo:])
            return jnp.max(m_lo), jnp.max(m_hi)

        def rest_of_third_block():
            n_lo, n_hi = SB_NARROW, t_sz - SB_NARROW
            zns, ls, starts, lhs = [], [], [], []
            for r in range(n_grp):
                start = pl.multiple_of((i0 + r + SB_PAD - 2) * t_sz, t_sz)
                zn, log_1mb = neg_scores(q_tile(i0 + r)[n_lo:], kp_ref[pl.ds(start, t_sz), :])
                lhs.append(_sb_split(log_1mb))
                zns.append(zn)
                ls.append(log_1mb)
                starts.append(start)
            sums = jnp.dot(jnp.concatenate(lhs, axis=0), uo_ref[...], preferred_element_type=F32)
            m = None
            for r in range(n_grp):
                sr = sums[r * n_hi:(r + 1) * n_hi]
                carry = carry_ref[r, n_lo:t_sz, :]
                w = jnp.exp((ls[r] - zns[r]) + (carry + sr[:, :t_sz]))
                v = vp_ref[pl.ds(starts[r], t_sz), :]
                acc_ref[r, n_lo:t_sz, :] += jnp.dot(w.astype(BF16), v, preferred_element_type=F32)
                carry = carry + sr[:, t_sz:]
                carry_ref[r, n_lo:t_sz, :] = carry
                m = carry if m is None else jnp.maximum(m, carry)
            return jnp.max(m)

        def step(d):
            zns, ls, starts, lhs = [], [], [], []
            for r in range(n_grp):
                start = pl.multiple_of((i0 + r + SB_PAD - d) * t_sz, t_sz)
                zn, log_1mb = neg_scores(q_tile(i0 + r), kp_ref[pl.ds(start, t_sz), :])
                lhs.append(_sb_split(log_1mb))
                zns.append(zn)
                ls.append(log_1mb)
                starts.append(start)
            sums = jnp.dot(jnp.concatenate(lhs, axis=0), uo_ref[...], preferred_element_type=F32)
            m = None
            for r in range(n_grp):
                sr = sums[r * t_sz:(r + 1) * t_sz]
                w = jnp.exp((ls[r] - zns[r]) + (carry_ref[r] + sr[:, :t_sz]))
                v = vp_ref[pl.ds(starts[r], t_sz), :]
                acc_ref[r] += jnp.dot(w.astype(BF16), v, preferred_element_type=F32)
                carry = carry_ref[r] + sr[:, t_sz:]
                carry_ref[r] = carry
                m = carry if m is None else jnp.maximum(m, carry)
            return jnp.max(m)

        def more(c):
            d, m = c
            return jnp.logical_and(d <= i0 + n_grp - 1, m > SB_LOG_WEIGHT_FLOOR)

        m_lo, m_hi = eager()
        m = lax.cond(m_hi > SB_LOG_WEIGHT_FLOOR,
                     lambda: jnp.maximum(m_lo, rest_of_third_block()), lambda: m_lo)
        lax.while_loop(more, lambda c: (c[0] + 1, step(c[0])), (jnp.int32(SB_EAGER), m))
        for r in range(n_grp):
            o_ref[pl.ds(pl.multiple_of((i0 + r) * t_sz, t_sz), t_sz), :] = acc_ref[r].astype(o_ref.dtype)
        return 0

    lax.fori_loop(0, s_len // (t_sz * n_grp), group, 0)


def _sb_attention(u3):
    b, s, _ = u3.shape
    dh = HEAD_DIM
    assert SB_EAGER - 1 <= SB_PAD and s % (SB_T * SB_GROUP) == 0
    row = lax.broadcasted_iota(jnp.int32, (2 * SB_T, 2 * SB_T), 0) % SB_T
    col = lax.broadcasted_iota(jnp.int32, (2 * SB_T, 2 * SB_T), 1)
    uo = jnp.where((col >= SB_T) | (row > col), 1.0, 0.0).astype(BF16)

    def head_spec(col0):
        return pl.BlockSpec((None, s, dh), lambda bi, hi: (bi, 0, col0 // dh + hi))

    padded = pltpu.VMEM((s + SB_PAD * SB_T, dh), BF16)
    return pl.pallas_call(
        _sb_attn_kernel,
        out_shape=jax.ShapeDtypeStruct((b, s, ATTN_WIDTH), BF16),
        grid=(b, ATTN_HEADS),
        in_specs=[head_spec(UCOL_Q), head_spec(UCOL_K), head_spec(UCOL_V),
                  pl.BlockSpec((2 * SB_T, 2 * SB_T), lambda bi, hi: (0, 0))],
        out_specs=pl.BlockSpec((None, s, dh), lambda bi, hi: (bi, 0, hi)),
        scratch_shapes=[padded, padded,
                        pltpu.VMEM((SB_GROUP, SB_T, dh), F32), pltpu.VMEM((SB_GROUP, SB_T, SB_T), F32)],
        compiler_params=_params("parallel", "parallel"),
        name="sb_attention",
    )(u3, u3, u3, uo)


def _out_proj_kernel(x_ref, yc_ref, ya_ref, yl_ref, gc_ref, ga0_ref, ga1_ref, gl_ref,
                     pw_ref, nc_ref, na_ref, nl_ref, w_ref, o_ref):
    def normed(y, n_ref):
        return y * _rms_scale(y) * n_ref[...]

    def gated(yn, gate_ref):
        return (yn * _silu(gate_ref[...].astype(F32))).astype(BF16)

    half = ATTN_WIDTH // 2
    y_conv = jnp.dot(yc_ref[...], pw_ref[...], preferred_element_type=F32)
    ya = normed(ya_ref[...].astype(F32), na_ref)
    parts = [
        (gated(normed(y_conv, nc_ref), gc_ref), 0),
        (jnp.concatenate([gated(ya[:, :half], ga0_ref), gated(ya[:, half:], ga1_ref)], axis=-1), CONV_WIDTH),
        (gated(normed(yl_ref[...].astype(F32), nl_ref), gl_ref), CONV_WIDTH + ATTN_WIDTH),
    ]
    acc = x_ref[...]
    for y, r0 in parts:
        acc = acc + jnp.dot(y, w_ref[r0:r0 + y.shape[1], :], preferred_element_type=F32)
    o_ref[...] = acc


def _out_proj(x, yc, ya, yl, u, pw, nc, na, nl, w, layer):
    t, d = x.shape
    tm = TM_OUT_PROJ
    gw = COL_BLOCK
    return pl.pallas_call(
        _out_proj_kernel,
        out_shape=jax.ShapeDtypeStruct((t, d), F32),
        grid=(t // tm,),
        in_specs=[
            pl.BlockSpec((tm, d), lambda i: (i, 0)),
            pl.BlockSpec((tm, CONV_WIDTH), lambda i: (i, 0)),
            pl.BlockSpec((tm, ATTN_WIDTH), lambda i: (i, 0)),
            pl.BlockSpec((tm, LRU_WIDTH), lambda i: (i, 0)),
            pl.BlockSpec((tm, gw), lambda i: (i, UCOL_CGATE // gw)),
            pl.BlockSpec((tm, gw), lambda i: (i, UCOL_AGATE // gw)),
            pl.BlockSpec((tm, gw), lambda i: (i, UCOL_AGATE // gw + 1)),
            pl.BlockSpec((tm, gw), lambda i: (i, UCOL_RGATE // gw)),
            pl.BlockSpec((CONV_WIDTH, CONV_WIDTH), lambda i: (0, 0)),
            pl.BlockSpec((1, CONV_WIDTH), lambda i: (0, 0)),
            pl.BlockSpec((1, ATTN_WIDTH), lambda i: (0, 0)),
            pl.BlockSpec((1, LRU_WIDTH), lambda i: (0, 0)),
            pl.BlockSpec((None, d, d), lambda i: (layer, 0, 0)),
        ],
        out_specs=pl.BlockSpec((tm, d), lambda i: (i, 0)),
        compiler_params=_params("parallel"),
        name="out_proj",
    )(x, yc, ya, yl, u, u, u, u, pw, nc, na, nl, w)


def _xattn_kernel(x_ref, g_ref, wq_ref, k_ref, v_ref, wo_ref, fg_ref, o_ref, *, final_norm):
    x = x_ref[...]
    h = (x * _rms_scale(x) * g_ref[...]).astype(BF16)
    q = jnp.dot(h, wq_ref[...], preferred_element_type=F32).astype(BF16)
    dh = XATTN_WIDTH // XATTN_HEADS
    scale = dh ** -0.5
    heads = []
    for n in range(XATTN_HEADS):
        qh = q[:, n * dh:(n + 1) * dh]
        kh = k_ref[:, n * dh:(n + 1) * dh]
        vh = v_ref[:, n * dh:(n + 1) * dh]
        s = lax.dot_general(qh, kh, (((1,), (1,)), ((), ())), preferred_element_type=F32) * scale
        e = jnp.exp(s - jnp.max(s, axis=-1, keepdims=True))
        p = (e / jnp.sum(e, axis=-1, keepdims=True)).astype(BF16)
        heads.append(jnp.dot(p, vh, preferred_element_type=F32).astype(BF16))
    acc = x + jnp.dot(jnp.concatenate(heads, axis=-1), wo_ref[...], preferred_element_type=F32)
    if final_norm:
        acc = acc * _rms_scale(acc) * fg_ref[...]
    o_ref[...] = acc


def _xattn(x3, g, wq, kv, wo, fg, layer, *, final_norm):
    b, s, d = x3.shape
    m = kv.shape[1]
    tm = TM_XATTN
    xw = XATTN_WIDTH
    return pl.pallas_call(
        functools.partial(_xattn_kernel, final_norm=final_norm),
        out_shape=jax.ShapeDtypeStruct((b, s, d), F32),
        grid=(b, s // tm),
        in_specs=[
            pl.BlockSpec((None, tm, d), lambda bi, ti: (bi, ti, 0)),
            pl.BlockSpec((1, d), lambda bi, ti: (0, 0)),
            pl.BlockSpec((None, d, xw), lambda bi, ti: (layer, 0, 0)),
            pl.BlockSpec((None, m, xw), lambda bi, ti: (bi, 0, 0)),
            pl.BlockSpec((None, m, xw), lambda bi, ti: (bi, 0, 1)),
            pl.BlockSpec((None, xw, d), lambda bi, ti: (layer, 0, 0)),
            pl.BlockSpec((1, d), lambda bi, ti: (0, 0)),
        ],
        out_specs=pl.BlockSpec((None, tm, d), lambda bi, ti: (bi, ti, 0)),
        compiler_params=_params("parallel", "parallel"),
        name="xattn_final" if final_norm else "xattn",
    )(x3, g, wq, kv, kv, wo, fg)


def kernel(x, mem, mix_norm_g, w_in, conv_dw_w, conv_dw_b, conv_ln_g, conv_ln_b, conv_pw_w,
           lru_conv_w, lru_conv_b, lru_wa, lru_ba, lru_wx, lru_bx, lru_lambda,
           out_norm_conv, out_norm_attn, out_norm_lru, w_out,
           xattn_norm_g, mem_norm_g, xattn_wq, xattn_wkv, xattn_wo, final_norm_g):
    b, s, d = x.shape
    m = mem.shape[1]
    depth = w_in.shape[0]
    t = b * s
    row = lambda a: a.reshape(1, -1).astype(F32)

    xt = x.reshape(t, d)
    memt = mem.reshape(b * m, d)
    w_in_b, w_out_b = w_in.astype(BF16), w_out.astype(BF16)
    wq_b, wkv_b, wo_b = xattn_wq.astype(BF16), xattn_wkv.astype(BF16), xattn_wo.astype(BF16)
    for l in range(depth):
        conv_params = (conv_dw_w[l], row(conv_dw_b[l]), row(conv_ln_g[l]), row(conv_ln_b[l]))
        lru_gates = jnp.concatenate([lru_wa[l], lru_wx[l]], axis=-1).astype(BF16)
        lru_params = (lru_conv_w[l], row(lru_conv_b[l]), lru_gates, row(lru_ba[l]), row(lru_bx[l]),
                      row(lru_lambda[l]))
        u, conv_act, y_lru = _mixer_in(xt, row(mix_norm_g[l]), w_in_b, l, conv_params, lru_params, seq_len=s)
        y_attn = _sb_attention(u.reshape(b, s, U_WIDTH))
        xt = _out_proj(xt, conv_act, y_attn.reshape(t, -1), y_lru, u, conv_pw_w[l].astype(BF16),
                       row(out_norm_conv[l]), row(out_norm_attn[l]), row(out_norm_lru[l]), w_out_b, l)
        kv = _norm_matmul(memt, row(mem_norm_g[l]), wkv_b, l, tm=256, tn=1024,
                          name="mem_kv").reshape(b, m, 2 * XATTN_WIDTH)
        xt = _xattn(xt.reshape(b, s, d), row(xattn_norm_g[l]), wq_b, kv, wo_b, row(final_norm_g), l,
                    final_norm=(l == depth - 1)).reshape(t, d)
    return xt.reshape(b, s, d)
```

```python
import functools

import jax
import jax.numpy as jnp
from jax import lax
from jax.experimental import pallas as pl
from jax.experimental.pallas import tpu as pltpu

F32 = jnp.float32
BF16 = jnp.bfloat16

D_MODEL = 2048
CONV_WIDTH = 512
CONV_KERNEL = 31
HEAD_DIM = 128
ATTN_WIDTH = 1024
ATTN_HEADS = 8
LRU_WIDTH = 512
LRU_HEADS = 4
LRU_CONV_KERNEL = 4
LRU_C = 8.0
XATTN_HEADS = 4
XATTN_WIDTH = 512
IN_WIDTH = 3 * CONV_WIDTH + 4 * ATTN_WIDTH + 2 * LRU_WIDTH

COL_BLOCK = 512
N_COL_BLOCKS = IN_WIDTH // COL_BLOCK
W_BLOCK_LRU_X = 11
N_BRANCH_BLOCKS = 3
U_WIDTH = IN_WIDTH - N_BRANCH_BLOCKS * COL_BLOCK
UCOL_CGATE, UCOL_Q, UCOL_K, UCOL_V, UCOL_AGATE, UCOL_RGATE = 0, 512, 1536, 2560, 3584, 4608

VMEM_LIMIT_BYTES = 56 * 1024 * 1024
SUBLANES = 8
LANES = 128

TM_MIXER_IN = 1024
TM_OUT_PROJ = 512
TM_XATTN = 1024

RMS_EPS = 1e-6
LN_EPS = 1e-5


def _params(*sem):
    return pltpu.CompilerParams(dimension_semantics=sem, vmem_limit_bytes=VMEM_LIMIT_BYTES)


def _rms_scale(x):
    return lax.rsqrt(jnp.mean(x * x, axis=-1, keepdims=True) + RMS_EPS)


def _softplus(x):
    return jnp.maximum(x, 0.0) + jnp.log(1.0 + jnp.exp(-jnp.abs(x)))


def _silu(x):
    return x * jax.nn.sigmoid(x)


def _norm_matmul_kernel(x_ref, g_ref, w_ref, o_ref, h_ref):
    @pl.when(pl.program_id(1) == 0)
    def _():
        x = x_ref[...]
        h_ref[...] = (x * _rms_scale(x) * g_ref[...]).astype(BF16)

    o_ref[...] = jnp.dot(h_ref[...], w_ref[...], preferred_element_type=F32).astype(o_ref.dtype)


def _norm_matmul(x, g, w, layer, *, tm, tn, name):
    t, d = x.shape
    n = w.shape[2]
    return pl.pallas_call(
        _norm_matmul_kernel,
        out_shape=jax.ShapeDtypeStruct((t, n), BF16),
        grid=(t // tm, n // tn),
        in_specs=[
            pl.BlockSpec((tm, d), lambda i, j: (i, 0)),
            pl.BlockSpec((1, d), lambda i, j: (0, 0)),
            pl.BlockSpec((None, d, tn), lambda i, j: (layer, 0, j)),
        ],
        out_specs=pl.BlockSpec((tm, tn), lambda i, j: (i, j)),
        scratch_shapes=[pltpu.VMEM((tm, d), BF16)],
        compiler_params=_params("parallel", "arbitrary"),
        name=name,
    )(x, g, w)


CONV_HALO = 32
CONV_ROW_CHUNK = 64
BRANCH_STEP_ROWS = 128
LRU_HALO = SUBLANES


def _depthwise_conv_rows(buf_ref, dww_ref, dwb_ref, r0):
    base = CONV_HALO - (CONV_KERNEL - 1)
    rows = CONV_ROW_CHUNK + SUBLANES
    n_m = (base + CONV_KERNEL - 1) // SUBLANES + 1
    out = []
    for lb in range(CONV_WIDTH // LANES):
        lanes = slice(lb * LANES, (lb + 1) * LANES)
        window = buf_ref[pl.ds(r0, rows + SUBLANES * (n_m - 1)), lanes]
        a = None
        for rho in reversed(range(SUBLANES)):
            q = None
            for m in range(n_m):
                k = SUBLANES * m + rho - base
                if 0 <= k < CONV_KERNEL:
                    term = dww_ref[k:k + 1, lanes] * window[SUBLANES * m:SUBLANES * m + rows]
                    q = term if q is None else q + term
            a = q if a is None else q + pltpu.roll(a, rows - 1, 0)
        out.append(a[:CONV_ROW_CHUNK] + dwb_ref[:, lanes])
    return jnp.concatenate(out, axis=1)


def _conv_branch_rows(cbuf_ref, dww_ref, dwb_ref, lng_ref, lnb_ref, r0):
    u = jnp.concatenate([_depthwise_conv_rows(cbuf_ref, dww_ref, dwb_ref, r0 + c * CONV_ROW_CHUNK)
                         for c in range(BRANCH_STEP_ROWS // CONV_ROW_CHUNK)], axis=0)
    mu = jnp.mean(u, axis=-1, keepdims=True)
    uc = u - mu
    var = jnp.mean(uc * uc, axis=-1, keepdims=True)
    y = uc * lax.rsqrt(var + LN_EPS) * lng_ref[...] + lnb_ref[...]
    return _silu(y).astype(BF16)


def _lru_branch_rows(rbuf_ref, cw_ref, cb_ref, wg_ref, ba_ref, bx_ref, lam_ref, hstate_ref, a_ref, b_ref,
                     yl_ref, r0):
    w = LRU_WIDTH
    n = BRANCH_STEP_ROWS
    window = rbuf_ref[pl.ds(r0, n + LRU_HALO), :]
    xc = jnp.broadcast_to(cb_ref[...], (n, w))
    for k in range(LRU_CONV_KERNEL):
        shift = LRU_HALO - (LRU_CONV_KERNEL - 1) + k
        if shift % SUBLANES == 0:
            tap = window[shift:shift + n]
        else:
            tap = pltpu.roll(window, n + LRU_HALO - shift, 0)[:n]
        xc = xc + cw_ref[k:k + 1, :] * tap

    xcb = xc.astype(BF16)
    hd = w // LRU_HEADS
    pre = [jnp.dot(xcb[:, hix * hd:(hix + 1) * hd], wg_ref[hix], preferred_element_type=F32)
           for hix in range(LRU_HEADS)]
    r = jax.nn.sigmoid(jnp.concatenate([p[:, :hd] for p in pre], axis=-1) + ba_ref[...])
    gate = jax.nn.sigmoid(jnp.concatenate([p[:, hd:] for p in pre], axis=-1) + bx_ref[...])
    log_a = (-LRU_C) * r * _softplus(-lam_ref[...])
    a = jnp.exp(log_a)
    a_ref[...] = a
    b_ref[...] = jnp.sqrt(jnp.tanh(-log_a) * (1.0 + a * a)) * (gate * xc)

    row = lax.broadcasted_iota(jnp.int32, (SUBLANES, w), 0)
    h_prev = hstate_ref[...]
    for g in range(n // SUBLANES):
        rows = slice(g * SUBLANES, (g + 1) * SUBLANES)
        ag = a_ref[rows, :]
        bg = b_ref[rows, :]
        d = 1
        while d < SUBLANES:
            a_sh = pltpu.roll(ag, d, 0)
            b_sh = pltpu.roll(bg, d, 0)
            m = row >= d
            bg = jnp.where(m, ag * b_sh + bg, bg)
            ag = jnp.where(m, ag * a_sh, ag)
            d *= 2
        h = ag * h_prev + bg
        b_ref[rows, :] = h
        h_prev = jnp.broadcast_to(h[SUBLANES - 1:SUBLANES, :], (SUBLANES, w))
    hstate_ref[...] = h_prev
    yl_ref[pl.ds(r0, n), :] = b_ref[...].astype(yl_ref.dtype)


def _mixer_in_kernel(x0_ref, x1_ref, x2_ref, x3_ref, g_ref, w_ref,
                     dww_ref, dwb_ref, lng_ref, lnb_ref,
                     cw_ref, cb_ref, wg_ref, ba_ref, bx_ref, lam_ref,
                     u_ref, yc_ref, yl_ref,
                     h_ref, val_ref, cbuf_ref, rbuf_ref, hstate_ref, a_ref, b_ref, *, tiles_per_seq):
    i = pl.program_id(0)
    j = pl.program_id(1)
    x_refs = (x0_ref, x1_ref, x2_ref, x3_ref)
    tm = x0_ref.shape[0]
    first_branch_step = N_BRANCH_BLOCKS
    n_branch_steps = tm // BRANCH_STEP_ROWS

    def project():
        return jnp.dot(h_ref[...], w_ref[...], preferred_element_type=F32)

    @pl.when(j == 0)
    def _():
        @pl.when(i % tiles_per_seq == 0)
        def _():
            cbuf_ref[0:CONV_HALO, :] = jnp.zeros((CONV_HALO, CONV_WIDTH), F32)
            rbuf_ref[0:LRU_HALO, :] = jnp.zeros((LRU_HALO, LRU_WIDTH), F32)
            hstate_ref[...] = jnp.zeros_like(hstate_ref)
            cbuf_ref[CONV_HALO + tm:CONV_HALO + tm + SUBLANES, :] = jnp.zeros((SUBLANES, CONV_WIDTH), F32)

        @pl.when(i % tiles_per_seq != 0)
        def _():
            cbuf_ref[0:CONV_HALO, :] = cbuf_ref[tm:tm + CONV_HALO, :]
            rbuf_ref[0:LRU_HALO, :] = rbuf_ref[tm:tm + LRU_HALO, :]

        d = h_ref.shape[1]
        cw = d // len(x_refs)
        sq = None
        for xr in x_refs:
            xc = xr[...]
            part = jnp.sum(xc * xc, axis=-1, keepdims=True)
            sq = part if sq is None else sq + part
        inv = lax.rsqrt(sq * (1.0 / d) + RMS_EPS)
        for c, xr in enumerate(x_refs):
            h_ref[:, c * cw:(c + 1) * cw] = (xr[...] * inv * g_ref[:, c * cw:(c + 1) * cw]).astype(BF16)
        val_ref[...] = project()

    @pl.when(j == 1)
    def _():
        cbuf_ref[CONV_HALO:CONV_HALO + tm, :] = val_ref[...] * jax.nn.sigmoid(project())

    @pl.when(j == 2)
    def _():
        rbuf_ref[LRU_HALO:LRU_HALO + tm, :] = project()

    @pl.when(jnp.logical_and(j >= first_branch_step, j < first_branch_step + n_branch_steps))
    def _():
        r0 = pl.multiple_of((j - first_branch_step) * BRANCH_STEP_ROWS, BRANCH_STEP_ROWS)
        _lru_branch_rows(rbuf_ref, cw_ref, cb_ref, wg_ref, ba_ref, bx_ref, lam_ref, hstate_ref, a_ref, b_ref,
                         yl_ref, r0)
        yc_ref[pl.ds(r0, BRANCH_STEP_ROWS), :] = _conv_branch_rows(cbuf_ref, dww_ref, dwb_ref, lng_ref, lnb_ref, r0)
        u_ref[...] = project().astype(u_ref.dtype)

    @pl.when(j >= first_branch_step + n_branch_steps)
    def _():
        u_ref[...] = project().astype(u_ref.dtype)


X_CHUNKS = 4


def _mixer_in(x, g, w, layer, conv_params, lru_params, *, seq_len):
    t, d = x.shape
    tm = TM_MIXER_IN
    n_tiles = t // tm
    assert seq_len % tm == 0 and tm % BRANCH_STEP_ROWS == 0 and BRANCH_STEP_ROWS % CONV_ROW_CHUNK == 0
    assert N_BRANCH_BLOCKS + tm // BRANCH_STEP_ROWS <= N_COL_BLOCKS and d % X_CHUNKS == 0
    dww, dwb, lng, lnb = conv_params
    cw, cb, wg, ba, bx, lam = lru_params
    hd = LRU_WIDTH // LRU_HEADS

    def w_block(i, j):
        blk = jnp.where(j == 2, W_BLOCK_LRU_X, jnp.where(jnp.logical_and(j > 2, j <= W_BLOCK_LRU_X), j - 1, j))
        return (layer, 0, blk)

    def x_chunk(c):
        def index(i, j):
            ahead = (j >= N_COL_BLOCKS - X_CHUNKS + c).astype(jnp.int32)
            return (jnp.minimum(i + ahead, n_tiles - 1), c)
        return pl.BlockSpec((tm, d // X_CHUNKS), index)

    const = lambda shape: pl.BlockSpec(shape, lambda i, j: (0,) * len(shape))
    return pl.pallas_call(
        functools.partial(_mixer_in_kernel, tiles_per_seq=seq_len // tm),
        out_shape=(jax.ShapeDtypeStruct((t, U_WIDTH), BF16),
                   jax.ShapeDtypeStruct((t, CONV_WIDTH), BF16),
                   jax.ShapeDtypeStruct((t, LRU_WIDTH), BF16)),
        grid=(n_tiles, N_COL_BLOCKS),
        in_specs=[
            *[x_chunk(c) for c in range(X_CHUNKS)],
            const((1, d)),
            pl.BlockSpec((None, d, COL_BLOCK), w_block),
            const((CONV_KERNEL, CONV_WIDTH)), const((1, CONV_WIDTH)), const((1, CONV_WIDTH)),
            const((1, CONV_WIDTH)),
            const((LRU_CONV_KERNEL, LRU_WIDTH)), const((1, LRU_WIDTH)),
            const((LRU_HEADS, hd, 2 * hd)), const((1, LRU_WIDTH)), const((1, LRU_WIDTH)), const((1, LRU_WIDTH)),
        ],
        out_specs=(
            pl.BlockSpec((tm, COL_BLOCK), lambda i, j: (i, jnp.maximum(j - N_BRANCH_BLOCKS, 0))),
            pl.BlockSpec((tm, CONV_WIDTH), lambda i, j: (i, 0)),
            pl.BlockSpec((tm, LRU_WIDTH), lambda i, j: (i, 0)),
        ),
        scratch_shapes=[
            pltpu.VMEM((tm, d), BF16),
            pltpu.VMEM((tm, CONV_WIDTH), F32),
            pltpu.VMEM((CONV_HALO + tm + SUBLANES, CONV_WIDTH), F32),
            pltpu.VMEM((LRU_HALO + tm, LRU_WIDTH), F32),
            pltpu.VMEM((SUBLANES, LRU_WIDTH), F32),
            pltpu.VMEM((BRANCH_STEP_ROWS, LRU_WIDTH), F32),
            pltpu.VMEM((BRANCH_STEP_ROWS, LRU_WIDTH), F32),
        ],
        compiler_params=_params("arbitrary", "arbitrary"),
        name="mixer_in",
    )(*([x] * X_CHUNKS), g, w, dww, dwb, lng, lnb, cw, cb, wg, ba, bx, lam)


SB_T = 128
SB_GROUP = 32
SB_EAGER = 3
SB_NARROW = 64
SB_PAD = SB_GROUP - 1
SB_LOG_WEIGHT_FLOOR = -104.0


def _sb_split(log_1mb):
    hi = log_1mb.astype(BF16)
    lo = (log_1mb - hi.astype(F32)).astype(BF16)
    return jnp.concatenate([hi, lo], axis=1)


def _sb_attn_kernel(q_ref, k_ref, v_ref, uo_ref, o_ref, kp_ref, vp_ref, acc_ref, carry_ref):
    s_len = q_ref.shape[0]
    scale = HEAD_DIM ** -0.5
    t_sz, n_grp = SB_T, SB_GROUP
    pad_rows = SB_PAD * t_sz
    @pl.when(jnp.logical_and(pl.program_id(0) == 0, pl.program_id(1) == 0))
    def _():
        kp_ref[0:pad_rows, :] = jnp.zeros((pad_rows, HEAD_DIM), BF16)
        vp_ref[0:pad_rows, :] = jnp.zeros((pad_rows, HEAD_DIM), BF16)

    kp_ref[pad_rows:pad_rows + s_len, :] = k_ref[...]
    vp_ref[pad_rows:pad_rows + s_len, :] = v_ref[...]
    causal = (lax.broadcasted_iota(jnp.int32, (t_sz, t_sz), 1)
              < lax.broadcasted_iota(jnp.int32, (t_sz, t_sz), 0))

    def q_tile(i):
        return q_ref[pl.ds(pl.multiple_of(i * t_sz, t_sz), t_sz), :]

    def neg_scores(q, k):
        zn = lax.dot_general(q, k, (((1,), (1,)), ((), ())), preferred_element_type=F32) * (-scale)
        log_1mb = jnp.minimum(zn, 0.0) - jnp.log(1.0 + jnp.exp(-jnp.abs(zn)))
        return zn, log_1mb

    def group(g, _):
        i0 = g * n_grp

        def eager():
            n_lo = SB_NARROW
            zns, ls, zn0s, l0s, starts, lhs = [], [], [], [], [], []
            for r in range(n_grp):
                start = pl.multiple_of((i0 + r + SB_PAD - 2) * t_sz, t_sz)
                q = q_tile(i0 + r)
                zn, log_1mb = neg_scores(q, kp_ref[pl.ds(start + t_sz, 2 * t_sz), :])
                zn0, log_1mb0 = neg_scores(q[:n_lo], kp_ref[pl.ds(start, t_sz), :])
                lhs.append(jnp.concatenate([
                    _sb_split(log_1mb0),
                    _sb_split(log_1mb[:, :t_sz]),
                    _sb_split(jnp.where(causal, log_1mb[:, t_sz:], 0.0)),
                ], axis=0))
                zns.append(zn)
                ls.append(log_1mb)
                zn0s.append(zn0)
                l0s.append(log_1mb0)
                starts.append(start)
            sums = [jnp.dot(x, uo_ref[...], preferred_element_type=F32) for x in lhs]
            m_lo = m_hi = None
            for r in range(n_grp):
                s0, s1, s2 = sums[r][:n_lo], sums[r][n_lo:n_lo + t_sz], sums[r][n_lo + t_sz:]
                zn, log_1mb = zns[r], ls[r]
                w2 = jnp.where(causal, jnp.exp((log_1mb[:, t_sz:] - zn[:, t_sz:]) + s2[:, :t_sz]), 0.0)
                carry = s2[:, t_sz:]
                w1 = jnp.exp((log_1mb[:, :t_sz] - zn[:, :t_sz]) + (carry + s1[:, :t_sz]))
                carry = carry + s1[:, t_sz:]
                w0 = jnp.exp((l0s[r] - zn0s[r]) + (carry[:n_lo] + s0[:, :t_sz]))
                carry_lo = carry[:n_lo] + s0[:, t_sz:]
                w12 = jnp.concatenate([w1.astype(BF16), w2.astype(BF16)], axis=1)
                acc = jnp.dot(w12, vp_ref[pl.ds(starts[r] + t_sz, 2 * t_sz), :], preferred_element_type=F32)
                acc_lo = jnp.dot(w0.astype(BF16), vp_ref[pl.ds(starts[r], t_sz), :], preferred_element_type=F32)
                acc = jnp.concatenate([acc[:n_lo] + acc_lo, acc[n_lo:]], axis=0)
                acc_ref[r] = acc
                o_ref[pl.ds(pl.multiple_of((i0 + r) * t_sz, t_sz), t_sz), :] = acc.astype(o_ref.dtype)
                carry_ref[r, 0:n_lo, :] = carry_lo
                carry_ref[r, n_lo:t_sz, :] = carry[n_lo:]
                m_lo = carry_lo if m_lo is None else jnp.maximum(m_lo, carry_lo)
                m_hi = carry[n_lo:] if m_hi is None else jnp.maximum(m_hi, carry[n_lo:])
            return jnp.max(m_lo), jnp.max(m_hi)

        def rest_of_third_block():
            n_lo, n_hi = SB_NARROW, t_sz - SB_NARROW
            zns, ls, starts, lhs = [], [], [], []
            for r in range(n_grp):
                start = pl.multiple_of((i0 + r + SB_PAD - 2) * t_sz, t_sz)
                zn, log_1mb = neg_scores(q_tile(i0 + r)[n_lo:], kp_ref[pl.ds(start, t_sz), :])
                lhs.append(_sb_split(log_1mb))
                zns.append(zn)
                ls.append(log_1mb)
                starts.append(start)
            sums = jnp.dot(jnp.concatenate(lhs, axis=0), uo_ref[...], preferred_element_type=F32)
            m = None
            for r in range(n_grp):
                sr = sums[r * n_hi:(r + 1) * n_hi]
                carry = carry_ref[r, n_lo:t_sz, :]
                w = jnp.exp((ls[r] - zns[r]) + (carry + sr[:, :t_sz]))
                v = vp_ref[pl.ds(starts[r], t_sz), :]
                acc_ref[r, n_lo:t_sz, :] += jnp.dot(w.astype(BF16), v, preferred_element_type=F32)
                carry = carry + sr[:, t_sz:]
                carry_ref[r, n_lo:t_sz, :] = carry
                m = carry if m is None else jnp.maximum(m, carry)
            return jnp.max(m)

        def step(d):
            zns, ls, starts, lhs = [], [], [], []
            for r in range(n_grp):
                start = pl.multiple_of((i0 + r + SB_PAD - d) * t_sz, t_sz)
                zn, log_1mb = neg_scores(q_tile(i0 + r), kp_ref[pl.ds(start, t_sz), :])
                lhs.append(_sb_split(log_1mb))
                zns.append(zn)
                ls.append(log_1mb)
                starts.append(start)
            sums = jnp.dot(jnp.concatenate(lhs, axis=0), uo_ref[...], preferred_element_type=F32)
            m = None
            for r in range(n_grp):
                sr = sums[r * t_sz:(r + 1) * t_sz]
                w = jnp.exp((ls[r] - zns[r]) + (carry_ref[r] + sr[:, :t_sz]))
                v = vp_ref[pl.ds(starts[r], t_sz), :]
                acc_ref[r] += jnp.dot(w.astype(BF16), v, preferred_element_type=F32)
                carry = carry_ref[r] + sr[:, t_sz:]
                carry_ref[r] = carry
                m = carry if m is None else jnp.maximum(m, carry)
            return jnp.max(m)

        def more(c):
            d, m = c
            return jnp.logical_and(d <= i0 + n_grp - 1, m > SB_LOG_WEIGHT_FLOOR)

        m_lo, m_hi = eager()
        m = lax.cond(m_hi > SB_LOG_WEIGHT_FLOOR,
                     lambda: jnp.maximum(m_lo, rest_of_third_block()), lambda: m_lo)
        lax.while_loop(more, lambda c: (c[0] + 1, step(c[0])), (jnp.int32(SB_EAGER), m))

        @pl.when(jnp.logical_or(m_hi > SB_LOG_WEIGHT_FLOOR, m > SB_LOG_WEIGHT_FLOOR))
        def _():
            for r in range(n_grp):
                o_ref[pl.ds(pl.multiple_of((i0 + r) * t_sz, t_sz), t_sz), :] = acc_ref[r].astype(o_ref.dtype)
        return 0

    lax.fori_loop(0, s_len // (t_sz * n_grp), group, 0)


def _sb_attention(u3):
    b, s, _ = u3.shape
    dh = HEAD_DIM
    assert SB_EAGER - 1 <= SB_PAD and s % (SB_T * SB_GROUP) == 0
    row = lax.broadcasted_iota(jnp.int32, (2 * SB_T, 2 * SB_T), 0) % SB_T
    col = lax.broadcasted_iota(jnp.int32, (2 * SB_T, 2 * SB_T), 1)
    uo = jnp.where((col >= SB_T) | (row > col), 1.0, 0.0).astype(BF16)

    def head_spec(col0):
        return pl.BlockSpec((None, s, dh), lambda bi, hi: (bi, 0, col0 // dh + hi))

    padded = pltpu.VMEM((s + SB_PAD * SB_T, dh), BF16)
    return pl.pallas_call(
        _sb_attn_kernel,
        out_shape=jax.ShapeDtypeStruct((b, s, ATTN_WIDTH), BF16),
        grid=(b, ATTN_HEADS),
        in_specs=[head_spec(UCOL_Q), head_spec(UCOL_K), head_spec(UCOL_V),
                  pl.BlockSpec((2 * SB_T, 2 * SB_T), lambda bi, hi: (0, 0))],
        out_specs=pl.BlockSpec((None, s, dh), lambda bi, hi: (bi, 0, hi)),
        scratch_shapes=[padded, padded,
                        pltpu.VMEM((SB_GROUP, SB_T, dh), F32), pltpu.VMEM((SB_GROUP, SB_T, SB_T), F32)],
        compiler_params=_params("arbitrary", "arbitrary"),
        name="sb_attention",
    )(u3, u3, u3, uo)


def _out_proj_kernel(x_ref, yc_ref, ya_ref, yl_ref, gc_ref, ga0_ref, ga1_ref, gl_ref,
                     pw_ref, nc_ref, na_ref, nl_ref, w_ref, o_ref):
    def normed(y, n_ref):
        return y * _rms_scale(y) * n_ref[...]

    def gated(yn, gate_ref):
        return (yn * _silu(gate_ref[...].astype(F32))).astype(BF16)

    half = ATTN_WIDTH // 2
    y_conv = jnp.dot(yc_ref[...], pw_ref[...], preferred_element_type=F32)
    ya = normed(ya_ref[...].astype(F32), na_ref)
    parts = [
        (gated(normed(y_conv, nc_ref), gc_ref), 0),
        (jnp.concatenate([gated(ya[:, :half], ga0_ref), gated(ya[:, half:], ga1_ref)], axis=-1), CONV_WIDTH),
        (gated(normed(yl_ref[...].astype(F32), nl_ref), gl_ref), CONV_WIDTH + ATTN_WIDTH),
    ]
    acc = x_ref[...]
    for y, r0 in parts:
        acc = acc + jnp.dot(y, w_ref[r0:r0 + y.shape[1], :], preferred_element_type=F32)
    o_ref[...] = acc


def _out_proj(x, yc, ya, yl, u, pw, nc, na, nl, w, layer):
    t, d = x.shape
    tm = TM_OUT_PROJ
    gw = COL_BLOCK
    return pl.pallas_call(
        _out_proj_kernel,
        out_shape=jax.ShapeDtypeStruct((t, d), F32),
        grid=(t // tm,),
        in_specs=[
            pl.BlockSpec((tm, d), lambda i: (i, 0)),
            pl.BlockSpec((tm, CONV_WIDTH), lambda i: (i, 0)),
            pl.BlockSpec((tm, ATTN_WIDTH), lambda i: (i, 0)),
            pl.BlockSpec((tm, LRU_WIDTH), lambda i: (i, 0)),
            pl.BlockSpec((tm, gw), lambda i: (i, UCOL_CGATE // gw)),
            pl.BlockSpec((tm, gw), lambda i: (i, UCOL_AGATE // gw)),
            pl.BlockSpec((tm, gw), lambda i: (i, UCOL_AGATE // gw + 1)),
            pl.BlockSpec((tm, gw), lambda i: (i, UCOL_RGATE // gw)),
            pl.BlockSpec((CONV_WIDTH, CONV_WIDTH), lambda i: (0, 0)),
            pl.BlockSpec((1, CONV_WIDTH), lambda i: (0, 0)),
            pl.BlockSpec((1, ATTN_WIDTH), lambda i: (0, 0)),
            pl.BlockSpec((1, LRU_WIDTH), lambda i: (0, 0)),
            pl.BlockSpec((None, d, d), lambda i: (layer, 0, 0)),
        ],
        out_specs=pl.BlockSpec((tm, d), lambda i: (i, 0)),
        compiler_params=_params("parallel"),
        name="out_proj",
    )(x, yc, ya, yl, u, u, u, u, pw, nc, na, nl, w)


def _xattn_kernel(x_ref, g_ref, wq_ref, k_ref, v_ref, wo_ref, fg_ref, o_ref, *, final_norm):
    x = x_ref[...]
    h = (x * _rms_scale(x) * g_ref[...]).astype(BF16)
    q = jnp.dot(h, wq_ref[...], preferred_element_type=F32).astype(BF16)
    dh = XATTN_WIDTH // XATTN_HEADS
    scale = dh ** -0.5
    heads = []
    for n in range(XATTN_HEADS):
        qh = q[:, n * dh:(n + 1) * dh]
        kh = k_ref[:, n * dh:(n + 1) * dh]
        vh = v_ref[:, n * dh:(n + 1) * dh]
        s = lax.dot_general(qh, kh, (((1,), (1,)), ((), ())), preferred_element_type=F32) * scale
        e = jnp.exp(s - jnp.max(s, axis=-1, keepdims=True))
        p = (e / jnp.sum(e, axis=-1, keepdims=True)).astype(BF16)
        heads.append(jnp.dot(p, vh, preferred_element_type=F32).astype(BF16))
    acc = x + jnp.dot(jnp.concatenate(heads, axis=-1), wo_ref[...], preferred_element_type=F32)
    if final_norm:
        acc = acc * _rms_scale(acc) * fg_ref[...]
    o_ref[...] = acc


def _xattn(x3, g, wq, kv, wo, fg, layer, *, final_norm):
    b, s, d = x3.shape
    m = kv.shape[1]
    tm = TM_XATTN
    xw = XATTN_WIDTH
    return pl.pallas_call(
        functools.partial(_xattn_kernel, final_norm=final_norm),
        out_shape=jax.ShapeDtypeStruct((b, s, d), F32),
        grid=(b, s // tm),
        in_specs=[
            pl.BlockSpec((None, tm, d), lambda bi, ti: (bi, ti, 0)),
            pl.BlockSpec((1, d), lambda bi, ti: (0, 0)),
            pl.BlockSpec((None, d, xw), lambda bi, ti: (layer, 0, 0)),
            pl.BlockSpec((None, m, xw), lambda bi, ti: (bi, 0, 0)),
            pl.BlockSpec((None, m, xw), lambda bi, ti: (bi, 0, 1)),
            pl.BlockSpec((None, xw, d), lambda bi, ti: (layer, 0, 0)),
            pl.BlockSpec((1, d), lambda bi, ti: (0, 0)),
        ],
        out_specs=pl.BlockSpec((None, tm, d), lambda bi, ti: (bi, ti, 0)),
        compiler_params=_params("parallel", "parallel"),
        name="xattn_final" if final_norm else "xattn",
    )(x3, g, wq, kv, kv, wo, fg)


def kernel(x, mem, mix_norm_g, w_in, conv_dw_w, conv_dw_b, conv_ln_g, conv_ln_b, conv_pw_w,
           lru_conv_w, lru_conv_b, lru_wa, lru_ba, lru_wx, lru_bx, lru_lambda,
           out_norm_conv, out_norm_attn, out_norm_lru, w_out,
           xattn_norm_g, mem_norm_g, xattn_wq, xattn_wkv, xattn_wo, final_norm_g):
    b, s, d = x.shape
    m = mem.shape[1]
    depth = w_in.shape[0]
    t = b * s
    row = lambda a: a.reshape(1, -1).astype(F32)

    xt = x.reshape(t, d)
    memt = mem.reshape(b * m, d)
    w_in_b, w_out_b = w_in.astype(BF16), w_out.astype(BF16)
    wq_b, wkv_b, wo_b = xattn_wq.astype(BF16), xattn_wkv.astype(BF16), xattn_wo.astype(BF16)
    for l in range(depth):
        conv_params = (conv_dw_w[l], row(conv_dw_b[l]), row(conv_ln_g[l]), row(conv_ln_b[l]))
        lru_gates = jnp.concatenate([lru_wa[l], lru_wx[l]], axis=-1).astype(BF16)
        lru_params = (lru_conv_w[l], row(lru_conv_b[l]), lru_gates, row(lru_ba[l]), row(lru_bx[l]),
                      row(lru_lambda[l]))
        u, conv_act, y_lru = _mixer_in(xt, row(mix_norm_g[l]), w_in_b, l, conv_params, lru_params, seq_len=s)
        y_attn = _sb_attention(u.reshape(b, s, U_WIDTH))
        xt = _out_proj(xt, conv_act, y_attn.reshape(t, -1), y_lru, u, conv_pw_w[l].astype(BF16),
                       row(out_norm_conv[l]), row(out_norm_attn[l]), row(out_norm_lru[l]), w_out_b, l)
        kv = _norm_matmul(memt, row(mem_norm_g[l]), wkv_b, l, tm=256, tn=1024,
                          name="mem_kv").reshape(b, m, 2 * XATTN_WIDTH)
        xt = _xattn(xt.reshape(b, s, d), row(xattn_norm_g[l]), wq_b, kv, wo_b, row(final_norm_g), l,
                    final_norm=(l == depth - 1)).reshape(t, d)
    return xt.reshape(b, s, d)
```

```python
import functools

import jax
import jax.numpy as jnp
from jax import lax
from jax.experimental import pallas as pl
from jax.experimental.pallas import tpu as pltpu

F32 = jnp.float32
BF16 = jnp.bfloat16

D_MODEL = 2048
CONV_WIDTH = 512
CONV_KERNEL = 31
HEAD_DIM = 128
ATTN_WIDTH = 1024
ATTN_HEADS = 8
LRU_WIDTH = 512
LRU_HEADS = 4
LRU_CONV_KERNEL = 4
LRU_C = 8.0
XATTN_HEADS = 4
XATTN_WIDTH = 512
IN_WIDTH = 3 * CONV_WIDTH + 4 * ATTN_WIDTH + 2 * LRU_WIDTH

COL_BLOCK = 512
N_COL_BLOCKS = IN_WIDTH // COL_BLOCK
W_BLOCK_LRU_X = 11
N_BRANCH_BLOCKS = 3
U_WIDTH = IN_WIDTH - N_BRANCH_BLOCKS * COL_BLOCK
UCOL_CGATE, UCOL_Q, UCOL_K, UCOL_V, UCOL_AGATE, UCOL_RGATE = 0, 512, 1536, 2560, 3584, 4608

VMEM_LIMIT_BYTES = 56 * 1024 * 1024
SUBLANES = 8
LANES = 128

TM_MIXER_IN = 1024
TM_OUT_PROJ = 512
TM_XATTN = 1024

RMS_EPS = 1e-6
LN_EPS = 1e-5


def _params(*sem):
    return pltpu.CompilerParams(dimension_semantics=sem, vmem_limit_bytes=VMEM_LIMIT_BYTES)


def _rms_scale(x):
    return lax.rsqrt(jnp.mean(x * x, axis=-1, keepdims=True) + RMS_EPS)


def _softplus(x):
    return jnp.maximum(x, 0.0) + jnp.log(1.0 + jnp.exp(-jnp.abs(x)))


def _silu(x):
    return x * jax.nn.sigmoid(x)


def _norm_matmul_kernel(x_ref, g_ref, w_ref, o_ref, h_ref):
    @pl.when(pl.program_id(1) == 0)
    def _():
        x = x_ref[...]
        h_ref[...] = (x * _rms_scale(x) * g_ref[...]).astype(BF16)

    o_ref[...] = jnp.dot(h_ref[...], w_ref[...], preferred_element_type=F32).astype(o_ref.dtype)


def _norm_matmul(x, g, w, layer, *, tm, tn, name):
    t, d = x.shape
    n = w.shape[2]
    return pl.pallas_call(
        _norm_matmul_kernel,
        out_shape=jax.ShapeDtypeStruct((t, n), BF16),
        grid=(t // tm, n // tn),
        in_specs=[
            pl.BlockSpec((tm, d), lambda i, j: (i, 0)),
            pl.BlockSpec((1, d), lambda i, j: (0, 0)),
            pl.BlockSpec((None, d, tn), lambda i, j: (layer, 0, j)),
        ],
        out_specs=pl.BlockSpec((tm, tn), lambda i, j: (i, j)),
        scratch_shapes=[pltpu.VMEM((tm, d), BF16)],
        compiler_params=_params("parallel", "arbitrary"),
        name=name,
    )(x, g, w)


CONV_HALO = 32
CONV_ROW_CHUNK = 64
BRANCH_STEP_ROWS = 128
LRU_HALO = SUBLANES


def _depthwise_conv_rows(buf_ref, dww_ref, dwb_ref, r0):
    base = CONV_HALO - (CONV_KERNEL - 1)
    rows = CONV_ROW_CHUNK + SUBLANES
    n_m = (base + CONV_KERNEL - 1) // SUBLANES + 1
    out = []
    for lb in range(CONV_WIDTH // LANES):
        lanes = slice(lb * LANES, (lb + 1) * LANES)
        window = buf_ref[pl.ds(r0, rows + SUBLANES * (n_m - 1)), lanes]
        a = None
        for rho in reversed(range(SUBLANES)):
            q = None
            for m in range(n_m):
                k = SUBLANES * m + rho - base
                if 0 <= k < CONV_KERNEL:
                    term = dww_ref[k:k + 1, lanes] * window[SUBLANES * m:SUBLANES * m + rows]
                    q = term if q is None else q + term
            a = q if a is None else q + pltpu.roll(a, rows - 1, 0)
        out.append(a[:CONV_ROW_CHUNK] + dwb_ref[:, lanes])
    return jnp.concatenate(out, axis=1)


def _conv_branch_rows(cbuf_ref, dww_ref, dwb_ref, lng_ref, lnb_ref, r0):
    u = jnp.concatenate([_depthwise_conv_rows(cbuf_ref, dww_ref, dwb_ref, r0 + c * CONV_ROW_CHUNK)
                         for c in range(BRANCH_STEP_ROWS // CONV_ROW_CHUNK)], axis=0)
    mu = jnp.mean(u, axis=-1, keepdims=True)
    uc = u - mu
    var = jnp.mean(uc * uc, axis=-1, keepdims=True)
    y = uc * lax.rsqrt(var + LN_EPS) * lng_ref[...] + lnb_ref[...]
    return _silu(y).astype(BF16)


def _lru_branch_rows(rbuf_ref, cw_ref, cb_ref, wg_ref, ba_ref, bx_ref, lam_ref, hstate_ref, a_ref, b_ref,
                     yl_ref, r0):
    w = LRU_WIDTH
    n = BRANCH_STEP_ROWS
    window = rbuf_ref[pl.ds(r0, n + LRU_HALO), :]
    xc = jnp.broadcast_to(cb_ref[...], (n, w))
    for k in range(LRU_CONV_KERNEL):
        shift = LRU_HALO - (LRU_CONV_KERNEL - 1) + k
        if shift % SUBLANES == 0:
            tap = window[shift:shift + n]
        else:
            tap = pltpu.roll(window, n + LRU_HALO - shift, 0)[:n]
        xc = xc + cw_ref[k:k + 1, :] * tap

    xcb = xc.astype(BF16)
    hd = w // LRU_HEADS
    pre = [jnp.dot(xcb[:, hix * hd:(hix + 1) * hd], wg_ref[hix], preferred_element_type=F32)
           for hix in range(LRU_HEADS)]
    r = jax.nn.sigmoid(jnp.concatenate([p[:, :hd] for p in pre], axis=-1) + ba_ref[...])
    gate = jax.nn.sigmoid(jnp.concatenate([p[:, hd:] for p in pre], axis=-1) + bx_ref[...])
    log_a = (-LRU_C) * r * _softplus(-lam_ref[...])
    a = jnp.exp(log_a)
    a_ref[...] = a
    b_ref[...] = jnp.sqrt(jnp.tanh(-log_a) * (1.0 + a * a)) * (gate * xc)

    row = lax.broadcasted_iota(jnp.int32, (SUBLANES, w), 0)
    h_prev = hstate_ref[...]
    for g in range(n // SUBLANES):
        rows = slice(g * SUBLANES, (g + 1) * SUBLANES)
        ag = a_ref[rows, :]
        bg = b_ref[rows, :]
        d = 1
        while d < SUBLANES:
            a_sh = pltpu.roll(ag, d, 0)
            b_sh = pltpu.roll(bg, d, 0)
            m = row >= d
            bg = jnp.where(m, ag * b_sh + bg, bg)
            ag = jnp.where(m, ag * a_sh, ag)
            d *= 2
        h = ag * h_prev + bg
        b_ref[rows, :] = h
        h_prev = jnp.broadcast_to(h[SUBLANES - 1:SUBLANES, :], (SUBLANES, w))
    hstate_ref[...] = h_prev
    yl_ref[pl.ds(r0, n), :] = b_ref[...].astype(yl_ref.dtype)


def _mixer_in_kernel(x0_ref, x1_ref, x2_ref, x3_ref, g_ref, w_ref,
                     dww_ref, dwb_ref, lng_ref, lnb_ref,
                     cw_ref, cb_ref, wg_ref, ba_ref, bx_ref, lam_ref,
                     u_ref, yc_ref, yl_ref,
                     h_ref, val_ref, cbuf_ref, rbuf_ref, hstate_ref, a_ref, b_ref, *, tiles_per_seq):
    i = pl.program_id(0)
    j = pl.program_id(1)
    x_refs = (x0_ref, x1_ref, x2_ref, x3_ref)
    tm = x0_ref.shape[0]
    first_branch_step = N_BRANCH_BLOCKS
    n_branch_steps = tm // BRANCH_STEP_ROWS

    def project():
        return jnp.dot(h_ref[...], w_ref[...], preferred_element_type=F32)

    @pl.when(j == 0)
    def _():
        @pl.when(i % tiles_per_seq == 0)
        def _():
            cbuf_ref[0:CONV_HALO, :] = jnp.zeros((CONV_HALO, CONV_WIDTH), F32)
            rbuf_ref[0:LRU_HALO, :] = jnp.zeros((LRU_HALO, LRU_WIDTH), F32)
            hstate_ref[...] = jnp.zeros_like(hstate_ref)
            cbuf_ref[CONV_HALO + tm:CONV_HALO + tm + SUBLANES, :] = jnp.zeros((SUBLANES, CONV_WIDTH), F32)

        @pl.when(i % tiles_per_seq != 0)
        def _():
            cbuf_ref[0:CONV_HALO, :] = cbuf_ref[tm:tm + CONV_HALO, :]
            rbuf_ref[0:LRU_HALO, :] = rbuf_ref[tm:tm + LRU_HALO, :]

        d = h_ref.shape[1]
        cw = d // len(x_refs)
        sq = None
        for xr in x_refs:
            xc = xr[...]
            part = jnp.sum(xc * xc, axis=-1, keepdims=True)
            sq = part if sq is None else sq + part
        inv = lax.rsqrt(sq * (1.0 / d) + RMS_EPS)
        for c, xr in enumerate(x_refs):
            h_ref[:, c * cw:(c + 1) * cw] = (xr[...] * inv * g_ref[:, c * cw:(c + 1) * cw]).astype(BF16)
        val_ref[...] = project()

    @pl.when(j == 1)
    def _():
        cbuf_ref[CONV_HALO:CONV_HALO + tm, :] = val_ref[...] * jax.nn.sigmoid(project())

    @pl.when(j == 2)
    def _():
        rbuf_ref[LRU_HALO:LRU_HALO + tm, :] = project()

    @pl.when(jnp.logical_and(j >= first_branch_step, j < first_branch_step + n_branch_steps))
    def _():
        r0 = pl.multiple_of((j - first_branch_step) * BRANCH_STEP_ROWS, BRANCH_STEP_ROWS)
        _lru_branch_rows(rbuf_ref, cw_ref, cb_ref, wg_ref, ba_ref, bx_ref, lam_ref, hstate_ref, a_ref, b_ref,
                         yl_ref, r0)
        yc_ref[pl.ds(r0, BRANCH_STEP_ROWS), :] = _conv_branch_rows(cbuf_ref, dww_ref, dwb_ref, lng_ref, lnb_ref, r0)
        u_ref[...] = project().astype(u_ref.dtype)

    @pl.when(j >= first_branch_step + n_branch_steps)
    def _():
        u_ref[...] = project().astype(u_ref.dtype)


X_CHUNKS = 4


def _mixer_in(x, g, w, layer, conv_params, lru_params, *, seq_len):
    t, d = x.shape
    tm = TM_MIXER_IN
    n_tiles = t // tm
    assert seq_len % tm == 0 and tm % BRANCH_STEP_ROWS == 0 and BRANCH_STEP_ROWS % CONV_ROW_CHUNK == 0
    assert N_BRANCH_BLOCKS + tm // BRANCH_STEP_ROWS <= N_COL_BLOCKS and d % X_CHUNKS == 0
    dww, dwb, lng, lnb = conv_params
    cw, cb, wg, ba, bx, lam = lru_params
    hd = LRU_WIDTH // LRU_HEADS

    def w_block(i, j):
        blk = jnp.where(j == 2, W_BLOCK_LRU_X, jnp.where(jnp.logical_and(j > 2, j <= W_BLOCK_LRU_X), j - 1, j))
        return (layer, 0, blk)

    def x_chunk(c):
        def index(i, j):
            ahead = (j >= N_COL_BLOCKS - X_CHUNKS + c).astype(jnp.int32)
            return (jnp.minimum(i + ahead, n_tiles - 1), c)
        return pl.BlockSpec((tm, d // X_CHUNKS), index)

    const = lambda shape: pl.BlockSpec(shape, lambda i, j: (0,) * len(shape))
    return pl.pallas_call(
        functools.partial(_mixer_in_kernel, tiles_per_seq=seq_len // tm),
        out_shape=(jax.ShapeDtypeStruct((t, U_WIDTH), BF16),
                   jax.ShapeDtypeStruct((t, CONV_WIDTH), BF16),
                   jax.ShapeDtypeStruct((t, LRU_WIDTH), BF16)),
        grid=(n_tiles, N_COL_BLOCKS),
        in_specs=[
            *[x_chunk(c) for c in range(X_CHUNKS)],
            const((1, d)),
            pl.BlockSpec((None, d, COL_BLOCK), w_block),
            const((CONV_KERNEL, CONV_WIDTH)), const((1, CONV_WIDTH)), const((1, CONV_WIDTH)),
            const((1, CONV_WIDTH)),
            const((LRU_CONV_KERNEL, LRU_WIDTH)), const((1, LRU_WIDTH)),
            const((LRU_HEADS, hd, 2 * hd)), const((1, LRU_WIDTH)), const((1, LRU_WIDTH)), const((1, LRU_WIDTH)),
        ],
        out_specs=(
            pl.BlockSpec((tm, COL_BLOCK), lambda i, j: (i, jnp.maximum(j - N_BRANCH_BLOCKS, 0))),
            pl.BlockSpec((tm, CONV_WIDTH), lambda i, j: (i, 0)),
            pl.BlockSpec((tm, LRU_WIDTH), lambda i, j: (i, 0)),
        ),
        scratch_shapes=[
            pltpu.VMEM((tm, d), BF16),
            pltpu.VMEM((tm, CONV_WIDTH), F32),
            pltpu.VMEM((CONV_HALO + tm + SUBLANES, CONV_WIDTH), F32),
            pltpu.VMEM((LRU_HALO + tm, LRU_WIDTH), F32),
            pltpu.VMEM((SUBLANES, LRU_WIDTH), F32),
            pltpu.VMEM((BRANCH_STEP_ROWS, LRU_WIDTH), F32),
            pltpu.VMEM((BRANCH_STEP_ROWS, LRU_WIDTH), F32),
        ],
        compiler_params=_params("arbitrary", "arbitrary"),
        name="mixer_in",
    )(*([x] * X_CHUNKS), g, w, dww, dwb, lng, lnb, cw, cb, wg, ba, bx, lam)


SB_T = 128
SB_NARROW = 64
SB_EAGER = 3
SB_LOG_WEIGHT_FLOOR = -104.0
SB_DONE = -1e30


def _sb_split(log_1mb):
    hi = log_1mb.astype(BF16)
    lo = (log_1mb - hi.astype(F32)).astype(BF16)
    return jnp.concatenate([hi, lo], axis=1)


def _sb_attn_kernel(q_ref, k_ref, v_ref, uo_ref, o_ref, acc_ref, carry_ref):
    s_len = q_ref.shape[0]
    scale = HEAD_DIM ** -0.5
    t_sz, n_lo = SB_T, SB_NARROW
    n_tiles = s_len // t_sz
    causal = (lax.broadcasted_iota(jnp.int32, (t_sz, t_sz), 1)
              < lax.broadcasted_iota(jnp.int32, (t_sz, t_sz), 0))

    def tile(i):
        return slice(i * t_sz, (i + 1) * t_sz)

    def neg_scores(q, k):
        zn = lax.dot_general(q, k, (((1,), (1,)), ((), ())), preferred_element_type=F32) * (-scale)
        log_1mb = jnp.minimum(zn, 0.0) - jnp.log(1.0 + jnp.exp(-jnp.abs(zn)))
        return zn, log_1mb

    def eager():
        zns, ls, zn0s, l0s, lhs = [], [], [], [], []
        for r in range(n_tiles):
            q = q_ref[tile(r), :]
            first = max(r - 1, 0)
            zn, log_1mb = neg_scores(q, k_ref[first * t_sz:(r + 1) * t_sz, :])
            pieces = [_sb_split(jnp.where(causal, log_1mb[:, -t_sz:], 0.0))]
            if r >= 1:
                pieces.insert(0, _sb_split(log_1mb[:, :t_sz]))
            if r >= 2:
                zn0, log_1mb0 = neg_scores(q[:n_lo], k_ref[tile(r - 2), :])
                pieces.insert(0, _sb_split(log_1mb0))
                zn0s.append(zn0)
                l0s.append(log_1mb0)
            else:
                zn0s.append(None)
                l0s.append(None)
            lhs.append(jnp.concatenate(pieces, axis=0))
            zns.append(zn)
            ls.append(log_1mb)
        sums = [jnp.dot(x, uo_ref[...], preferred_element_type=F32) for x in lhs]
        m_lo = m_hi = None
        for r in range(n_tiles):
            zn, log_1mb = zns[r], ls[r]
            s2 = sums[r][-t_sz:]
            w2 = jnp.where(causal, jnp.exp((log_1mb[:, -t_sz:] - zn[:, -t_sz:]) + s2[:, :t_sz]), 0.0)
            carry = s2[:, t_sz:]
            ws = [w2.astype(BF16)]
            if r >= 1:
                s1 = sums[r][-2 * t_sz:-t_sz]
                w1 = jnp.exp((log_1mb[:, :t_sz] - zn[:, :t_sz]) + (carry + s1[:, :t_sz]))
                carry = carry + s1[:, t_sz:]
                ws.insert(0, w1.astype(BF16))
            first = max(r - 1, 0)
            acc = jnp.dot(jnp.concatenate(ws, axis=1), v_ref[first * t_sz:(r + 1) * t_sz, :],
                          preferred_element_type=F32)
            if r >= 2:
                s0 = sums[r][:n_lo]
                w0 = jnp.exp((l0s[r] - zn0s[r]) + (carry[:n_lo] + s0[:, :t_sz]))
                acc_lo = jnp.dot(w0.astype(BF16), v_ref[tile(r - 2), :], preferred_element_type=F32)
                acc = jnp.concatenate([acc[:n_lo] + acc_lo, acc[n_lo:]], axis=0)
                carry_lo = carry[:n_lo] + s0[:, t_sz:]
                carry_ref[r, 0:n_lo, :] = carry_lo
                carry_ref[r, n_lo:t_sz, :] = carry[n_lo:]
                m_lo = carry_lo if m_lo is None else jnp.maximum(m_lo, carry_lo)
                m_hi = carry[n_lo:] if m_hi is None else jnp.maximum(m_hi, carry[n_lo:])
            acc_ref[r] = acc
            o_ref[tile(r), :] = acc.astype(o_ref.dtype)
        return jnp.max(m_lo), jnp.max(m_hi)

    def rest_of_third_block():
        n_hi = t_sz - n_lo
        tiles = range(2, n_tiles)
        zns, ls, lhs = {}, {}, []
        for r in tiles:
            zns[r], ls[r] = neg_scores(q_ref[r * t_sz + n_lo:(r + 1) * t_sz, :], k_ref[tile(r - 2), :])
            lhs.append(_sb_split(ls[r]))
        sums = jnp.dot(jnp.concatenate(lhs, axis=0), uo_ref[...], preferred_element_type=F32)
        m = None
        for n, r in enumerate(tiles):
            sr = sums[n * n_hi:(n + 1) * n_hi]
            carry = carry_ref[r, n_lo:t_sz, :]
            w = jnp.exp((ls[r] - zns[r]) + (carry + sr[:, :t_sz]))
            acc_ref[r, n_lo:t_sz, :] += jnp.dot(w.astype(BF16), v_ref[tile(r - 2), :], preferred_element_type=F32)
            carry = carry + sr[:, t_sz:]
            carry_ref[r, n_lo:t_sz, :] = carry
            if r >= SB_EAGER:
                m = carry if m is None else jnp.maximum(m, carry)
        return jnp.max(m)

    def step(d):
        tiles = range(SB_EAGER, n_tiles)
        zns, ls, starts, lhs = {}, {}, {}, []
        for r in tiles:
            starts[r] = pl.multiple_of(jnp.maximum(r - d, 0) * t_sz, t_sz)
            zns[r], ls[r] = neg_scores(q_ref[tile(r), :], k_ref[pl.ds(starts[r], t_sz), :])
            lhs.append(_sb_split(ls[r]))
        sums = jnp.dot(jnp.concatenate(lhs, axis=0), uo_ref[...], preferred_element_type=F32)
        m = None
        for n, r in enumerate(tiles):
            sr = sums[n * t_sz:(n + 1) * t_sz]
            w = jnp.exp((ls[r] - zns[r]) + (carry_ref[r] + sr[:, :t_sz]))
            w = jnp.where(r >= d, w, 0.0)
            acc_ref[r] += jnp.dot(w.astype(BF16), v_ref[pl.ds(starts[r], t_sz), :], preferred_element_type=F32)
            carry = carry_ref[r] + sr[:, t_sz:]
            carry_ref[r] = carry
            left = jnp.where(r > d, carry, SB_DONE)
            m = left if m is None else jnp.maximum(m, left)
        return jnp.max(m)

    def more(c):
        d, m = c
        return jnp.logical_and(d <= n_tiles - 1, m > SB_LOG_WEIGHT_FLOOR)

    m_lo, m_hi = eager()
    m = lax.cond(m_hi > SB_LOG_WEIGHT_FLOOR,
                 lambda: jnp.maximum(m_lo, rest_of_third_block()), lambda: m_lo)
    lax.while_loop(more, lambda c: (c[0] + 1, step(c[0])), (jnp.int32(SB_EAGER), m))

    @pl.when(jnp.logical_or(m_hi > SB_LOG_WEIGHT_FLOOR, m > SB_LOG_WEIGHT_FLOOR))
    def _():
        for r in range(2, n_tiles):
            o_ref[tile(r), :] = acc_ref[r].astype(o_ref.dtype)


def _sb_attention(u3):
    b, s, _ = u3.shape
    dh = HEAD_DIM
    assert s % SB_T == 0 and s // SB_T > SB_EAGER and SB_NARROW % 16 == 0
    row = lax.broadcasted_iota(jnp.int32, (2 * SB_T, 2 * SB_T), 0) % SB_T
    col = lax.broadcasted_iota(jnp.int32, (2 * SB_T, 2 * SB_T), 1)
    uo = jnp.where((col >= SB_T) | (row > col), 1.0, 0.0).astype(BF16)

    def head_spec(col0):
        return pl.BlockSpec((None, s, dh), lambda bi, hi: (bi, 0, col0 // dh + hi))

    n_tiles = s // SB_T
    return pl.pallas_call(
        _sb_attn_kernel,
        out_shape=jax.ShapeDtypeStruct((b, s, ATTN_WIDTH), BF16),
        grid=(b, ATTN_HEADS),
        in_specs=[head_spec(UCOL_Q), head_spec(UCOL_K), head_spec(UCOL_V),
                  pl.BlockSpec((2 * SB_T, 2 * SB_T), lambda bi, hi: (0, 0))],
        out_specs=pl.BlockSpec((None, s, dh), lambda bi, hi: (bi, 0, hi)),
        scratch_shapes=[pltpu.VMEM((n_tiles, SB_T, dh), F32), pltpu.VMEM((n_tiles, SB_T, SB_T), F32)],
        compiler_params=_params("parallel", "parallel"),
        name="sb_attention",
    )(u3, u3, u3, uo)


def _out_proj_kernel(x_ref, yc_ref, ya_ref, yl_ref, gc_ref, ga0_ref, ga1_ref, gl_ref,
                     pw_ref, nc_ref, na_ref, nl_ref, w_ref, o_ref):
    def normed(y, n_ref):
        return y * _rms_scale(y) * n_ref[...]

    def gated(yn, gate_ref):
        return (yn * _silu(gate_ref[...].astype(F32))).astype(BF16)

    half = ATTN_WIDTH // 2
    y_conv = jnp.dot(yc_ref[...], pw_ref[...], preferred_element_type=F32)
    ya = normed(ya_ref[...].astype(F32), na_ref)
    parts = [
        (gated(normed(y_conv, nc_ref), gc_ref), 0),
        (jnp.concatenate([gated(ya[:, :half], ga0_ref), gated(ya[:, half:], ga1_ref)], axis=-1), CONV_WIDTH),
        (gated(normed(yl_ref[...].astype(F32), nl_ref), gl_ref), CONV_WIDTH + ATTN_WIDTH),
    ]
    acc = x_ref[...]
    for y, r0 in parts:
        acc = acc + jnp.dot(y, w_ref[r0:r0 + y.shape[1], :], preferred_element_type=F32)
    o_ref[...] = acc


def _out_proj(x, yc, ya, yl, u, pw, nc, na, nl, w, layer):
    t, d = x.shape
    tm = TM_OUT_PROJ
    gw = COL_BLOCK
    return pl.pallas_call(
        _out_proj_kernel,
        out_shape=jax.ShapeDtypeStruct((t, d), F32),
        grid=(t // tm,),
        in_specs=[
            pl.BlockSpec((tm, d), lambda i: (i, 0)),
            pl.BlockSpec((tm, CONV_WIDTH), lambda i: (i, 0)),
            pl.BlockSpec((tm, ATTN_WIDTH), lambda i: (i, 0)),
            pl.BlockSpec((tm, LRU_WIDTH), lambda i: (i, 0)),
            pl.BlockSpec((tm, gw), lambda i: (i, UCOL_CGATE // gw)),
            pl.BlockSpec((tm, gw), lambda i: (i, UCOL_AGATE // gw)),
            pl.BlockSpec((tm, gw), lambda i: (i, UCOL_AGATE // gw + 1)),
            pl.BlockSpec((tm, gw), lambda i: (i, UCOL_RGATE // gw)),
            pl.BlockSpec((CONV_WIDTH, CONV_WIDTH), lambda i: (0, 0)),
            pl.BlockSpec((1, CONV_WIDTH), lambda i: (0, 0)),
            pl.BlockSpec((1, ATTN_WIDTH), lambda i: (0, 0)),
            pl.BlockSpec((1, LRU_WIDTH), lambda i: (0, 0)),
            pl.BlockSpec((None, d, d), lambda i: (layer, 0, 0)),
        ],
        out_specs=pl.BlockSpec((tm, d), lambda i: (i, 0)),
        compiler_params=_params("parallel"),
        name="out_proj",
    )(x, yc, ya, yl, u, u, u, u, pw, nc, na, nl, w)


def _xattn_kernel(x_ref, g_ref, wq_ref, k_ref, v_ref, wo_ref, fg_ref, o_ref, *, final_norm):
    x = x_ref[...]
    h = (x * _rms_scale(x) * g_ref[...]).astype(BF16)
    q = jnp.dot(h, wq_ref[...], preferred_element_type=F32).astype(BF16)
    dh = XATTN_WIDTH // XATTN_HEADS
    scale = dh ** -0.5
    heads = []
    for n in range(XATTN_HEADS):
        qh = q[:, n * dh:(n + 1) * dh]
        kh = k_ref[:, n * dh:(n + 1) * dh]
        vh = v_ref[:, n * dh:(n + 1) * dh]
        s = lax.dot_general(qh, kh, (((1,), (1,)), ((), ())), preferred_element_type=F32) * scale
        e = jnp.exp(s - jnp.max(s, axis=-1, keepdims=True))
        p = (e / jnp.sum(e, axis=-1, keepdims=True)).astype(BF16)
        heads.append(jnp.dot(p, vh, preferred_element_type=F32).astype(BF16))
    acc = x + jnp.dot(jnp.concatenate(heads, axis=-1), wo_ref[...], preferred_element_type=F32)
    if final_norm:
        acc = acc * _rms_scale(acc) * fg_ref[...]
    o_ref[...] = acc


def _xattn(x3, g, wq, kv, wo, fg, layer, *, final_norm):
    b, s, d = x3.shape
    m = kv.shape[1]
    tm = TM_XATTN
    xw = XATTN_WIDTH
    return pl.pallas_call(
        functools.partial(_xattn_kernel, final_norm=final_norm),
        out_shape=jax.ShapeDtypeStruct((b, s, d), F32),
        grid=(b, s // tm),
        in_specs=[
            pl.BlockSpec((None, tm, d), lambda bi, ti: (bi, ti, 0)),
            pl.BlockSpec((1, d), lambda bi, ti: (0, 0)),
            pl.BlockSpec((None, d, xw), lambda bi, ti: (layer, 0, 0)),
            pl.BlockSpec((None, m, xw), lambda bi, ti: (bi, 0, 0)),
            pl.BlockSpec((None, m, xw), lambda bi, ti: (bi, 0, 1)),
            pl.BlockSpec((None, xw, d), lambda bi, ti: (layer, 0, 0)),
            pl.BlockSpec((1, d), lambda bi, ti: (0, 0)),
        ],
        out_specs=pl.BlockSpec((None, tm, d), lambda bi, ti: (bi, ti, 0)),
        compiler_params=_params("parallel", "parallel"),
        name="xattn_final" if final_norm else "xattn",
    )(x3, g, wq, kv, kv, wo, fg)


def kernel(x, mem, mix_norm_g, w_in, conv_dw_w, conv_dw_b, conv_ln_g, conv_ln_b, conv_pw_w,
           lru_conv_w, lru_conv_b, lru_wa, lru_ba, lru_wx, lru_bx, lru_lambda,
           out_norm_conv, out_norm_attn, out_norm_lru, w_out,
           xattn_norm_g, mem_norm_g, xattn_wq, xattn_wkv, xattn_wo, final_norm_g):
    b, s, d = x.shape
    m = mem.shape[1]
    depth = w_in.shape[0]
    t = b * s
    row = lambda a: a.reshape(1, -1).astype(F32)

    xt = x.reshape(t, d)
    memt = mem.reshape(b * m, d)
    w_in_b, w_out_b = w_in.astype(BF16), w_out.astype(BF16)
    wq_b, wkv_b, wo_b = xattn_wq.astype(BF16), xattn_wkv.astype(BF16), xattn_wo.astype(BF16)
    for l in range(depth):
        conv_params = (conv_dw_w[l], row(conv_dw_b[l]), row(conv_ln_g[l]), row(conv_ln_b[l]))
        lru_gates = jnp.concatenate([lru_wa[l], lru_wx[l]], axis=-1).astype(BF16)
        lru_params = (lru_conv_w[l], row(lru_conv_b[l]), lru_gates, row(lru_ba[l]), row(lru_bx[l]),
                      row(lru_lambda[l]))
        u, conv_act, y_lru = _mixer_in(xt, row(mix_norm_g[l]), w_in_b, l, conv_params, lru_params, seq_len=s)
        y_attn = _sb_attention(u.reshape(b, s, U_WIDTH))
        xt = _out_proj(xt, conv_act, y_attn.reshape(t, -1), y_lru, u, conv_pw_w[l].astype(BF16),
                       row(out_norm_conv[l]), row(out_norm_attn[l]), row(out_norm_lru[l]), w_out_b, l)
        kv = _norm_matmul(memt, row(mem_norm_g[l]), wkv_b, l, tm=256, tn=1024,
                          name="mem_kv").reshape(b, m, 2 * XATTN_WIDTH)
        xt = _xattn(xt.reshape(b, s, d), row(xattn_norm_g[l]), wq_b, kv, wo_b, row(final_norm_g), l,
                    final_norm=(l == depth - 1)).reshape(t, d)
    return xt.reshape(b, s, d)
```

```python
import functools

import jax
import jax.numpy as jnp
from jax import lax
from jax.experimental import pallas as pl
from jax.experimental.pallas import tpu as pltpu

F32 = jnp.float32
BF16 = jnp.bfloat16

D_MODEL = 2048
CONV_WIDTH = 512
CONV_KERNEL = 31
HEAD_DIM = 128
ATTN_WIDTH = 1024
ATTN_HEADS = 8
LRU_WIDTH = 512
LRU_HEADS = 4
LRU_CONV_KERNEL = 4
LRU_C = 8.0
XATTN_HEADS = 4
XATTN_WIDTH = 512
IN_WIDTH = 3 * CONV_WIDTH + 4 * ATTN_WIDTH + 2 * LRU_WIDTH

COL_BLOCK = 512
N_COL_BLOCKS = IN_WIDTH // COL_BLOCK
W_BLOCK_LRU_X = 11
N_BRANCH_BLOCKS = 3
U_WIDTH = IN_WIDTH - N_BRANCH_BLOCKS * COL_BLOCK
UCOL_CGATE, UCOL_Q, UCOL_K, UCOL_V, UCOL_AGATE, UCOL_RGATE = 0, 512, 1536, 2560, 3584, 4608

VMEM_LIMIT_BYTES = 56 * 1024 * 1024
SUBLANES = 8
LANES = 128

TM_MIXER_IN = 1024
TM_OUT_PROJ = 512
TM_XATTN = 1024

RMS_EPS = 1e-6
LN_EPS = 1e-5


def _params(*sem):
    return pltpu.CompilerParams(dimension_semantics=sem, vmem_limit_bytes=VMEM_LIMIT_BYTES)


def _rms_scale(x):
    return lax.rsqrt(jnp.mean(x * x, axis=-1, keepdims=True) + RMS_EPS)


def _softplus(x):
    return jnp.maximum(x, 0.0) + jnp.log(1.0 + jnp.exp(-jnp.abs(x)))


def _silu(x):
    return x * jax.nn.sigmoid(x)


def _norm_matmul_kernel(x_ref, g_ref, w_ref, o_ref, h_ref):
    @pl.when(pl.program_id(1) == 0)
    def _():
        x = x_ref[...]
        h_ref[...] = (x * _rms_scale(x) * g_ref[...]).astype(BF16)

    o_ref[...] = jnp.dot(h_ref[...], w_ref[...], preferred_element_type=F32).astype(o_ref.dtype)


def _norm_matmul(x, g, w, layer, *, tm, tn, name):
    t, d = x.shape
    n = w.shape[2]
    return pl.pallas_call(
        _norm_matmul_kernel,
        out_shape=jax.ShapeDtypeStruct((t, n), BF16),
        grid=(t // tm, n // tn),
        in_specs=[
            pl.BlockSpec((tm, d), lambda i, j: (i, 0)),
            pl.BlockSpec((1, d), lambda i, j: (0, 0)),
            pl.BlockSpec((None, d, tn), lambda i, j: (layer, 0, j)),
        ],
        out_specs=pl.BlockSpec((tm, tn), lambda i, j: (i, j)),
        scratch_shapes=[pltpu.VMEM((tm, d), BF16)],
        compiler_params=_params("parallel", "arbitrary"),
        name=name,
    )(x, g, w)


CONV_HALO = 32
CONV_ROW_CHUNK = 64
BRANCH_STEP_ROWS = 128
LRU_HALO = SUBLANES


def _depthwise_conv_rows(buf_ref, dww_ref, dwb_ref, r0):
    base = CONV_HALO - (CONV_KERNEL - 1)
    rows = CONV_ROW_CHUNK + SUBLANES
    n_m = (base + CONV_KERNEL - 1) // SUBLANES + 1
    out = []
    for lb in range(CONV_WIDTH // LANES):
        lanes = slice(lb * LANES, (lb + 1) * LANES)
        window = buf_ref[pl.ds(r0, rows + SUBLANES * (n_m - 1)), lanes]
        a = None
        for rho in reversed(range(SUBLANES)):
            q = None
            for m in range(n_m):
                k = SUBLANES * m + rho - base
                if 0 <= k < CONV_KERNEL:
                    term = dww_ref[k:k + 1, lanes] * window[SUBLANES * m:SUBLANES * m + rows]
                    q = term if q is None else q + term
            a = q if a is None else q + pltpu.roll(a, rows - 1, 0)
        out.append(a[:CONV_ROW_CHUNK] + dwb_ref[:, lanes])
    return jnp.concatenate(out, axis=1)


def _conv_branch_rows(cbuf_ref, dww_ref, dwb_ref, lng_ref, lnb_ref, r0):
    u = jnp.concatenate([_depthwise_conv_rows(cbuf_ref, dww_ref, dwb_ref, r0 + c * CONV_ROW_CHUNK)
                         for c in range(BRANCH_STEP_ROWS // CONV_ROW_CHUNK)], axis=0)
    mu = jnp.mean(u, axis=-1, keepdims=True)
    uc = u - mu
    var = jnp.mean(uc * uc, axis=-1, keepdims=True)
    y = uc * lax.rsqrt(var + LN_EPS) * lng_ref[...] + lnb_ref[...]
    return _silu(y).astype(BF16)


def _lru_branch_rows(rbuf_ref, cw_ref, cb_ref, wg_ref, ba_ref, bx_ref, lam_ref, hstate_ref, a_ref, b_ref,
                     yl_ref, r0):
    w = LRU_WIDTH
    n = BRANCH_STEP_ROWS
    window = rbuf_ref[pl.ds(r0, n + LRU_HALO), :]
    xc = jnp.broadcast_to(cb_ref[...], (n, w))
    for k in range(LRU_CONV_KERNEL):
        shift = LRU_HALO - (LRU_CONV_KERNEL - 1) + k
        if shift % SUBLANES == 0:
            tap = window[shift:shift + n]
        else:
            tap = pltpu.roll(window, n + LRU_HALO - shift, 0)[:n]
        xc = xc + cw_ref[k:k + 1, :] * tap

    xcb = xc.astype(BF16)
    hd = w // LRU_HEADS
    pre = [jnp.dot(xcb[:, hix * hd:(hix + 1) * hd], wg_ref[hix], preferred_element_type=F32)
           for hix in range(LRU_HEADS)]
    r = jax.nn.sigmoid(jnp.concatenate([p[:, :hd] for p in pre], axis=-1) + ba_ref[...])
    gate = jax.nn.sigmoid(jnp.concatenate([p[:, hd:] for p in pre], axis=-1) + bx_ref[...])
    log_a = (-LRU_C) * r * _softplus(-lam_ref[...])
    a = jnp.exp(log_a)
    a_ref[...] = a
    b_ref[...] = jnp.sqrt(jnp.tanh(-log_a) * (1.0 + a * a)) * (gate * xc)

    row = lax.broadcasted_iota(jnp.int32, (SUBLANES, w), 0)
    h_prev = hstate_ref[...]
    for g in range(n // SUBLANES):
        rows = slice(g * SUBLANES, (g + 1) * SUBLANES)
        ag = a_ref[rows, :]
        bg = b_ref[rows, :]
        d = 1
        while d < SUBLANES:
            a_sh = pltpu.roll(ag, d, 0)
            b_sh = pltpu.roll(bg, d, 0)
            m = row >= d
            bg = jnp.where(m, ag * b_sh + bg, bg)
            ag = jnp.where(m, ag * a_sh, ag)
            d *= 2
        h = ag * h_prev + bg
        b_ref[rows, :] = h
        h_prev = jnp.broadcast_to(h[SUBLANES - 1:SUBLANES, :], (SUBLANES, w))
    hstate_ref[...] = h_prev
    yl_ref[pl.ds(r0, n), :] = b_ref[...].astype(yl_ref.dtype)


def _mixer_in_kernel(x0_ref, x1_ref, x2_ref, x3_ref, g_ref, w_ref,
                     dww_ref, dwb_ref, lng_ref, lnb_ref,
                     cw_ref, cb_ref, wg_ref, ba_ref, bx_ref, lam_ref,
                     u_ref, yc_ref, yl_ref,
                     h_ref, val_ref, cbuf_ref, rbuf_ref, hstate_ref, a_ref, b_ref, *, tiles_per_seq):
    i = pl.program_id(0)
    j = pl.program_id(1)
    x_refs = (x0_ref, x1_ref, x2_ref, x3_ref)
    tm = x0_ref.shape[0]
    first_branch_step = N_BRANCH_BLOCKS
    n_branch_steps = tm // BRANCH_STEP_ROWS

    def project():
        return jnp.dot(h_ref[...], w_ref[...], preferred_element_type=F32)

    @pl.when(j == 0)
    def _():
        @pl.when(i % tiles_per_seq == 0)
        def _():
            cbuf_ref[0:CONV_HALO, :] = jnp.zeros((CONV_HALO, CONV_WIDTH), F32)
            rbuf_ref[0:LRU_HALO, :] = jnp.zeros((LRU_HALO, LRU_WIDTH), F32)
            hstate_ref[...] = jnp.zeros_like(hstate_ref)
            cbuf_ref[CONV_HALO + tm:CONV_HALO + tm + SUBLANES, :] = jnp.zeros((SUBLANES, CONV_WIDTH), F32)

        @pl.when(i % tiles_per_seq != 0)
        def _():
            cbuf_ref[0:CONV_HALO, :] = cbuf_ref[tm:tm + CONV_HALO, :]
            rbuf_ref[0:LRU_HALO, :] = rbuf_ref[tm:tm + LRU_HALO, :]

        d = h_ref.shape[1]
        cw = d // len(x_refs)
        sq = None
        for xr in x_refs:
            xc = xr[...]
            part = jnp.sum(xc * xc, axis=-1, keepdims=True)
            sq = part if sq is None else sq + part
        inv = lax.rsqrt(sq * (1.0 / d) + RMS_EPS)
        for c, xr in enumerate(x_refs):
            h_ref[:, c * cw:(c + 1) * cw] = (xr[...] * inv * g_ref[:, c * cw:(c + 1) * cw]).astype(BF16)
        val_ref[...] = project()

    @pl.when(j == 1)
    def _():
        cbuf_ref[CONV_HALO:CONV_HALO + tm, :] = val_ref[...] * jax.nn.sigmoid(project())

    @pl.when(j == 2)
    def _():
        rbuf_ref[LRU_HALO:LRU_HALO + tm, :] = project()

    @pl.when(jnp.logical_and(j >= first_branch_step, j < first_branch_step + n_branch_steps))
    def _():
        r0 = pl.multiple_of((j - first_branch_step) * BRANCH_STEP_ROWS, BRANCH_STEP_ROWS)
        _lru_branch_rows(rbuf_ref, cw_ref, cb_ref, wg_ref, ba_ref, bx_ref, lam_ref, hstate_ref, a_ref, b_ref,
                         yl_ref, r0)
        yc_ref[pl.ds(r0, BRANCH_STEP_ROWS), :] = _conv_branch_rows(cbuf_ref, dww_ref, dwb_ref, lng_ref, lnb_ref, r0)
        u_ref[...] = project().astype(u_ref.dtype)

    @pl.when(j >= first_branch_step + n_branch_steps)
    def _():
        u_ref[...] = project().astype(u_ref.dtype)


X_CHUNKS = 4


def _mixer_in(x, g, w, layer, conv_params, lru_params, *, seq_len):
    t, d = x.shape
    tm = TM_MIXER_IN
    n_tiles = t // tm
    assert seq_len % tm == 0 and tm % BRANCH_STEP_ROWS == 0 and BRANCH_STEP_ROWS % CONV_ROW_CHUNK == 0
    assert N_BRANCH_BLOCKS + tm // BRANCH_STEP_ROWS <= N_COL_BLOCKS and d % X_CHUNKS == 0
    dww, dwb, lng, lnb = conv_params
    cw, cb, wg, ba, bx, lam = lru_params
    hd = LRU_WIDTH // LRU_HEADS

    def w_block(i, j):
        blk = jnp.where(j == 2, W_BLOCK_LRU_X, jnp.where(jnp.logical_and(j > 2, j <= W_BLOCK_LRU_X), j - 1, j))
        return (layer, 0, blk)

    def x_chunk(c):
        def index(i, j):
            ahead = (j >= N_COL_BLOCKS - X_CHUNKS + c).astype(jnp.int32)
            return (jnp.minimum(i + ahead, n_tiles - 1), c)
        return pl.BlockSpec((tm, d // X_CHUNKS), index)

    const = lambda shape: pl.BlockSpec(shape, lambda i, j: (0,) * len(shape))
    return pl.pallas_call(
        functools.partial(_mixer_in_kernel, tiles_per_seq=seq_len // tm),
        out_shape=(jax.ShapeDtypeStruct((t, U_WIDTH), BF16),
                   jax.ShapeDtypeStruct((t, CONV_WIDTH), BF16),
                   jax.ShapeDtypeStruct((t, LRU_WIDTH), BF16)),
        grid=(n_tiles, N_COL_BLOCKS),
        in_specs=[
            *[x_chunk(c) for c in range(X_CHUNKS)],
            const((1, d)),
            pl.BlockSpec((None, d, COL_BLOCK), w_block),
            const((CONV_KERNEL, CONV_WIDTH)), const((1, CONV_WIDTH)), const((1, CONV_WIDTH)),
            const((1, CONV_WIDTH)),
            const((LRU_CONV_KERNEL, LRU_WIDTH)), const((1, LRU_WIDTH)),
            const((LRU_HEADS, hd, 2 * hd)), const((1, LRU_WIDTH)), const((1, LRU_WIDTH)), const((1, LRU_WIDTH)),
        ],
        out_specs=(
            pl.BlockSpec((tm, COL_BLOCK), lambda i, j: (i, jnp.maximum(j - N_BRANCH_BLOCKS, 0))),
            pl.BlockSpec((tm, CONV_WIDTH), lambda i, j: (i, 0)),
            pl.BlockSpec((tm, LRU_WIDTH), lambda i, j: (i, 0)),
        ),
        scratch_shapes=[
            pltpu.VMEM((tm, d), BF16),
            pltpu.VMEM((tm, CONV_WIDTH), F32),
            pltpu.VMEM((CONV_HALO + tm + SUBLANES, CONV_WIDTH), F32),
            pltpu.VMEM((LRU_HALO + tm, LRU_WIDTH), F32),
            pltpu.VMEM((SUBLANES, LRU_WIDTH), F32),
            pltpu.VMEM((BRANCH_STEP_ROWS, LRU_WIDTH), F32),
            pltpu.VMEM((BRANCH_STEP_ROWS, LRU_WIDTH), F32),
        ],
        compiler_params=_params("arbitrary", "arbitrary"),
        name="mixer_in",
    )(*([x] * X_CHUNKS), g, w, dww, dwb, lng, lnb, cw, cb, wg, ba, bx, lam)


SB_T = 128
SB_NARROW = 48
SB_EAGER = 3
SB_LOG_WEIGHT_FLOOR = -104.0
SB_DONE = -1e30


def _sb_split(log_1mb):
    hi = log_1mb.astype(BF16)
    lo = (log_1mb - hi.astype(F32)).astype(BF16)
    return jnp.concatenate([hi, lo], axis=1)


def _sb_attn_kernel(q_ref, k_ref, v_ref, uo_ref, o_ref, acc_ref, carry_ref):
    s_len = q_ref.shape[0]
    scale = HEAD_DIM ** -0.5
    t_sz, n_lo = SB_T, SB_NARROW
    n_tiles = s_len // t_sz
    causal = (lax.broadcasted_iota(jnp.int32, (t_sz, t_sz), 1)
              < lax.broadcasted_iota(jnp.int32, (t_sz, t_sz), 0))

    def tile(i):
        return slice(i * t_sz, (i + 1) * t_sz)

    def neg_scores(q, k):
        zn = lax.dot_general(q, k, (((1,), (1,)), ((), ())), preferred_element_type=F32) * (-scale)
        log_1mb = jnp.minimum(zn, 0.0) - jnp.log(1.0 + jnp.exp(-jnp.abs(zn)))
        return zn, log_1mb

    def eager():
        zns, ls, zn0s, l0s, lhs = [], [], [], [], []
        for r in range(n_tiles):
            q = q_ref[tile(r), :]
            first = max(r - 1, 0)
            zn, log_1mb = neg_scores(q, k_ref[first * t_sz:(r + 1) * t_sz, :])
            pieces = [_sb_split(jnp.where(causal, log_1mb[:, -t_sz:], 0.0))]
            if r >= 1:
                pieces.insert(0, _sb_split(log_1mb[:, :t_sz]))
            if r >= 2:
                zn0, log_1mb0 = neg_scores(q[:n_lo], k_ref[tile(r - 2), :])
                pieces.insert(0, _sb_split(log_1mb0))
                zn0s.append(zn0)
                l0s.append(log_1mb0)
            else:
                zn0s.append(None)
                l0s.append(None)
            lhs.append(jnp.concatenate(pieces, axis=0))
            zns.append(zn)
            ls.append(log_1mb)
        sums = [jnp.dot(x, uo_ref[...], preferred_element_type=F32) for x in lhs]
        m_lo = m_hi = None
        for r in range(n_tiles):
            zn, log_1mb = zns[r], ls[r]
            s2 = sums[r][-t_sz:]
            w2 = jnp.where(causal, jnp.exp((log_1mb[:, -t_sz:] - zn[:, -t_sz:]) + s2[:, :t_sz]), 0.0)
            carry = s2[:, t_sz:]
            ws = [w2.astype(BF16)]
            if r >= 1:
                s1 = sums[r][-2 * t_sz:-t_sz]
                w1 = jnp.exp((log_1mb[:, :t_sz] - zn[:, :t_sz]) + (carry + s1[:, :t_sz]))
                carry = carry + s1[:, t_sz:]
                ws.insert(0, w1.astype(BF16))
            first = max(r - 1, 0)
            acc = jnp.dot(jnp.concatenate(ws, axis=1), v_ref[first * t_sz:(r + 1) * t_sz, :],
                          preferred_element_type=F32)
            if r >= 2:
                s0 = sums[r][:n_lo]
                w0 = jnp.exp((l0s[r] - zn0s[r]) + (carry[:n_lo] + s0[:, :t_sz]))
                acc_lo = jnp.dot(w0.astype(BF16), v_ref[tile(r - 2), :], preferred_element_type=F32)
                acc = jnp.concatenate([acc[:n_lo] + acc_lo, acc[n_lo:]], axis=0)
                carry_lo = carry[:n_lo] + s0[:, t_sz:]
                carry_ref[r, 0:n_lo, :] = carry_lo
                carry_ref[r, n_lo:t_sz, :] = carry[n_lo:]
                m_lo = carry_lo if m_lo is None else jnp.maximum(m_lo, carry_lo)
                m_hi = carry[n_lo:] if m_hi is None else jnp.maximum(m_hi, carry[n_lo:])
            acc_ref[r] = acc
            o_ref[tile(r), :] = acc.astype(o_ref.dtype)
        return jnp.max(m_lo), jnp.max(m_hi)

    def rest_of_third_block():
        n_hi = t_sz - n_lo
        tiles = range(2, n_tiles)
        zns, ls, lhs = {}, {}, []
        for r in tiles:
            zns[r], ls[r] = neg_scores(q_ref[r * t_sz + n_lo:(r + 1) * t_sz, :], k_ref[tile(r - 2), :])
            lhs.append(_sb_split(ls[r]))
        sums = jnp.dot(jnp.concatenate(lhs, axis=0), uo_ref[...], preferred_element_type=F32)
        m = None
        for n, r in enumerate(tiles):
            sr = sums[n * n_hi:(n + 1) * n_hi]
            carry = carry_ref[r, n_lo:t_sz, :]
            w = jnp.exp((ls[r] - zns[r]) + (carry + sr[:, :t_sz]))
            acc_ref[r, n_lo:t_sz, :] += jnp.dot(w.astype(BF16), v_ref[tile(r - 2), :], preferred_element_type=F32)
            carry = carry + sr[:, t_sz:]
            carry_ref[r, n_lo:t_sz, :] = carry
            if r >= SB_EAGER:
                m = carry if m is None else jnp.maximum(m, carry)
        return jnp.max(m)

    def step(d):
        tiles = range(SB_EAGER, n_tiles)
        zns, ls, starts, lhs = {}, {}, {}, []
        for r in tiles:
            starts[r] = pl.multiple_of(jnp.maximum(r - d, 0) * t_sz, t_sz)
            zns[r], ls[r] = neg_scores(q_ref[tile(r), :], k_ref[pl.ds(starts[r], t_sz), :])
            lhs.append(_sb_split(ls[r]))
        sums = jnp.dot(jnp.concatenate(lhs, axis=0), uo_ref[...], preferred_element_type=F32)
        m = None
        for n, r in enumerate(tiles):
            sr = sums[n * t_sz:(n + 1) * t_sz]
            w = jnp.exp((ls[r] - zns[r]) + (carry_ref[r] + sr[:, :t_sz]))
            w = jnp.where(r >= d, w, 0.0)
            acc_ref[r] += jnp.dot(w.astype(BF16), v_ref[pl.ds(starts[r], t_sz), :], preferred_element_type=F32)
            carry = carry_ref[r] + sr[:, t_sz:]
            carry_ref[r] = carry
            left = jnp.where(r > d, carry, SB_DONE)
            m = left if m is None else jnp.maximum(m, left)
        return jnp.max(m)

    def more(c):
        d, m = c
        return jnp.logical_and(d <= n_tiles - 1, m > SB_LOG_WEIGHT_FLOOR)

    m_lo, m_hi = eager()
    m = lax.cond(m_hi > SB_LOG_WEIGHT_FLOOR,
                 lambda: jnp.maximum(m_lo, rest_of_third_block()), lambda: m_lo)
    lax.while_loop(more, lambda c: (c[0] + 1, step(c[0])), (jnp.int32(SB_EAGER), m))

    @pl.when(jnp.logical_or(m_hi > SB_LOG_WEIGHT_FLOOR, m > SB_LOG_WEIGHT_FLOOR))
    def _():
        for r in range(2, n_tiles):
            o_ref[tile(r), :] = acc_ref[r].astype(o_ref.dtype)


def _sb_attention(u3):
    b, s, _ = u3.shape
    dh = HEAD_DIM
    assert s % SB_T == 0 and s // SB_T > SB_EAGER and SB_NARROW % 16 == 0
    row = lax.broadcasted_iota(jnp.int32, (2 * SB_T, 2 * SB_T), 0) % SB_T
    col = lax.broadcasted_iota(jnp.int32, (2 * SB_T, 2 * SB_T), 1)
    uo = jnp.where((col >= SB_T) | (row > col), 1.0, 0.0).astype(BF16)

    def head_spec(col0):
        return pl.BlockSpec((None, s, dh), lambda bi, hi: (bi, 0, col0 // dh + hi))

    n_tiles = s // SB_T
    return pl.pallas_call(
        _sb_attn_kernel,
        out_shape=jax.ShapeDtypeStruct((b, s, ATTN_WIDTH), BF16),
        grid=(b, ATTN_HEADS),
        in_specs=[head_spec(UCOL_Q), head_spec(UCOL_K), head_spec(UCOL_V),
                  pl.BlockSpec((2 * SB_T, 2 * SB_T), lambda bi, hi: (0, 0))],
        out_specs=pl.BlockSpec((None, s, dh), lambda bi, hi: (bi, 0, hi)),
        scratch_shapes=[pltpu.VMEM((n_tiles, SB_T, dh), F32), pltpu.VMEM((n_tiles, SB_T, SB_T), F32)],
        compiler_params=_params("parallel", "parallel"),
        name="sb_attention",
    )(u3, u3, u3, uo)


def _out_proj_kernel(x_ref, yc_ref, ya_ref, yl_ref, gc_ref, ga0_ref, ga1_ref, gl_ref,
                     pw_ref, nc_ref, na_ref, nl_ref, w_ref, o_ref):
    def normed(y, n_ref):
        return y * _rms_scale(y) * n_ref[...]

    def gated(yn, gate_ref):
        return (yn * _silu(gate_ref[...].astype(F32))).astype(BF16)

    half = ATTN_WIDTH // 2
    y_conv = jnp.dot(yc_ref[...], pw_ref[...], preferred_element_type=F32)
    ya = normed(ya_ref[...].astype(F32), na_ref)
    parts = [
        (gated(normed(y_conv, nc_ref), gc_ref), 0),
        (jnp.concatenate([gated(ya[:, :half], ga0_ref), gated(ya[:, half:], ga1_ref)], axis=-1), CONV_WIDTH),
        (gated(normed(yl_ref[...].astype(F32), nl_ref), gl_ref), CONV_WIDTH + ATTN_WIDTH),
    ]
    acc = x_ref[...]
    for y, r0 in parts:
        acc = acc + jnp.dot(y, w_ref[r0:r0 + y.shape[1], :], preferred_element_type=F32)
    o_ref[...] = acc


def _out_proj(x, yc, ya, yl, u, pw, nc, na, nl, w, layer):
    t, d = x.shape
    tm = TM_OUT_PROJ
    gw = COL_BLOCK
    return pl.pallas_call(
        _out_proj_kernel,
        out_shape=jax.ShapeDtypeStruct((t, d), F32),
        grid=(t // tm,),
        in_specs=[
            pl.BlockSpec((tm, d), lambda i: (i, 0)),
            pl.BlockSpec((tm, CONV_WIDTH), lambda i: (i, 0)),
            pl.BlockSpec((tm, ATTN_WIDTH), lambda i: (i, 0)),
            pl.BlockSpec((tm, LRU_WIDTH), lambda i: (i, 0)),
            pl.BlockSpec((tm, gw), lambda i: (i, UCOL_CGATE // gw)),
            pl.BlockSpec((tm, gw), lambda i: (i, UCOL_AGATE // gw)),
            pl.BlockSpec((tm, gw), lambda i: (i, UCOL_AGATE // gw + 1)),
            pl.BlockSpec((tm, gw), lambda i: (i, UCOL_RGATE // gw)),
            pl.BlockSpec((CONV_WIDTH, CONV_WIDTH), lambda i: (0, 0)),
            pl.BlockSpec((1, CONV_WIDTH), lambda i: (0, 0)),
            pl.BlockSpec((1, ATTN_WIDTH), lambda i: (0, 0)),
            pl.BlockSpec((1, LRU_WIDTH), lambda i: (0, 0)),
            pl.BlockSpec((None, d, d), lambda i: (layer, 0, 0)),
        ],
        out_specs=pl.BlockSpec((tm, d), lambda i: (i, 0)),
        compiler_params=_params("parallel"),
        name="out_proj",
    )(x, yc, ya, yl, u, u, u, u, pw, nc, na, nl, w)


def _xattn_kernel(x_ref, g_ref, wq_ref, k_ref, v_ref, wo_ref, fg_ref, o_ref, *, final_norm):
    x = x_ref[...]
    h = (x * _rms_scale(x) * g_ref[...]).astype(BF16)
    q = jnp.dot(h, wq_ref[...], preferred_element_type=F32).astype(BF16)
    dh = XATTN_WIDTH // XATTN_HEADS
    scale = dh ** -0.5
    heads = []
    for n in range(XATTN_HEADS):
        qh = q[:, n * dh:(n + 1) * dh]
        kh = k_ref[:, n * dh:(n + 1) * dh]
        vh = v_ref[:, n * dh:(n + 1) * dh]
        s = lax.dot_general(qh, kh, (((1,), (1,)), ((), ())), preferred_element_type=F32) * scale
        e = jnp.exp(s - jnp.max(s, axis=-1, keepdims=True))
        p = (e / jnp.sum(e, axis=-1, keepdims=True)).astype(BF16)
        heads.append(jnp.dot(p, vh, preferred_element_type=F32).astype(BF16))
    acc = x + jnp.dot(jnp.concatenate(heads, axis=-1), wo_ref[...], preferred_element_type=F32)
    if final_norm:
        acc = acc * _rms_scale(acc) * fg_ref[...]
    o_ref[...] = acc


def _xattn(x3, g, wq, kv, wo, fg, layer, *, final_norm):
    b, s, d = x3.shape
    m = kv.shape[1]
    tm = TM_XATTN
    xw = XATTN_WIDTH
    return pl.pallas_call(
        functools.partial(_xattn_kernel, final_norm=final_norm),
        out_shape=jax.ShapeDtypeStruct((b, s, d), F32),
        grid=(b, s // tm),
        in_specs=[
            pl.BlockSpec((None, tm, d), lambda bi, ti: (bi, ti, 0)),
            pl.BlockSpec((1, d), lambda bi, ti: (0, 0)),
            pl.BlockSpec((None, d, xw), lambda bi, ti: (layer, 0, 0)),
            pl.BlockSpec((None, m, xw), lambda bi, ti: (bi, 0, 0)),
            pl.BlockSpec((None, m, xw), lambda bi, ti: (bi, 0, 1)),
            pl.BlockSpec((None, xw, d), lambda bi, ti: (layer, 0, 0)),
            pl.BlockSpec((1, d), lambda bi, ti: (0, 0)),
        ],
        out_specs=pl.BlockSpec((None, tm, d), lambda bi, ti: (bi, ti, 0)),
        compiler_params=_params("parallel", "parallel"),
        name="xattn_final" if final_norm else "xattn",
    )(x3, g, wq, kv, kv, wo, fg)


def kernel(x, mem, mix_norm_g, w_in, conv_dw_w, conv_dw_b, conv_ln_g, conv_ln_b, conv_pw_w,
           lru_conv_w, lru_conv_b, lru_wa, lru_ba, lru_wx, lru_bx, lru_lambda,
           out_norm_conv, out_norm_attn, out_norm_lru, w_out,
           xattn_norm_g, mem_norm_g, xattn_wq, xattn_wkv, xattn_wo, final_norm_g):
    b, s, d = x.shape
    m = mem.shape[1]
    depth = w_in.shape[0]
    t = b * s
    row = lambda a: a.reshape(1, -1).astype(F32)

    xt = x.reshape(t, d)
    memt = mem.reshape(b * m, d)
    w_in_b, w_out_b = w_in.astype(BF16), w_out.astype(BF16)
    wq_b, wkv_b, wo_b = xattn_wq.astype(BF16), xattn_wkv.astype(BF16), xattn_wo.astype(BF16)
    for l in range(depth):
        conv_params = (conv_dw_w[l], row(conv_dw_b[l]), row(conv_ln_g[l]), row(conv_ln_b[l]))
        lru_gates = jnp.concatenate([lru_wa[l], lru_wx[l]], axis=-1).astype(BF16)
        lru_params = (lru_conv_w[l], row(lru_conv_b[l]), lru_gates, row(lru_ba[l]), row(lru_bx[l]),
                      row(lru_lambda[l]))
        u, conv_act, y_lru = _mixer_in(xt, row(mix_norm_g[l]), w_in_b, l, conv_params, lru_params, seq_len=s)
        y_attn = _sb_attention(u.reshape(b, s, U_WIDTH))
        xt = _out_proj(xt, conv_act, y_attn.reshape(t, -1), y_lru, u, conv_pw_w[l].astype(BF16),
                       row(out_norm_conv[l]), row(out_norm_attn[l]), row(out_norm_lru[l]), w_out_b, l)
        kv = _norm_matmul(memt, row(mem_norm_g[l]), wkv_b, l, tm=256, tn=1024,
                          name="mem_kv").reshape(b, m, 2 * XATTN_WIDTH)
        xt = _xattn(xt.reshape(b, s, d), row(xattn_norm_g[l]), wq_b, kv, wo_b, row(final_norm_g), l,
                    final_norm=(l == depth - 1)).reshape(t, d)
    return xt.reshape(b, s, d)
```

```python
import functools

import jax
import jax.numpy as jnp
from jax import lax
from jax.experimental import pallas as pl
from jax.experimental.pallas import tpu as pltpu

F32 = jnp.float32
BF16 = jnp.bfloat16

D_MODEL = 2048
CONV_WIDTH = 512
CONV_KERNEL = 31
HEAD_DIM = 128
ATTN_WIDTH = 1024
ATTN_HEADS = 8
LRU_WIDTH = 512
LRU_HEADS = 4
LRU_CONV_KERNEL = 4
LRU_C = 8.0
XATTN_HEADS = 4
XATTN_WIDTH = 512
IN_WIDTH = 3 * CONV_WIDTH + 4 * ATTN_WIDTH + 2 * LRU_WIDTH

COL_BLOCK = 512
N_COL_BLOCKS = IN_WIDTH // COL_BLOCK
W_BLOCK_LRU_X = 11
N_BRANCH_BLOCKS = 3
U_WIDTH = IN_WIDTH - N_BRANCH_BLOCKS * COL_BLOCK
UCOL_CGATE, UCOL_Q, UCOL_K, UCOL_V, UCOL_AGATE, UCOL_RGATE = 0, 512, 1536, 2560, 3584, 4608

VMEM_LIMIT_BYTES = 56 * 1024 * 1024
SUBLANES = 8
LANES = 128

TM_MIXER_IN = 1024
TM_OUT_PROJ = 512
TM_XATTN = 1024

RMS_EPS = 1e-6
LN_EPS = 1e-5


def _params(*sem):
    return pltpu.CompilerParams(dimension_semantics=sem, vmem_limit_bytes=VMEM_LIMIT_BYTES)


def _rms_scale(x):
    return lax.rsqrt(jnp.mean(x * x, axis=-1, keepdims=True) + RMS_EPS)


def _softplus(x):
    return jnp.maximum(x, 0.0) + jnp.log(1.0 + jnp.exp(-jnp.abs(x)))


def _silu(x):
    return x * jax.nn.sigmoid(x)


def _norm_matmul_kernel(x_ref, g_ref, w_ref, o_ref, h_ref):
    @pl.when(pl.program_id(1) == 0)
    def _():
        x = x_ref[...]
        h_ref[...] = (x * _rms_scale(x) * g_ref[...]).astype(BF16)

    o_ref[...] = jnp.dot(h_ref[...], w_ref[...], preferred_element_type=F32).astype(o_ref.dtype)


def _norm_matmul(x, g, w, layer, *, tm, tn, name):
    t, d = x.shape
    n = w.shape[2]
    return pl.pallas_call(
        _norm_matmul_kernel,
        out_shape=jax.ShapeDtypeStruct((t, n), BF16),
        grid=(t // tm, n // tn),
        in_specs=[
            pl.BlockSpec((tm, d), lambda i, j: (i, 0)),
            pl.BlockSpec((1, d), lambda i, j: (0, 0)),
            pl.BlockSpec((None, d, tn), lambda i, j: (layer, 0, j)),
        ],
        out_specs=pl.BlockSpec((tm, tn), lambda i, j: (i, j)),
        scratch_shapes=[pltpu.VMEM((tm, d), BF16)],
        compiler_params=_params("parallel", "arbitrary"),
        name=name,
    )(x, g, w)


CONV_HALO = 32
CONV_ROW_CHUNK = 128
BRANCH_STEP_ROWS = 128
LRU_HALO = SUBLANES


def _depthwise_conv_rows(buf_ref, dww_ref, dwb_ref, r0):
    base = CONV_HALO - (CONV_KERNEL - 1)
    rows = CONV_ROW_CHUNK + SUBLANES
    n_m = (base + CONV_KERNEL - 1) // SUBLANES + 1
    out = []
    for lb in range(CONV_WIDTH // LANES):
        lanes = slice(lb * LANES, (lb + 1) * LANES)
        window = buf_ref[pl.ds(r0, rows + SUBLANES * (n_m - 1)), lanes]
        a = None
        for rho in reversed(range(SUBLANES)):
            q = None
            for m in range(n_m):
                k = SUBLANES * m + rho - base
                if 0 <= k < CONV_KERNEL:
                    term = dww_ref[k:k + 1, lanes] * window[SUBLANES * m:SUBLANES * m + rows]
                    q = term if q is None else q + term
            a = q if a is None else q + pltpu.roll(a, rows - 1, 0)
        out.append(a[:CONV_ROW_CHUNK] + dwb_ref[:, lanes])
    return jnp.concatenate(out, axis=1)


def _conv_branch_rows(cbuf_ref, dww_ref, dwb_ref, lng_ref, lnb_ref, r0):
    u = jnp.concatenate([_depthwise_conv_rows(cbuf_ref, dww_ref, dwb_ref, r0 + c * CONV_ROW_CHUNK)
                         for c in range(BRANCH_STEP_ROWS // CONV_ROW_CHUNK)], axis=0)
    mu = jnp.mean(u, axis=-1, keepdims=True)
    uc = u - mu
    var = jnp.mean(uc * uc, axis=-1, keepdims=True)
    y = uc * lax.rsqrt(var + LN_EPS) * lng_ref[...] + lnb_ref[...]
    return _silu(y).astype(BF16)


def _lru_branch_rows(rbuf_ref, cw_ref, cb_ref, wg_ref, ba_ref, bx_ref, lam_ref, hstate_ref, a_ref, b_ref,
                     yl_ref, r0):
    w = LRU_WIDTH
    n = BRANCH_STEP_ROWS
    window = rbuf_ref[pl.ds(r0, n + LRU_HALO), :]
    xc = jnp.broadcast_to(cb_ref[...], (n, w))
    for k in range(LRU_CONV_KERNEL):
        shift = LRU_HALO - (LRU_CONV_KERNEL - 1) + k
        if shift % SUBLANES == 0:
            tap = window[shift:shift + n]
        else:
            tap = pltpu.roll(window, n + LRU_HALO - shift, 0)[:n]
        xc = xc + cw_ref[k:k + 1, :] * tap

    xcb = xc.astype(BF16)
    hd = w // LRU_HEADS
    pre = [jnp.dot(xcb[:, hix * hd:(hix + 1) * hd], wg_ref[hix], preferred_element_type=F32)
           for hix in range(LRU_HEADS)]
    r = jax.nn.sigmoid(jnp.concatenate([p[:, :hd] for p in pre], axis=-1) + ba_ref[...])
    gate = jax.nn.sigmoid(jnp.concatenate([p[:, hd:] for p in pre], axis=-1) + bx_ref[...])
    log_a = (-LRU_C) * r * _softplus(-lam_ref[...])
    a = jnp.exp(log_a)
    a_ref[...] = a
    b_ref[...] = jnp.sqrt(jnp.tanh(-log_a) * (1.0 + a * a)) * (gate * xc)

    row = lax.broadcasted_iota(jnp.int32, (SUBLANES, w), 0)
    h_prev = hstate_ref[...]
    for g in range(n // SUBLANES):
        rows = slice(g * SUBLANES, (g + 1) * SUBLANES)
        ag = a_ref[rows, :]
        bg = b_ref[rows, :]
        d = 1
        while d < SUBLANES:
            a_sh = pltpu.roll(ag, d, 0)
            b_sh = pltpu.roll(bg, d, 0)
            m = row >= d
            bg = jnp.where(m, ag * b_sh + bg, bg)
            ag = jnp.where(m, ag * a_sh, ag)
            d *= 2
        h = ag * h_prev + bg
        b_ref[rows, :] = h
        h_prev = jnp.broadcast_to(h[SUBLANES - 1:SUBLANES, :], (SUBLANES, w))
    hstate_ref[...] = h_prev
    yl_ref[pl.ds(r0, n), :] = b_ref[...].astype(yl_ref.dtype)


def _mixer_in_kernel(x0_ref, x1_ref, x2_ref, x3_ref, g_ref, w_ref,
                     dww_ref, dwb_ref, lng_ref, lnb_ref,
                     cw_ref, cb_ref, wg_ref, ba_ref, bx_ref, lam_ref,
                     u_ref, yc_ref, yl_ref,
                     h_ref, val_ref, cbuf_ref, rbuf_ref, hstate_ref, a_ref, b_ref, *, tiles_per_seq):
    i = pl.program_id(0)
    j = pl.program_id(1)
    x_refs = (x0_ref, x1_ref, x2_ref, x3_ref)
    tm = x0_ref.shape[0]
    first_branch_step = N_BRANCH_BLOCKS
    n_branch_steps = tm // BRANCH_STEP_ROWS

    def project():
        return jnp.dot(h_ref[...], w_ref[...], preferred_element_type=F32)

    @pl.when(j == 0)
    def _():
        @pl.when(i % tiles_per_seq == 0)
        def _():
            cbuf_ref[0:CONV_HALO, :] = jnp.zeros((CONV_HALO, CONV_WIDTH), F32)
            rbuf_ref[0:LRU_HALO, :] = jnp.zeros((LRU_HALO, LRU_WIDTH), F32)
            hstate_ref[...] = jnp.zeros_like(hstate_ref)
            cbuf_ref[CONV_HALO + tm:CONV_HALO + tm + SUBLANES, :] = jnp.zeros((SUBLANES, CONV_WIDTH), F32)

        @pl.when(i % tiles_per_seq != 0)
        def _():
            cbuf_ref[0:CONV_HALO, :] = cbuf_ref[tm:tm + CONV_HALO, :]
            rbuf_ref[0:LRU_HALO, :] = rbuf_ref[tm:tm + LRU_HALO, :]

        d = h_ref.shape[1]
        cw = d // len(x_refs)
        sq = None
        for xr in x_refs:
            xc = xr[...]
            part = jnp.sum(xc * xc, axis=-1, keepdims=True)
            sq = part if sq is None else sq + part
        inv = lax.rsqrt(sq * (1.0 / d) + RMS_EPS)
        for c, xr in enumerate(x_refs):
            h_ref[:, c * cw:(c + 1) * cw] = (xr[...] * inv * g_ref[:, c * cw:(c + 1) * cw]).astype(BF16)
        val_ref[...] = project()

    @pl.when(j == 1)
    def _():
        cbuf_ref[CONV_HALO:CONV_HALO + tm, :] = val_ref[...] * jax.nn.sigmoid(project())

    @pl.when(j == 2)
    def _():
        rbuf_ref[LRU_HALO:LRU_HALO + tm, :] = project()

    @pl.when(jnp.logical_and(j >= first_branch_step, j < first_branch_step + n_branch_steps))
    def _():
        r0 = pl.multiple_of((j - first_branch_step) * BRANCH_STEP_ROWS, BRANCH_STEP_ROWS)
        _lru_branch_rows(rbuf_ref, cw_ref, cb_ref, wg_ref, ba_ref, bx_ref, lam_ref, hstate_ref, a_ref, b_ref,
                         yl_ref, r0)
        yc_ref[pl.ds(r0, BRANCH_STEP_ROWS), :] = _conv_branch_rows(cbuf_ref, dww_ref, dwb_ref, lng_ref, lnb_ref, r0)
        u_ref[...] = project().astype(u_ref.dtype)

    @pl.when(j >= first_branch_step + n_branch_steps)
    def _():
        u_ref[...] = project().astype(u_ref.dtype)


X_CHUNKS = 4


def _mixer_in(x, g, w, layer, conv_params, lru_params, *, seq_len):
    t, d = x.shape
    tm = TM_MIXER_IN
    n_tiles = t // tm
    assert seq_len % tm == 0 and tm % BRANCH_STEP_ROWS == 0 and BRANCH_STEP_ROWS % CONV_ROW_CHUNK == 0
    assert N_BRANCH_BLOCKS + tm // BRANCH_STEP_ROWS <= N_COL_BLOCKS and d % X_CHUNKS == 0
    dww, dwb, lng, lnb = conv_params
    cw, cb, wg, ba, bx, lam = lru_params
    hd = LRU_WIDTH // LRU_HEADS

    def w_block(i, j):
        blk = jnp.where(j == 2, W_BLOCK_LRU_X, jnp.where(jnp.logical_and(j > 2, j <= W_BLOCK_LRU_X), j - 1, j))
        return (layer, 0, blk)

    def x_chunk(c):
        def index(i, j):
            ahead = (j >= N_COL_BLOCKS - X_CHUNKS + c).astype(jnp.int32)
            return (jnp.minimum(i + ahead, n_tiles - 1), c)
        return pl.BlockSpec((tm, d // X_CHUNKS), index)

    const = lambda shape: pl.BlockSpec(shape, lambda i, j: (0,) * len(shape))
    return pl.pallas_call(
        functools.partial(_mixer_in_kernel, tiles_per_seq=seq_len // tm),
        out_shape=(jax.ShapeDtypeStruct((t, U_WIDTH), BF16),
                   jax.ShapeDtypeStruct((t, CONV_WIDTH), BF16),
                   jax.ShapeDtypeStruct((t, LRU_WIDTH), BF16)),
        grid=(n_tiles, N_COL_BLOCKS),
        in_specs=[
            *[x_chunk(c) for c in range(X_CHUNKS)],
            const((1, d)),
            pl.BlockSpec((None, d, COL_BLOCK), w_block),
            const((CONV_KERNEL, CONV_WIDTH)), const((1, CONV_WIDTH)), const((1, CONV_WIDTH)),
            const((1, CONV_WIDTH)),
            const((LRU_CONV_KERNEL, LRU_WIDTH)), const((1, LRU_WIDTH)),
            const((LRU_HEADS, hd, 2 * hd)), const((1, LRU_WIDTH)), const((1, LRU_WIDTH)), const((1, LRU_WIDTH)),
        ],
        out_specs=(
            pl.BlockSpec((tm, COL_BLOCK), lambda i, j: (i, jnp.maximum(j - N_BRANCH_BLOCKS, 0))),
            pl.BlockSpec((tm, CONV_WIDTH), lambda i, j: (i, 0)),
            pl.BlockSpec((tm, LRU_WIDTH), lambda i, j: (i, 0)),
        ),
        scratch_shapes=[
            pltpu.VMEM((tm, d), BF16),
            pltpu.VMEM((tm, CONV_WIDTH), F32),
            pltpu.VMEM((CONV_HALO + tm + SUBLANES, CONV_WIDTH), F32),
            pltpu.VMEM((LRU_HALO + tm, LRU_WIDTH), F32),
            pltpu.VMEM((SUBLANES, LRU_WIDTH), F32),
            pltpu.VMEM((BRANCH_STEP_ROWS, LRU_WIDTH), F32),
            pltpu.VMEM((BRANCH_STEP_ROWS, LRU_WIDTH), F32),
        ],
        compiler_params=_params("arbitrary", "arbitrary"),
        name="mixer_in",
    )(*([x] * X_CHUNKS), g, w, dww, dwb, lng, lnb, cw, cb, wg, ba, bx, lam)


SB_T = 128
SB_NARROW = 48
SB_EAGER = 3
SB_LOG_WEIGHT_FLOOR = -104.0
SB_DONE = -1e30


def _sb_split(log_1mb):
    hi = log_1mb.astype(BF16)
    lo = (log_1mb - hi.astype(F32)).astype(BF16)
    return jnp.concatenate([hi, lo], axis=1)


def _sb_attn_kernel(q_ref, k_ref, v_ref, uo_ref, o_ref, acc_ref, carry_ref):
    s_len = q_ref.shape[0]
    scale = HEAD_DIM ** -0.5
    t_sz, n_lo = SB_T, SB_NARROW
    n_tiles = s_len // t_sz
    causal = (lax.broadcasted_iota(jnp.int32, (t_sz, t_sz), 1)
              < lax.broadcasted_iota(jnp.int32, (t_sz, t_sz), 0))

    def tile(i):
        return slice(i * t_sz, (i + 1) * t_sz)

    def neg_scores(q, k):
        zn = lax.dot_general(q, k, (((1,), (1,)), ((), ())), preferred_element_type=F32) * (-scale)
        log_1mb = jnp.minimum(zn, 0.0) - jnp.log(1.0 + jnp.exp(-jnp.abs(zn)))
        return zn, log_1mb

    def eager():
        zns, ls, zn0s, l0s, lhs = [], [], [], [], []
        for r in range(n_tiles):
            q = q_ref[tile(r), :]
            first = max(r - 1, 0)
            zn, log_1mb = neg_scores(q, k_ref[first * t_sz:(r + 1) * t_sz, :])
            pieces = [_sb_split(jnp.where(causal, log_1mb[:, -t_sz:], 0.0))]
            if r >= 1:
                pieces.insert(0, _sb_split(log_1mb[:, :t_sz]))
            if r >= 2:
                zn0, log_1mb0 = neg_scores(q[:n_lo], k_ref[tile(r - 2), :])
                pieces.insert(0, _sb_split(log_1mb0))
                zn0s.append(zn0)
                l0s.append(log_1mb0)
            else:
                zn0s.append(None)
                l0s.append(None)
            lhs.append(jnp.concatenate(pieces, axis=0))
            zns.append(zn)
            ls.append(log_1mb)
        sums = [jnp.dot(x, uo_ref[...], preferred_element_type=F32) for x in lhs]
        m_lo = m_hi = None
        for r in range(n_tiles):
            zn, log_1mb = zns[r], ls[r]
            s2 = sums[r][-t_sz:]
            w2 = jnp.where(causal, jnp.exp((log_1mb[:, -t_sz:] - zn[:, -t_sz:]) + s2[:, :t_sz]), 0.0)
            carry = s2[:, t_sz:]
            ws = [w2.astype(BF16)]
            if r >= 1:
                s1 = sums[r][-2 * t_sz:-t_sz]
                w1 = jnp.exp((log_1mb[:, :t_sz] - zn[:, :t_sz]) + (carry + s1[:, :t_sz]))
                carry = carry + s1[:, t_sz:]
                ws.insert(0, w1.astype(BF16))
            first = max(r - 1, 0)
            acc = jnp.dot(jnp.concatenate(ws, axis=1), v_ref[first * t_sz:(r + 1) * t_sz, :],
                          preferred_element_type=F32)
            if r >= 2:
                s0 = sums[r][:n_lo]
                w0 = jnp.exp((l0s[r] - zn0s[r]) + (carry[:n_lo] + s0[:, :t_sz]))
                acc_lo = jnp.dot(w0.astype(BF16), v_ref[tile(r - 2), :], preferred_element_type=F32)
                acc = jnp.concatenate([acc[:n_lo] + acc_lo, acc[n_lo:]], axis=0)
                carry_lo = carry[:n_lo] + s0[:, t_sz:]
                carry_ref[r, 0:n_lo, :] = carry_lo
                carry_ref[r, n_lo:t_sz, :] = carry[n_lo:]
                m_lo = carry_lo if m_lo is None else jnp.maximum(m_lo, carry_lo)
                m_hi = carry[n_lo:] if m_hi is None else jnp.maximum(m_hi, carry[n_lo:])
            acc_ref[r] = acc
            o_ref[tile(r), :] = acc.astype(o_ref.dtype)
        return jnp.max(m_lo), jnp.max(m_hi)

    def rest_of_third_block():
        n_hi = t_sz - n_lo
        tiles = range(2, n_tiles)
        zns, ls, lhs = {}, {}, []
        for r in tiles:
            zns[r], ls[r] = neg_scores(q_ref[r * t_sz + n_lo:(r + 1) * t_sz, :], k_ref[tile(r - 2), :])
            lhs.append(_sb_split(ls[r]))
        sums = jnp.dot(jnp.concatenate(lhs, axis=0), uo_ref[...], preferred_element_type=F32)
        m = None
        for n, r in enumerate(tiles):
            sr = sums[n * n_hi:(n + 1) * n_hi]
            carry = carry_ref[r, n_lo:t_sz, :]
            w = jnp.exp((ls[r] - zns[r]) + (carry + sr[:, :t_sz]))
            acc_ref[r, n_lo:t_sz, :] += jnp.dot(w.astype(BF16), v_ref[tile(r - 2), :], preferred_element_type=F32)
            carry = carry + sr[:, t_sz:]
            carry_ref[r, n_lo:t_sz, :] = carry
            if r >= SB_EAGER:
                m = carry if m is None else jnp.maximum(m, carry)
        return jnp.max(m)

    def step(d):
        tiles = range(SB_EAGER, n_tiles)
        zns, ls, starts, lhs = {}, {}, {}, []
        for r in tiles:
            starts[r] = pl.multiple_of(jnp.maximum(r - d, 0) * t_sz, t_sz)
            zns[r], ls[r] = neg_scores(q_ref[tile(r), :], k_ref[pl.ds(starts[r], t_sz), :])
            lhs.append(_sb_split(ls[r]))
        sums = jnp.dot(jnp.concatenate(lhs, axis=0), uo_ref[...], preferred_element_type=F32)
        m = None
        for n, r in enumerate(tiles):
            sr = sums[n * t_sz:(n + 1) * t_sz]
            w = jnp.exp((ls[r] - zns[r]) + (carry_ref[r] + sr[:, :t_sz]))
            w = jnp.where(r >= d, w, 0.0)
            acc_ref[r] += jnp.dot(w.astype(BF16), v_ref[pl.ds(starts[r], t_sz), :], preferred_element_type=F32)
            carry = carry_ref[r] + sr[:, t_sz:]
            carry_ref[r] = carry
            left = jnp.where(r > d, carry, SB_DONE)
            m = left if m is None else jnp.maximum(m, left)
        return jnp.max(m)

    def more(c):
        d, m = c
        return jnp.logical_and(d <= n_tiles - 1, m > SB_LOG_WEIGHT_FLOOR)

    m_lo, m_hi = eager()
    m = lax.cond(m_hi > SB_LOG_WEIGHT_FLOOR,
                 lambda: jnp.maximum(m_lo, rest_of_third_block()), lambda: m_lo)
    lax.while_loop(more, lambda c: (c[0] + 1, step(c[0])), (jnp.int32(SB_EAGER), m))

    @pl.when(jnp.logical_or(m_hi > SB_LOG_WEIGHT_FLOOR, m > SB_LOG_WEIGHT_FLOOR))
    def _():
        for r in range(2, n_tiles):
            o_ref[tile(r), :] = acc_ref[r].astype(o_ref.dtype)


def _sb_attention(u3):
    b, s, _ = u3.shape
    dh = HEAD_DIM
    assert s % SB_T == 0 and s // SB_T > SB_EAGER and SB_NARROW % 16 == 0
    row = lax.broadcasted_iota(jnp.int32, (2 * SB_T, 2 * SB_T), 0) % SB_T
    col = lax.broadcasted_iota(jnp.int32, (2 * SB_T, 2 * SB_T), 1)
    uo = jnp.where((col >= SB_T) | (row > col), 1.0, 0.0).astype(BF16)

    def head_spec(col0):
        return pl.BlockSpec((None, s, dh), lambda bi, hi: (bi, 0, col0 // dh + hi))

    n_tiles = s // SB_T
    return pl.pallas_call(
        _sb_attn_kernel,
        out_shape=jax.ShapeDtypeStruct((b, s, ATTN_WIDTH), BF16),
        grid=(b, ATTN_HEADS),
        in_specs=[head_spec(UCOL_Q), head_spec(UCOL_K), head_spec(UCOL_V),
                  pl.BlockSpec((2 * SB_T, 2 * SB_T), lambda bi, hi: (0, 0))],
        out_specs=pl.BlockSpec((None, s, dh), lambda bi, hi: (bi, 0, hi)),
        scratch_shapes=[pltpu.VMEM((n_tiles, SB_T, dh), F32), pltpu.VMEM((n_tiles, SB_T, SB_T), F32)],
        compiler_params=_params("parallel", "parallel"),
        name="sb_attention",
    )(u3, u3, u3, uo)


def _out_proj_kernel(x_ref, yc_ref, ya_ref, yl_ref, gc_ref, ga0_ref, ga1_ref, gl_ref,
                     pw_ref, nc_ref, na_ref, nl_ref, w_ref, o_ref):
    def normed(y, n_ref):
        return y * _rms_scale(y) * n_ref[...]

    def gated(yn, gate_ref):
        return (yn * _silu(gate_ref[...].astype(F32))).astype(BF16)

    half = ATTN_WIDTH // 2
    y_conv = jnp.dot(yc_ref[...], pw_ref[...], preferred_element_type=F32)
    ya = normed(ya_ref[...].astype(F32), na_ref)
    parts = [
        (gated(normed(y_conv, nc_ref), gc_ref), 0),
        (jnp.concatenate([gated(ya[:, :half], ga0_ref), gated(ya[:, half:], ga1_ref)], axis=-1), CONV_WIDTH),
        (gated(normed(yl_ref[...].astype(F32), nl_ref), gl_ref), CONV_WIDTH + ATTN_WIDTH),
    ]
    acc = x_ref[...]
    for y, r0 in parts:
        acc = acc + jnp.dot(y, w_ref[r0:r0 + y.shape[1], :], preferred_element_type=F32)
    o_ref[...] = acc


def _out_proj(x, yc, ya, yl, u, pw, nc, na, nl, w, layer):
    t, d = x.shape
    tm = TM_OUT_PROJ
    gw = COL_BLOCK
    return pl.pallas_call(
        _out_proj_kernel,
        out_shape=jax.ShapeDtypeStruct((t, d), F32),
        grid=(t // tm,),
        in_specs=[
            pl.BlockSpec((tm, d), lambda i: (i, 0)),
            pl.BlockSpec((tm, CONV_WIDTH), lambda i: (i, 0)),
            pl.BlockSpec((tm, ATTN_WIDTH), lambda i: (i, 0)),
            pl.BlockSpec((tm, LRU_WIDTH), lambda i: (i, 0)),
            pl.BlockSpec((tm, gw), lambda i: (i, UCOL_CGATE // gw)),
            pl.BlockSpec((tm, gw), lambda i: (i, UCOL_AGATE // gw)),
            pl.BlockSpec((tm, gw), lambda i: (i, UCOL_AGATE // gw + 1)),
            pl.BlockSpec((tm, gw), lambda i: (i, UCOL_RGATE // gw)),
            pl.BlockSpec((CONV_WIDTH, CONV_WIDTH), lambda i: (0, 0)),
            pl.BlockSpec((1, CONV_WIDTH), lambda i: (0, 0)),
            pl.BlockSpec((1, ATTN_WIDTH), lambda i: (0, 0)),
            pl.BlockSpec((1, LRU_WIDTH), lambda i: (0, 0)),
            pl.BlockSpec((None, d, d), lambda i: (layer, 0, 0)),
        ],
        out_specs=pl.BlockSpec((tm, d), lambda i: (i, 0)),
        compiler_params=_params("parallel"),
        name="out_proj",
    )(x, yc, ya, yl, u, u, u, u, pw, nc, na, nl, w)


def _xattn_kernel(x_ref, g_ref, wq_ref, k_ref, v_ref, wo_ref, fg_ref, o_ref, *, final_norm):
    x = x_ref[...]
    h = (x * _rms_scale(x) * g_ref[...]).astype(BF16)
    q = jnp.dot(h, wq_ref[...], preferred_element_type=F32).astype(BF16)
    dh = XATTN_WIDTH // XATTN_HEADS
    scale = dh ** -0.5
    heads = []
    for n in range(XATTN_HEADS):
        qh = q[:, n * dh:(n + 1) * dh]
        kh = k_ref[:, n * dh:(n + 1) * dh]
        vh = v_ref[:, n * dh:(n + 1) * dh]
        s = lax.dot_general(qh, kh, (((1,), (1,)), ((), ())), preferred_element_type=F32) * scale
        e = jnp.exp(s - jnp.max(s, axis=-1, keepdims=True))
        p = (e / jnp.sum(e, axis=-1, keepdims=True)).astype(BF16)
        heads.append(jnp.dot(p, vh, preferred_element_type=F32).astype(BF16))
    acc = x + jnp.dot(jnp.concatenate(heads, axis=-1), wo_ref[...], preferred_element_type=F32)
    if final_norm:
        acc = acc * _rms_scale(acc) * fg_ref[...]
    o_ref[...] = acc


def _xattn(x3, g, wq, kv, wo, fg, layer, *, final_norm):
    b, s, d = x3.shape
    m = kv.shape[1]
    tm = TM_XATTN
    xw = XATTN_WIDTH
    return pl.pallas_call(
        functools.partial(_xattn_kernel, final_norm=final_norm),
        out_shape=jax.ShapeDtypeStruct((b, s, d), F32),
        grid=(b, s // tm),
        in_specs=[
            pl.BlockSpec((None, tm, d), lambda bi, ti: (bi, ti, 0)),
            pl.BlockSpec((1, d), lambda bi, ti: (0, 0)),
            pl.BlockSpec((None, d, xw), lambda bi, ti: (layer, 0, 0)),
            pl.BlockSpec((None, m, xw), lambda bi, ti: (bi, 0, 0)),
            pl.BlockSpec((None, m, xw), lambda bi, ti: (bi, 0, 1)),
            pl.BlockSpec((None, xw, d), lambda bi, ti: (layer, 0, 0)),
            pl.BlockSpec((1, d), lambda bi, ti: (0, 0)),
        ],
        out_specs=pl.BlockSpec((None, tm, d), lambda bi, ti: (bi, ti, 0)),
        compiler_params=_params("parallel", "parallel"),
        name="xattn_final" if final_norm else "xattn",
    )(x3, g, wq, kv, kv, wo, fg)


def kernel(x, mem, mix_norm_g, w_in, conv_dw_w, conv_dw_b, conv_ln_g, conv_ln_b, conv_pw_w,
           lru_conv_w, lru_conv_b, lru_wa, lru_ba, lru_wx, lru_bx, lru_lambda,
           out_norm_conv, out_norm_attn, out_norm_lru, w_out,
           xattn_norm_g, mem_norm_g, xattn_wq, xattn_wkv, xattn_wo, final_norm_g):
    b, s, d = x.shape
    m = mem.shape[1]
    depth = w_in.shape[0]
    t = b * s
    row = lambda a: a.reshape(1, -1).astype(F32)

    xt = x.reshape(t, d)
    memt = mem.reshape(b * m, d)
    w_in_b, w_out_b = w_in.astype(BF16), w_out.astype(BF16)
    wq_b, wkv_b, wo_b = xattn_wq.astype(BF16), xattn_wkv.astype(BF16), xattn_wo.astype(BF16)
    for l in range(depth):
        conv_params = (conv_dw_w[l], row(conv_dw_b[l]), row(conv_ln_g[l]), row(conv_ln_b[l]))
        lru_gates = jnp.concatenate([lru_wa[l], lru_wx[l]], axis=-1).astype(BF16)
        lru_params = (lru_conv_w[l], row(lru_conv_b[l]), lru_gates, row(lru_ba[l]), row(lru_bx[l]),
                      row(lru_lambda[l]))
        u, conv_act, y_lru = _mixer_in(xt, row(mix_norm_g[l]), w_in_b, l, conv_params, lru_params, seq_len=s)
        y_attn = _sb_attention(u.reshape(b, s, U_WIDTH))
        xt = _out_proj(xt, conv_act, y_attn.reshape(t, -1), y_lru, u, conv_pw_w[l].astype(BF16),
                       row(out_norm_conv[l]), row(out_norm_attn[l]), row(out_norm_lru[l]), w_out_b, l)
        kv = _norm_matmul(memt, row(mem_norm_g[l]), wkv_b, l, tm=256, tn=1024,
                          name="mem_kv").reshape(b, m, 2 * XATTN_WIDTH)
        xt = _xattn(xt.reshape(b, s, d), row(xattn_norm_g[l]), wq_b, kv, wo_b, row(final_norm_g), l,
                    final_norm=(l == depth - 1)).reshape(t, d)
    return xt.reshape(b, s, d)
```

```python
import functools

import jax
import jax.numpy as jnp
from jax import lax
from jax.experimental import pallas as pl
from jax.experimental.pallas import tpu as pltpu

F32 = jnp.float32
BF16 = jnp.bfloat16

D_MODEL = 2048
CONV_WIDTH = 512
CONV_KERNEL = 31
HEAD_DIM = 128
ATTN_WIDTH = 1024
ATTN_HEADS = 8
LRU_WIDTH = 512
LRU_HEADS = 4
LRU_CONV_KERNEL = 4
LRU_C = 8.0
XATTN_HEADS = 4
XATTN_WIDTH = 512
IN_WIDTH = 3 * CONV_WIDTH + 4 * ATTN_WIDTH + 2 * LRU_WIDTH

COL_BLOCK = 512
N_COL_BLOCKS = IN_WIDTH // COL_BLOCK
W_BLOCK_LRU_X = 11
N_BRANCH_BLOCKS = 3
U_WIDTH = IN_WIDTH - N_BRANCH_BLOCKS * COL_BLOCK
UCOL_CGATE, UCOL_Q, UCOL_K, UCOL_V, UCOL_AGATE, UCOL_RGATE = 0, 512, 1536, 2560, 3584, 4608

VMEM_LIMIT_BYTES = 56 * 1024 * 1024
SUBLANES = 8
LANES = 128

TM_MIXER_IN = 1024
TM_OUT_PROJ = 512
TM_XATTN = 1024

RMS_EPS = 1e-6
LN_EPS = 1e-5


def _params(*sem):
    return pltpu.CompilerParams(dimension_semantics=sem, vmem_limit_bytes=VMEM_LIMIT_BYTES)


def _rms_scale(x):
    return lax.rsqrt(jnp.mean(x * x, axis=-1, keepdims=True) + RMS_EPS)


def _softplus(x):
    return jnp.maximum(x, 0.0) + jnp.log(1.0 + jnp.exp(-jnp.abs(x)))


def _silu(x):
    return x * jax.nn.sigmoid(x)


def _norm_matmul_kernel(x_ref, g_ref, w_ref, o_ref, h_ref):
    @pl.when(pl.program_id(1) == 0)
    def _():
        x = x_ref[...]
        h_ref[...] = (x * _rms_scale(x) * g_ref[...]).astype(BF16)

    o_ref[...] = jnp.dot(h_ref[...], w_ref[...], preferred_element_type=F32).astype(o_ref.dtype)


def _norm_matmul(x, g, w, layer, *, tm, tn, name):
    t, d = x.shape
    n = w.shape[2]
    return pl.pallas_call(
        _norm_matmul_kernel,
        out_shape=jax.ShapeDtypeStruct((t, n), BF16),
        grid=(t // tm, n // tn),
        in_specs=[
            pl.BlockSpec((tm, d), lambda i, j: (i, 0)),
            pl.BlockSpec((1, d), lambda i, j: (0, 0)),
            pl.BlockSpec((None, d, tn), lambda i, j: (layer, 0, j)),
        ],
        out_specs=pl.BlockSpec((tm, tn), lambda i, j: (i, j)),
        scratch_shapes=[pltpu.VMEM((tm, d), BF16)],
        compiler_params=_params("parallel", "arbitrary"),
        name=name,
    )(x, g, w)


CONV_HALO = 32
CONV_ROW_CHUNK = 128
BRANCH_STEP_ROWS = 128
LRU_HALO = SUBLANES


def _depthwise_conv_rows(buf_ref, dww_ref, dwb_ref, r0):
    base = CONV_HALO - (CONV_KERNEL - 1)
    rows = CONV_ROW_CHUNK + SUBLANES
    n_m = (base + CONV_KERNEL - 1) // SUBLANES + 1
    out = []
    for lb in range(CONV_WIDTH // LANES):
        lanes = slice(lb * LANES, (lb + 1) * LANES)
        window = buf_ref[pl.ds(r0, rows + SUBLANES * (n_m - 1)), lanes]
        a = None
        for rho in reversed(range(SUBLANES)):
            q = None
            for m in range(n_m):
                k = SUBLANES * m + rho - base
                if 0 <= k < CONV_KERNEL:
                    term = dww_ref[k:k + 1, lanes] * window[SUBLANES * m:SUBLANES * m + rows]
                    q = term if q is None else q + term
            a = q if a is None else q + pltpu.roll(a, rows - 1, 0)
        out.append(a[:CONV_ROW_CHUNK] + dwb_ref[:, lanes])
    return jnp.concatenate(out, axis=1)


def _conv_branch_rows(cbuf_ref, dww_ref, dwb_ref, lng_ref, lnb_ref, r0):
    u = jnp.concatenate([_depthwise_conv_rows(cbuf_ref, dww_ref, dwb_ref, r0 + c * CONV_ROW_CHUNK)
                         for c in range(BRANCH_STEP_ROWS // CONV_ROW_CHUNK)], axis=0)
    mu = jnp.mean(u, axis=-1, keepdims=True)
    uc = u - mu
    var = jnp.mean(uc * uc, axis=-1, keepdims=True)
    y = uc * lax.rsqrt(var + LN_EPS) * lng_ref[...] + lnb_ref[...]
    return _silu(y).astype(BF16)


def _lru_branch_rows(rbuf_ref, cw_ref, cb_ref, wg_ref, ba_ref, bx_ref, lam_ref, hstate_ref, a_ref, b_ref,
                     yl_ref, r0):
    w = LRU_WIDTH
    n = BRANCH_STEP_ROWS
    window = rbuf_ref[pl.ds(r0, n + LRU_HALO), :]
    xc = jnp.broadcast_to(cb_ref[...], (n, w))
    for k in range(LRU_CONV_KERNEL):
        shift = LRU_HALO - (LRU_CONV_KERNEL - 1) + k
        if shift % SUBLANES == 0:
            tap = window[shift:shift + n]
        else:
            tap = pltpu.roll(window, n + LRU_HALO - shift, 0)[:n]
        xc = xc + cw_ref[k:k + 1, :] * tap

    xcb = xc.astype(BF16)
    hd = w // LRU_HEADS
    pre = [jnp.dot(xcb[:, hix * hd:(hix + 1) * hd], wg_ref[hix], preferred_element_type=F32)
           for hix in range(LRU_HEADS)]
    r = jax.nn.sigmoid(jnp.concatenate([p[:, :hd] for p in pre], axis=-1) + ba_ref[...])
    gate = jax.nn.sigmoid(jnp.concatenate([p[:, hd:] for p in pre], axis=-1) + bx_ref[...])
    log_a = (-LRU_C) * r * _softplus(-lam_ref[...])
    a = jnp.exp(log_a)
    a_ref[...] = a
    b_ref[...] = jnp.sqrt(jnp.tanh(-log_a) * (1.0 + a * a)) * (gate * xc)

    row = lax.broadcasted_iota(jnp.int32, (SUBLANES, w), 0)
    h_prev = hstate_ref[...]
    for g in range(n // SUBLANES):
        rows = slice(g * SUBLANES, (g + 1) * SUBLANES)
        ag = a_ref[rows, :]
        bg = b_ref[rows, :]
        d = 1
        while d < SUBLANES:
            a_sh = pltpu.roll(ag, d, 0)
            b_sh = pltpu.roll(bg, d, 0)
            m = row >= d
            bg = jnp.where(m, ag * b_sh + bg, bg)
            ag = jnp.where(m, ag * a_sh, ag)
            d *= 2
        h = ag * h_prev + bg
        b_ref[rows, :] = h
        h_prev = jnp.broadcast_to(h[SUBLANES - 1:SUBLANES, :], (SUBLANES, w))
    hstate_ref[...] = h_prev
    yl_ref[pl.ds(r0, n), :] = b_ref[...].astype(yl_ref.dtype)


def _mixer_in_kernel(x0_ref, x1_ref, x2_ref, x3_ref, g_ref, w_ref,
                     dww_ref, dwb_ref, lng_ref, lnb_ref,
                     cw_ref, cb_ref, wg_ref, ba_ref, bx_ref, lam_ref,
                     u_ref, yc_ref, yl_ref,
                     h_ref, val_ref, cbuf_ref, rbuf_ref, hstate_ref, a_ref, b_ref, *, tiles_per_seq):
    i = pl.program_id(0)
    j = pl.program_id(1)
    x_refs = (x0_ref, x1_ref, x2_ref, x3_ref)
    tm = x0_ref.shape[0]
    first_branch_step = N_BRANCH_BLOCKS
    n_branch_steps = tm // BRANCH_STEP_ROWS

    def project():
        return jnp.dot(h_ref[...], w_ref[...], preferred_element_type=F32)

    @pl.when(j == 0)
    def _():
        @pl.when(i % tiles_per_seq == 0)
        def _():
            cbuf_ref[0:CONV_HALO, :] = jnp.zeros((CONV_HALO, CONV_WIDTH), F32)
            rbuf_ref[0:LRU_HALO, :] = jnp.zeros((LRU_HALO, LRU_WIDTH), F32)
            hstate_ref[...] = jnp.zeros_like(hstate_ref)
            cbuf_ref[CONV_HALO + tm:CONV_HALO + tm + SUBLANES, :] = jnp.zeros((SUBLANES, CONV_WIDTH), F32)

        @pl.when(i % tiles_per_seq != 0)
        def _():
            cbuf_ref[0:CONV_HALO, :] = cbuf_ref[tm:tm + CONV_HALO, :]
            rbuf_ref[0:LRU_HALO, :] = rbuf_ref[tm:tm + LRU_HALO, :]

        d = h_ref.shape[1]
        cw = d // len(x_refs)
        sq = None
        for xr in x_refs:
            xc = xr[...]
            part = jnp.sum(xc * xc, axis=-1, keepdims=True)
            sq = part if sq is None else sq + part
        inv = lax.rsqrt(sq * (1.0 / d) + RMS_EPS)
        for c, xr in enumerate(x_refs):
            h_ref[:, c * cw:(c + 1) * cw] = (xr[...] * inv * g_ref[:, c * cw:(c + 1) * cw]).astype(BF16)
        val_ref[...] = project()

    @pl.when(j == 1)
    def _():
        cbuf_ref[CONV_HALO:CONV_HALO + tm, :] = val_ref[...] * jax.nn.sigmoid(project())

    @pl.when(j == 2)
    def _():
        rbuf_ref[LRU_HALO:LRU_HALO + tm, :] = project()

    @pl.when(jnp.logical_and(j >= first_branch_step, j < first_branch_step + n_branch_steps))
    def _():
        r0 = pl.multiple_of((j - first_branch_step) * BRANCH_STEP_ROWS, BRANCH_STEP_ROWS)
        _lru_branch_rows(rbuf_ref, cw_ref, cb_ref, wg_ref, ba_ref, bx_ref, lam_ref, hstate_ref, a_ref, b_ref,
                         yl_ref, r0)
        yc_ref[pl.ds(r0, BRANCH_STEP_ROWS), :] = _conv_branch_rows(cbuf_ref, dww_ref, dwb_ref, lng_ref, lnb_ref, r0)
        u_ref[...] = project().astype(u_ref.dtype)

    @pl.when(j >= first_branch_step + n_branch_steps)
    def _():
        u_ref[...] = project().astype(u_ref.dtype)


X_CHUNKS = 4


def _mixer_in(x, g, w, layer, conv_params, lru_params, *, seq_len):
    t, d = x.shape
    tm = TM_MIXER_IN
    n_tiles = t // tm
    assert seq_len % tm == 0 and tm % BRANCH_STEP_ROWS == 0 and BRANCH_STEP_ROWS % CONV_ROW_CHUNK == 0
    assert N_BRANCH_BLOCKS + tm // BRANCH_STEP_ROWS <= N_COL_BLOCKS and d % X_CHUNKS == 0
    dww, dwb, lng, lnb = conv_params
    cw, cb, wg, ba, bx, lam = lru_params
    hd = LRU_WIDTH // LRU_HEADS

    def w_block(i, j):
        blk = jnp.where(j == 2, W_BLOCK_LRU_X, jnp.where(jnp.logical_and(j > 2, j <= W_BLOCK_LRU_X), j - 1, j))
        return (layer, 0, blk)

    def x_chunk(c):
        def index(i, j):
            ahead = (j >= N_COL_BLOCKS - X_CHUNKS + c).astype(jnp.int32)
            return (jnp.minimum(i + ahead, n_tiles - 1), c)
        return pl.BlockSpec((tm, d // X_CHUNKS), index)

    const = lambda shape: pl.BlockSpec(shape, lambda i, j: (0,) * len(shape))
    return pl.pallas_call(
        functools.partial(_mixer_in_kernel, tiles_per_seq=seq_len // tm),
        out_shape=(jax.ShapeDtypeStruct((t, U_WIDTH), BF16),
                   jax.ShapeDtypeStruct((t, CONV_WIDTH), BF16),
                   jax.ShapeDtypeStruct((t, LRU_WIDTH), BF16)),
        grid=(n_tiles, N_COL_BLOCKS),
        in_specs=[
            *[x_chunk(c) for c in range(X_CHUNKS)],
            const((1, d)),
            pl.BlockSpec((None, d, COL_BLOCK), w_block),
            const((CONV_KERNEL, CONV_WIDTH)), const((1, CONV_WIDTH)), const((1, CONV_WIDTH)),
            const((1, CONV_WIDTH)),
            const((LRU_CONV_KERNEL, LRU_WIDTH)), const((1, LRU_WIDTH)),
            const((LRU_HEADS, hd, 2 * hd)), const((1, LRU_WIDTH)), const((1, LRU_WIDTH)), const((1, LRU_WIDTH)),
        ],
        out_specs=(
            pl.BlockSpec((tm, COL_BLOCK), lambda i, j: (i, jnp.maximum(j - N_BRANCH_BLOCKS, 0))),
            pl.BlockSpec((tm, CONV_WIDTH), lambda i, j: (i, 0)),
            pl.BlockSpec((tm, LRU_WIDTH), lambda i, j: (i, 0)),
        ),
        scratch_shapes=[
            pltpu.VMEM((tm, d), BF16),
            pltpu.VMEM((tm, CONV_WIDTH), F32),
            pltpu.VMEM((CONV_HALO + tm + SUBLANES, CONV_WIDTH), F32),
            pltpu.VMEM((LRU_HALO + tm, LRU_WIDTH), F32),
            pltpu.VMEM((SUBLANES, LRU_WIDTH), F32),
            pltpu.VMEM((BRANCH_STEP_ROWS, LRU_WIDTH), F32),
            pltpu.VMEM((BRANCH_STEP_ROWS, LRU_WIDTH), F32),
        ],
        compiler_params=_params("arbitrary", "arbitrary"),
        name="mixer_in",
    )(*([x] * X_CHUNKS), g, w, dww, dwb, lng, lnb, cw, cb, wg, ba, bx, lam)


SB_T = 128
SB_NARROW = 32
SB_EAGER = 3
SB_LOG_WEIGHT_FLOOR = -104.0
SB_DONE = -1e30


def _sb_split(log_1mb):
    hi = log_1mb.astype(BF16)
    lo = (log_1mb - hi.astype(F32)).astype(BF16)
    return jnp.concatenate([hi, lo], axis=1)


def _sb_attn_kernel(q_ref, k_ref, v_ref, uo_ref, o_ref, acc_ref, carry_ref):
    s_len = q_ref.shape[0]
    scale = HEAD_DIM ** -0.5
    t_sz, n_lo = SB_T, SB_NARROW
    n_tiles = s_len // t_sz
    causal = (lax.broadcasted_iota(jnp.int32, (t_sz, t_sz), 1)
              < lax.broadcasted_iota(jnp.int32, (t_sz, t_sz), 0))

    def tile(i):
        return slice(i * t_sz, (i + 1) * t_sz)

    def neg_scores(q, k):
        zn = lax.dot_general(q, k, (((1,), (1,)), ((), ())), preferred_element_type=F32) * (-scale)
        log_1mb = jnp.minimum(zn, 0.0) - jnp.log(1.0 + jnp.exp(-jnp.abs(zn)))
        return zn, log_1mb

    def eager():
        zns, ls, zn0s, l0s, lhs = [], [], [], [], []
        for r in range(n_tiles):
            q = q_ref[tile(r), :]
            first = max(r - 1, 0)
            zn, log_1mb = neg_scores(q, k_ref[first * t_sz:(r + 1) * t_sz, :])
            pieces = [_sb_split(jnp.where(causal, log_1mb[:, -t_sz:], 0.0))]
            if r >= 1:
                pieces.insert(0, _sb_split(log_1mb[:, :t_sz]))
            if r >= 2:
                zn0, log_1mb0 = neg_scores(q[:n_lo], k_ref[tile(r - 2), :])
                pieces.insert(0, _sb_split(log_1mb0))
                zn0s.append(zn0)
                l0s.append(log_1mb0)
            else:
                zn0s.append(None)
                l0s.append(None)
            lhs.append(jnp.concatenate(pieces, axis=0))
            zns.append(zn)
            ls.append(log_1mb)
        sums = [jnp.dot(x, uo_ref[...], preferred_element_type=F32) for x in lhs]
        m_lo = m_hi = None
        for r in range(n_tiles):
            zn, log_1mb = zns[r], ls[r]
            s2 = sums[r][-t_sz:]
            w2 = jnp.where(causal, jnp.exp((log_1mb[:, -t_sz:] - zn[:, -t_sz:]) + s2[:, :t_sz]), 0.0)
            carry = s2[:, t_sz:]
            ws = [w2.astype(BF16)]
            if r >= 1:
                s1 = sums[r][-2 * t_sz:-t_sz]
                w1 = jnp.exp((log_1mb[:, :t_sz] - zn[:, :t_sz]) + (carry + s1[:, :t_sz]))
                carry = carry + s1[:, t_sz:]
                ws.insert(0, w1.astype(BF16))
            first = max(r - 1, 0)
            acc = jnp.dot(jnp.concatenate(ws, axis=1), v_ref[first * t_sz:(r + 1) * t_sz, :],
                          preferred_element_type=F32)
            if r >= 2:
                s0 = sums[r][:n_lo]
                w0 = jnp.exp((l0s[r] - zn0s[r]) + (carry[:n_lo] + s0[:, :t_sz]))
                acc_lo = jnp.dot(w0.astype(BF16), v_ref[tile(r - 2), :], preferred_element_type=F32)
                acc = jnp.concatenate([acc[:n_lo] + acc_lo, acc[n_lo:]], axis=0)
                carry_lo = carry[:n_lo] + s0[:, t_sz:]
                carry_ref[r, 0:n_lo, :] = carry_lo
                carry_ref[r, n_lo:t_sz, :] = carry[n_lo:]
                m_lo = carry_lo if m_lo is None else jnp.maximum(m_lo, carry_lo)
                m_hi = carry[n_lo:] if m_hi is None else jnp.maximum(m_hi, carry[n_lo:])
            acc_ref[r] = acc
            o_ref[tile(r), :] = acc.astype(o_ref.dtype)
        return jnp.max(m_lo), jnp.max(m_hi)

    def rest_of_third_block():
        n_hi = t_sz - n_lo
        tiles = range(2, n_tiles)
        zns, ls, lhs = {}, {}, []
        for r in tiles:
            zns[r], ls[r] = neg_scores(q_ref[r * t_sz + n_lo:(r + 1) * t_sz, :], k_ref[tile(r - 2), :])
            lhs.append(_sb_split(ls[r]))
        sums = jnp.dot(jnp.concatenate(lhs, axis=0), uo_ref[...], preferred_element_type=F32)
        m = None
        for n, r in enumerate(tiles):
            sr = sums[n * n_hi:(n + 1) * n_hi]
            carry = carry_ref[r, n_lo:t_sz, :]
            w = jnp.exp((ls[r] - zns[r]) + (carry + sr[:, :t_sz]))
            acc_ref[r, n_lo:t_sz, :] += jnp.dot(w.astype(BF16), v_ref[tile(r - 2), :], preferred_element_type=F32)
            carry = carry + sr[:, t_sz:]
            carry_ref[r, n_lo:t_sz, :] = carry
            if r >= SB_EAGER:
                m = carry if m is None else jnp.maximum(m, carry)
        return jnp.max(m)

    def step(d):
        tiles = range(SB_EAGER, n_tiles)
        zns, ls, starts, lhs = {}, {}, {}, []
        for r in tiles:
            starts[r] = pl.multiple_of(jnp.maximum(r - d, 0) * t_sz, t_sz)
            zns[r], ls[r] = neg_scores(q_ref[tile(r), :], k_ref[pl.ds(starts[r], t_sz), :])
            lhs.append(_sb_split(ls[r]))
        sums = jnp.dot(jnp.concatenate(lhs, axis=0), uo_ref[...], preferred_element_type=F32)
        m = None
        for n, r in enumerate(tiles):
            sr = sums[n * t_sz:(n + 1) * t_sz]
            w = jnp.exp((ls[r] - zns[r]) + (carry_ref[r] + sr[:, :t_sz]))
            w = jnp.where(r >= d, w, 0.0)
            acc_ref[r] += jnp.dot(w.astype(BF16), v_ref[pl.ds(starts[r], t_sz), :], preferred_element_type=F32)
            carry = carry_ref[r] + sr[:, t_sz:]
            carry_ref[r] = carry
            left = jnp.where(r > d, carry, SB_DONE)
            m = left if m is None else jnp.maximum(m, left)
        return jnp.max(m)

    def more(c):
        d, m = c
        return jnp.logical_and(d <= n_tiles - 1, m > SB_LOG_WEIGHT_FLOOR)

    m_lo, m_hi = eager()
    m = lax.cond(m_hi > SB_LOG_WEIGHT_FLOOR,
                 lambda: jnp.maximum(m_lo, rest_of_third_block()), lambda: m_lo)
    lax.while_loop(more, lambda c: (c[0] + 1, step(c[0])), (jnp.int32(SB_EAGER), m))

    @pl.when(jnp.logical_or(m_hi > SB_LOG_WEIGHT_FLOOR, m > SB_LOG_WEIGHT_FLOOR))
    def _():
        for r in range(2, n_tiles):
            o_ref[tile(r), :] = acc_ref[r].astype(o_ref.dtype)


def _sb_attention(u3):
    b, s, _ = u3.shape
    dh = HEAD_DIM
    assert s % SB_T == 0 and s // SB_T > SB_EAGER and SB_NARROW % 16 == 0
    row = lax.broadcasted_iota(jnp.int32, (2 * SB_T, 2 * SB_T), 0) % SB_T
    col = lax.broadcasted_iota(jnp.int32, (2 * SB_T, 2 * SB_T), 1)
    uo = jnp.where((col >= SB_T) | (row > col), 1.0, 0.0).astype(BF16)

    def head_spec(col0):
        return pl.BlockSpec((None, s, dh), lambda bi, hi: (bi, 0, col0 // dh + hi))

    n_tiles = s // SB_T
    return pl.pallas_call(
        _sb_attn_kernel,
        out_shape=jax.ShapeDtypeStruct((b, s, ATTN_WIDTH), BF16),
        grid=(b, ATTN_HEADS),
        in_specs=[head_spec(UCOL_Q), head_spec(UCOL_K), head_spec(UCOL_V),
                  pl.BlockSpec((2 * SB_T, 2 * SB_T), lambda bi, hi: (0, 0))],
        out_specs=pl.BlockSpec((None, s, dh), lambda bi, hi: (bi, 0, hi)),
        scratch_shapes=[pltpu.VMEM((n_tiles, SB_T, dh), F32), pltpu.VMEM((n_tiles, SB_T, SB_T), F32)],
        compiler_params=_params("parallel", "parallel"),
        name="sb_attention",
    )(u3, u3, u3, uo)


def _out_proj_kernel(x_ref, yc_ref, ya_ref, yl_ref, gc_ref, ga0_ref, ga1_ref, gl_ref,
                     pw_ref, nc_ref, na_ref, nl_ref, w_ref, o_ref):
    def normed(y, n_ref):
        return y * _rms_scale(y) * n_ref[...]

    def gated(yn, gate_ref):
        return (yn * _silu(gate_ref[...].astype(F32))).astype(BF16)

    half = ATTN_WIDTH // 2
    y_conv = jnp.dot(yc_ref[...], pw_ref[...], preferred_element_type=F32)
    ya = normed(ya_ref[...].astype(F32), na_ref)
    parts = [
        (gated(normed(y_conv, nc_ref), gc_ref), 0),
        (jnp.concatenate([gated(ya[:, :half], ga0_ref), gated(ya[:, half:], ga1_ref)], axis=-1), CONV_WIDTH),
        (gated(normed(yl_ref[...].astype(F32), nl_ref), gl_ref), CONV_WIDTH + ATTN_WIDTH),
    ]
    acc = x_ref[...]
    for y, r0 in parts:
        acc = acc + jnp.dot(y, w_ref[r0:r0 + y.shape[1], :], preferred_element_type=F32)
    o_ref[...] = acc


def _out_proj(x, yc, ya, yl, u, pw, nc, na, nl, w, layer):
    t, d = x.shape
    tm = TM_OUT_PROJ
    gw = COL_BLOCK
    return pl.pallas_call(
        _out_proj_kernel,
        out_shape=jax.ShapeDtypeStruct((t, d), F32),
        grid=(t // tm,),
        in_specs=[
            pl.BlockSpec((tm, d), lambda i: (i, 0)),
            pl.BlockSpec((tm, CONV_WIDTH), lambda i: (i, 0)),
            pl.BlockSpec((tm, ATTN_WIDTH), lambda i: (i, 0)),
            pl.BlockSpec((tm, LRU_WIDTH), lambda i: (i, 0)),
            pl.BlockSpec((tm, gw), lambda i: (i, UCOL_CGATE // gw)),
            pl.BlockSpec((tm, gw), lambda i: (i, UCOL_AGATE // gw)),
            pl.BlockSpec((tm, gw), lambda i: (i, UCOL_AGATE // gw + 1)),
            pl.BlockSpec((tm, gw), lambda i: (i, UCOL_RGATE // gw)),
            pl.BlockSpec((CONV_WIDTH, CONV_WIDTH), lambda i: (0, 0)),
            pl.BlockSpec((1, CONV_WIDTH), lambda i: (0, 0)),
            pl.BlockSpec((1, ATTN_WIDTH), lambda i: (0, 0)),
            pl.BlockSpec((1, LRU_WIDTH), lambda i: (0, 0)),
            pl.BlockSpec((None, d, d), lambda i: (layer, 0, 0)),
        ],
        out_specs=pl.BlockSpec((tm, d), lambda i: (i, 0)),
        compiler_params=_params("parallel"),
        name="out_proj",
    )(x, yc, ya, yl, u, u, u, u, pw, nc, na, nl, w)


def _xattn_kernel(x_ref, g_ref, wq_ref, k_ref, v_ref, wo_ref, fg_ref, o_ref, *, final_norm):
    x = x_ref[...]
    h = (x * _rms_scale(x) * g_ref[...]).astype(BF16)
    q = jnp.dot(h, wq_ref[...], preferred_element_type=F32).astype(BF16)
    dh = XATTN_WIDTH // XATTN_HEADS
    scale = dh ** -0.5
    heads = []
    for n in range(XATTN_HEADS):
        qh = q[:, n * dh:(n + 1) * dh]
        kh = k_ref[:, n * dh:(n + 1) * dh]
        vh = v_ref[:, n * dh:(n + 1) * dh]
        s = lax.dot_general(qh, kh, (((1,), (1,)), ((), ())), preferred_element_type=F32) * scale
        e = jnp.exp(s - jnp.max(s, axis=-1, keepdims=True))
        p = (e / jnp.sum(e, axis=-1, keepdims=True)).astype(BF16)
        heads.append(jnp.dot(p, vh, preferred_element_type=F32).astype(BF16))
    acc = x + jnp.dot(jnp.concatenate(heads, axis=-1), wo_ref[...], preferred_element_type=F32)
    if final_norm:
        acc = acc * _rms_scale(acc) * fg_ref[...]
    o_ref[...] = acc


def _xattn(x3, g, wq, kv, wo, fg, layer, *, final_norm):
    b, s, d = x3.shape
    m = kv.shape[1]
    tm = TM_XATTN
    xw = XATTN_WIDTH
    return pl.pallas_call(
        functools.partial(_xattn_kernel, final_norm=final_norm),
        out_shape=jax.ShapeDtypeStruct((b, s, d), F32),
        grid=(b, s // tm),
        in_specs=[
            pl.BlockSpec((None, tm, d), lambda bi, ti: (bi, ti, 0)),
            pl.BlockSpec((1, d), lambda bi, ti: (0, 0)),
            pl.BlockSpec((None, d, xw), lambda bi, ti: (layer, 0, 0)),
            pl.BlockSpec((None, m, xw), lambda bi, ti: (bi, 0, 0)),
            pl.BlockSpec((None, m, xw), lambda bi, ti: (bi, 0, 1)),
            pl.BlockSpec((None, xw, d), lambda bi, ti: (layer, 0, 0)),
            pl.BlockSpec((1, d), lambda bi, ti: (0, 0)),
        ],
        out_specs=pl.BlockSpec((None, tm, d), lambda bi, ti: (bi, ti, 0)),
        compiler_params=_params("parallel", "parallel"),
        name="xattn_final" if final_norm else "xattn",
    )(x3, g, wq, kv, kv, wo, fg)


def kernel(x, mem, mix_norm_g, w_in, conv_dw_w, conv_dw_b, conv_ln_g, conv_ln_b, conv_pw_w,
           lru_conv_w, lru_conv_b, lru_wa, lru_ba, lru_wx, lru_bx, lru_lambda,
           out_norm_conv, out_norm_attn, out_norm_lru, w_out,
           xattn_norm_g, mem_norm_g, xattn_wq, xattn_wkv, xattn_wo, final_norm_g):
    b, s, d = x.shape
    m = mem.shape[1]
    depth = w_in.shape[0]
    t = b * s
    row = lambda a: a.reshape(1, -1).astype(F32)

    xt = x.reshape(t, d)
    memt = mem.reshape(b * m, d)
    w_in_b, w_out_b = w_in.astype(BF16), w_out.astype(BF16)
    wq_b, wkv_b, wo_b = xattn_wq.astype(BF16), xattn_wkv.astype(BF16), xattn_wo.astype(BF16)
    for l in range(depth):
        conv_params = (conv_dw_w[l], row(conv_dw_b[l]), row(conv_ln_g[l]), row(conv_ln_b[l]))
        lru_gates = jnp.concatenate([lru_wa[l], lru_wx[l]], axis=-1).astype(BF16)
        lru_params = (lru_conv_w[l], row(lru_conv_b[l]), lru_gates, row(lru_ba[l]), row(lru_bx[l]),
                      row(lru_lambda[l]))
        u, conv_act, y_lru = _mixer_in(xt, row(mix_norm_g[l]), w_in_b, l, conv_params, lru_params, seq_len=s)
        y_attn = _sb_attention(u.reshape(b, s, U_WIDTH))
        xt = _out_proj(xt, conv_act, y_attn.reshape(t, -1), y_lru, u, conv_pw_w[l].astype(BF16),
                       row(out_norm_conv[l]), row(out_norm_attn[l]), row(out_norm_lru[l]), w_out_b, l)
        kv = _norm_matmul(memt, row(mem_norm_g[l]), wkv_b, l, tm=256, tn=1024,
                          name="mem_kv").reshape(b, m, 2 * XATTN_WIDTH)
        xt = _xattn(xt.reshape(b, s, d), row(xattn_norm_g[l]), wq_b, kv, wo_b, row(final_norm_g), l,
                    final_norm=(l == depth - 1)).reshape(t, d)
    return xt.reshape(b, s, d)
```
